```python
import jax, jax.numpy as jnp
from jax import lax
import numpy as np

D_MODEL = 1024
BATCH = 8
SEQ = 2048
DEPTH = 4
DEC_BATCH = 128
DEC_SEQ = 8
PAST_LEN = 16384
PAGE_SIZE = 128

POOL_WIDTH = D_MODEL // 4
POOL_WINDOWS = (2, 4, 8, 16)
POOL_GROUPS = len(POOL_WINDOWS)
POOL_GROUP_DIM = POOL_WIDTH // POOL_GROUPS
POOL_BUF = max(POOL_WINDOWS) - 1
SGU_WIDTH = D_MODEL // 4
SGU_HEADS = 4
SGU_HEAD_DIM = SGU_WIDTH // SGU_HEADS
SGU_CHUNK = 128
GLA_VWIDTH = D_MODEL // 2
GLA_KWIDTH = GLA_VWIDTH // 2
GLA_HEADS = 4
GLA_DK = GLA_KWIDTH // GLA_HEADS
GLA_DV = GLA_VWIDTH // GLA_HEADS
GLA_GATE_RANK = 16
GLA_GATE_NORM = 16.0
GLA_CHUNK = 64
MIX_WIDTH = POOL_WIDTH + SGU_WIDTH + GLA_VWIDTH
IN_SIZES = (POOL_WIDTH, SGU_WIDTH, SGU_WIDTH, GLA_KWIDTH, GLA_KWIDTH, GLA_VWIDTH, GLA_VWIDTH, GLA_GATE_RANK)
IN_WIDTH = sum(IN_SIZES)
D_FF = -(-8 * D_MODEL // (3 * 256)) * 256
EPS = 1e-6

kernel_name = "hybrid_pool_sgu_gla_decoder_step"


def rmsnorm(x, g):
    xf = x.astype(jnp.float32)
    y = xf * lax.rsqrt(jnp.mean(xf * xf, axis=-1, keepdims=True) + EPS)
    return (y * g.astype(jnp.float32)).astype(x.dtype)


def split_in(z):
    offs = [int(o) for o in np.cumsum(IN_SIZES)[:-1]]
    return jnp.split(z, offs, axis=-1)


def pool_mix(p, buf, start_pos, pool_w, pool_scale):
    B, T, _ = p.shape
    zrows = jnp.concatenate([buf.astype(p.dtype), p], axis=1)
    zf = zrows.astype(jnp.float32)
    cz = jnp.concatenate([jnp.zeros((B, 1, POOL_WIDTH), jnp.float32), jnp.cumsum(zf, axis=1)], axis=1)
    pos = start_pos + jnp.arange(T)
    pf = p.astype(jnp.float32)
    diffs = []
    for gi, w in enumerate(POOL_WINDOWS):
        sl = slice(gi * POOL_GROUP_DIM, (gi + 1) * POOL_GROUP_DIM)
        s = cz[:, POOL_BUF + 1:POOL_BUF + 1 + T, sl] - cz[:, POOL_BUF + 1 - w:POOL_BUF + 1 - w + T, sl]
        cnt = jnp.minimum(pos + 1, w).astype(jnp.float32)
        diffs.append(s / cnt[None, :, None] - pf[..., sl])
    d = jnp.stack(diffs, axis=2).astype(p.dtype)
    y = jnp.einsum('btgc,gcd->btgd', d, pool_w).reshape(B, T, POOL_WIDTH) * pool_scale
    new_buf = zrows[:, -POOL_BUF:]
    return y, new_buf


def sgu_mix(u, v, ws, b):
    B, T, H, Dh = v.shape
    n = -(-T // SGU_CHUNK)
    pad = n * SGU_CHUNK - T
    vp = jnp.pad(v, ((0, 0), (0, pad), (0, 0), (0, 0))).reshape(B, n, SGU_CHUNK, H, Dh)
    mask = jnp.tril(jnp.ones((SGU_CHUNK, SGU_CHUNK), dtype=bool))
    wm = jnp.where(mask[None], ws, jnp.zeros_like(ws))
    s = jnp.einsum('hij,bnjhd->bnihd', wm, vp) + b.T[None, None, :, :, None]
    s = s.reshape(B, n * SGU_CHUNK, H, Dh)[:, :T]
    return u * s


def gla_chunked(q, k, v, lg, s0):
    B, T, H, DK = q.shape
    DV = v.shape[-1]
    C = min(GLA_CHUNK, T)
    n = -(-T // C)
    pad = n * C - T

    def to_chunks(a):
        a = jnp.pad(a.astype(jnp.float32), ((0, 0), (0, pad), (0, 0), (0, 0)))
        return a.reshape(B, n, C, H, a.shape[-1]).transpose(1, 0, 3, 2, 4)

    qc, kc, vc, gc = to_chunks(q), to_chunks(k), to_chunks(v), to_chunks(lg)
    mask = jnp.tril(jnp.ones((C, C), dtype=bool))

    def step(S, inp):
        qi, ki, vi, gi = inp
        bcum = jnp.cumsum(gi, axis=2)
        blast = bcum[:, :, -1:, :]
        qe = qi * jnp.exp(bcum)
        ke = ki * jnp.exp(-bcum)
        kd = ki * jnp.exp(blast - bcum)
        att = jnp.where(mask, jnp.einsum('bhik,bhjk->bhij', qe, ke), 0.0)
        o = jnp.einsum('bhik,bhkv->bhiv', qe, S) + jnp.einsum('bhij,bhjv->bhiv', att, vi)
        S_new = jnp.exp(blast[:, :, 0, :])[..., None] * S + jnp.einsum('bhjk,bhjv->bhkv', kd, vi)
        return S_new, o

    S_fin, o = lax.scan(step, s0.astype(jnp.float32), (qc, kc, vc, gc))
    o = o.transpose(1, 0, 3, 2, 4).reshape(B, n * C, H, DV)[:, :T]
    return o, S_fin


def hybrid_layer(x, pool_buf, gla_s0, start_pos, attn_norm_g, w_in, pool_w, pool_scale,
                 sgu_norm_g, sgu_ws, sgu_b, gla_wa2, gla_ba, gla_norm_g, w_out,
                 ffn_norm_g, w_gate, w_up, w_down):
    B, T, _ = x.shape
    h = rmsnorm(x, attn_norm_g)
    z = h @ w_in
    p, u, v, q, k, vg, g, alow = split_in(z)
    a_out, new_buf = pool_mix(p, pool_buf, start_pos, pool_w, pool_scale)
    v_n = rmsnorm(v.reshape(B, T, SGU_HEADS, SGU_HEAD_DIM), sgu_norm_g)
    b_out = sgu_mix(u.reshape(B, T, SGU_HEADS, SGU_HEAD_DIM), v_n, sgu_ws, sgu_b).reshape(B, T, SGU_WIDTH)
    qh = q.reshape(B, T, GLA_HEADS, GLA_DK) * (GLA_DK ** -0.5)
    kh = k.reshape(B, T, GLA_HEADS, GLA_DK)
    vh = vg.reshape(B, T, GLA_HEADS, GLA_DV)
    lg = jax.nn.log_sigmoid((alow @ gla_wa2 + gla_ba).astype(jnp.float32)) / GLA_GATE_NORM
    lg = lg.reshape(B, T, GLA_HEADS, GLA_DK)
    o, s_new = gla_chunked(qh, kh, vh, lg, gla_s0)
    o = rmsnorm(o.astype(x.dtype), gla_norm_g) * jax.nn.silu(g.reshape(B, T, GLA_HEADS, GLA_DV))
    c_out = o.reshape(B, T, GLA_VWIDTH)
    x = x + jnp.concatenate([a_out, b_out, c_out], axis=-1) @ w_out
    hf = rmsnorm(x, ffn_norm_g)
    x = x + (jax.nn.silu(hf @ w_gate) * (hf @ w_up)) @ w_down
    return x, new_buf, s_new, v_n.reshape(B, T, SGU_WIDTH)


def setup_inputs(seed: int = 0) -> dict:
    key = jax.random.key(seed)
    ks = jax.random.split(key, 24)
    nrm = lambda k, shape: jax.random.normal(k, shape, jnp.float32)
    gain = lambda k, shape: 1.0 + 0.02 * nrm(k, shape)
    return {
        "x_prompt": nrm(ks[0], (BATCH, SEQ, D_MODEL)),
        "x_sample": nrm(ks[1], (DEC_BATCH, DEC_SEQ, D_MODEL)),
        "state_pool": nrm(ks[2], (DEPTH, DEC_BATCH, POOL_BUF, POOL_WIDTH)),
        "state_gla": 2.0 * nrm(ks[3], (DEPTH, DEC_BATCH, GLA_HEADS, GLA_DK, GLA_DV)),
        "attn_norm_g": gain(ks[4], (DEPTH, D_MODEL)),
        "w_in": nrm(ks[5], (DEPTH, D_MODEL, IN_WIDTH)) * D_MODEL ** -0.5,
        "pool_w": nrm(ks[6], (DEPTH, POOL_GROUPS, POOL_GROUP_DIM, POOL_GROUP_DIM)) * POOL_GROUP_DIM ** -0.5,
        "pool_scale": gain(ks[7], (DEPTH, POOL_WIDTH)),
        "sgu_norm_g": gain(ks[8], (DEPTH, SGU_HEADS, SGU_HEAD_DIM)),
        "sgu_ws": nrm(ks[9], (DEPTH, SGU_HEADS, SGU_CHUNK, SGU_CHUNK)) * SGU_CHUNK ** -0.5,
        "sgu_b": gain(ks[10], (DEPTH, SGU_HEADS, SGU_CHUNK)),
        "gla_wa2": nrm(ks[11], (DEPTH, GLA_GATE_RANK, GLA_KWIDTH)) * GLA_GATE_RANK ** -0.5,
        "gla_ba": 0.02 * nrm(ks[12], (DEPTH, GLA_KWIDTH)),
        "gla_norm_g": gain(ks[13], (DEPTH, GLA_DV)),
        "w_out": nrm(ks[14], (DEPTH, MIX_WIDTH, D_MODEL)) * MIX_WIDTH ** -0.5,
        "ffn_norm_g": gain(ks[15], (DEPTH, D_MODEL)),
        "w_gate": nrm(ks[16], (DEPTH, D_MODEL, D_FF)) * D_MODEL ** -0.5,
        "w_up": nrm(ks[17], (DEPTH, D_MODEL, D_FF)) * D_MODEL ** -0.5,
        "w_down": nrm(ks[18], (DEPTH, D_FF, D_MODEL)) * D_FF ** -0.5,
        "final_norm_g": gain(ks[19], (D_MODEL,)),
    }


def reference(x_prompt, x_sample, state_pool, state_gla, attn_norm_g, w_in, pool_w, pool_scale,
              sgu_norm_g, sgu_ws, sgu_b, gla_wa2, gla_ba, gla_norm_g, w_out,
              ffn_norm_g, w_gate, w_up, w_down, final_norm_g):
    xp, xs = x_prompt, x_sample
    Bp = xp.shape[0]
    pool_p, gla_p, pool_s, gla_s, sgu_s = [], [], [], [], []
    for l in range(DEPTH):
        lw = (attn_norm_g[l], w_in[l], pool_w[l], pool_scale[l], sgu_norm_g[l], sgu_ws[l], sgu_b[l],
              gla_wa2[l], gla_ba[l], gla_norm_g[l], w_out[l], ffn_norm_g[l], w_gate[l], w_up[l], w_down[l])
        buf0 = jnp.zeros((Bp, POOL_BUF, POOL_WIDTH), xp.dtype)
        s0 = jnp.zeros((Bp, GLA_HEADS, GLA_DK, GLA_DV), jnp.float32)
        xp, nb_p, ns_p, _ = hybrid_layer(xp, buf0, s0, 0, *lw)
        xs, nb_s, ns_s, v_s = hybrid_layer(xs, state_pool[l], state_gla[l], PAST_LEN, *lw)
        pool_p.append(nb_p.astype(state_pool.dtype))
        gla_p.append(ns_p.astype(state_gla.dtype))
        pool_s.append(nb_s.astype(state_pool.dtype))
        gla_s.append(ns_s.astype(state_gla.dtype))
        sgu_s.append(v_s)
    y_prompt = rmsnorm(xp, final_norm_g)
    y_sample = rmsnorm(xs, final_norm_g)
    return (y_prompt, y_sample, jnp.stack(pool_p), jnp.stack(gla_p), jnp.stack(pool_s), jnp.stack(gla_s), jnp.stack(sgu_s))
```

```python
import functools

import jax
import jax.numpy as jnp
from jax import lax
from jax.experimental import pallas as pl
from jax.experimental.pallas import tpu as pltpu

F32 = jnp.float32
BF16 = jnp.bfloat16

D_MODEL = 1024
DEPTH = 4
POOL_WIDTH = 256
POOL_WINDOWS = (2, 4, 8, 16)
POOL_GROUP_DIM = 64
POOL_BUF = 15
POOL_PAD = 16
SGU_WIDTH = 256
SGU_HEADS = 4
SGU_HEAD_DIM = 64
SGU_CHUNK = 128
GLA_HEADS = 4
GLA_DK = 64
GLA_DV = 128
GLA_KW = GLA_HEADS * GLA_DK
GLA_VW = GLA_HEADS * GLA_DV
GLA_RANK = 16
GLA_RANK_PAD = 128
GLA_GATE_NORM = 16.0
GLA_CHUNK = 64
D_FF = 2816
EPS = 1e-6

OFF_P, OFF_U, OFF_V, OFF_Q, OFF_K, OFF_VG, OFF_G, OFF_A = 0, 256, 512, 768, 1024, 1280, 1792, 2304
IN_MAIN = 2304
IN_PAD = IN_MAIN + GLA_RANK_PAD

VMEM_LIMIT = 56 * 1024 * 1024


def _rms(x, g):
    return x * lax.rsqrt(jnp.mean(x * x, axis=-1, keepdims=True) + EPS) * g


def _dot(a, b):
    return jnp.dot(a, b, preferred_element_type=F32)


def _dot_nt(a, b):
    return lax.dot_general(a, b, (((1,), (1,)), ((), ())), preferred_element_type=F32)


def _dot_tn(a, b):
    return lax.dot_general(a, b, (((0,), (0,)), ((), ())), preferred_element_type=F32)


def _split_bf16(x):
    hi = x.astype(BF16)
    lo = (x - hi.astype(F32)).astype(BF16)
    return hi, lo


def _mix_kernel(x_ref, pool0_ref, gla0_ref, ng_ref, win_ref, poolw_ref, pscale_ref,
                sgug_ref, sguw_ref, sgub_ref, wa2_ref, ba_ref, glag_ref, wout_ref,
                xo_ref, pool_o_ref, gla_o_ref, vn_o_ref,
                z_s, pz_s, lg_s, o_s, *, NS, T, pos0):
    j = pl.program_id(1)
    R = NS * T
    C = min(GLA_CHUNK, T)
    n_chunks = T // C

    x = x_ref[...]
    h = _rms(x, ng_ref[...]).astype(BF16)
    z_s[...] = _dot(h, win_ref[...])

    @pl.when(j == 0)
    def _():
        pz_s[:, 0:1, :] = jnp.zeros((NS, 1, POOL_WIDTH), F32)
        pz_s[:, 1:POOL_PAD, :] = pool0_ref[...]
        gla_o_ref[...] = gla0_ref[...]

    p3 = z_s[:, OFF_P:OFF_P + POOL_WIDTH].reshape(NS, T, POOL_WIDTH)
    pz_s[:, POOL_PAD:POOL_PAD + T, :] = p3
    e = pz_s[...]
    w2 = e + pltpu.roll(e, 1, 1)
    w4 = w2 + pltpu.roll(w2, 2, 1)
    w8 = w4 + pltpu.roll(w4, 4, 1)
    w16 = w8 + pltpu.roll(w8, 8, 1)
    grp = lax.broadcasted_iota(jnp.int32, (1, 1, POOL_WIDTH), 2) // POOL_GROUP_DIM
    wsum = jnp.where(grp == 0, w2, jnp.where(grp == 1, w4, jnp.where(grp == 2, w8, w16)))
    wsum = wsum[:, POOL_PAD:POOL_PAD + T, :]
    win = jnp.where(grp == 0, 2, jnp.where(grp == 1, 4, jnp.where(grp == 2, 8, 16)))
    pos = pos0 + j * T + lax.broadcasted_iota(jnp.int32, (1, T, 1), 1)
    cnt = jnp.minimum(pos + 1, win).astype(F32)
    d = (wsum / cnt - p3).reshape(R, POOL_WIDTH)
    a_out = _dot(d.astype(BF16), poolw_ref[...]) * pscale_ref[...]
    pool_o_ref[...] = pz_s[:, T + 1:T + POOL_PAD, :]
    if T >= POOL_PAD:
        pz_s[:, 0:POOL_PAD, :] = pz_s[:, T:T + POOL_PAD, :]

    v = z_s[:, OFF_V:OFF_V + SGU_WIDTH]
    u = z_s[:, OFF_U:OFF_U + SGU_WIDTH]
    r_i = lax.broadcasted_iota(jnp.int32, (SGU_WIDTH, SGU_WIDTH), 0) // SGU_HEAD_DIM
    c_i = lax.broadcasted_iota(jnp.int32, (SGU_WIDTH, SGU_WIDTH), 1) // SGU_HEAD_DIM
    head_ones = (r_i == c_i).astype(BF16)
    vsq_hi, vsq_lo = _split_bf16(v * v)
    ss = _dot(vsq_hi, head_ones) + _dot(vsq_lo, head_ones)
    vn = v * lax.rsqrt(ss * (1.0 / SGU_HEAD_DIM) + EPS) * sgug_ref[...]
    vn_o_ref[...] = vn
    lane_head = lax.broadcasted_iota(jnp.int32, (1, SGU_WIDTH), 1) // SGU_HEAD_DIM
    if T >= SGU_CHUNK:
        row = lax.broadcasted_iota(jnp.int32, (SGU_CHUNK, SGU_HEADS * SGU_CHUNK), 0)
        col = lax.broadcasted_iota(jnp.int32, (SGU_CHUNK, SGU_HEADS * SGU_CHUNK), 1)
        wcat = jnp.where((col & (SGU_CHUNK - 1)) <= row, sguw_ref[...], 0.0).astype(BF16)
        parts = []
        for c in range(R // SGU_CHUNK):
            vc = vn[c * SGU_CHUNK:(c + 1) * SGU_CHUNK]
            stack = jnp.concatenate(
                [jnp.where(lane_head == hh, vc, 0.0) for hh in range(SGU_HEADS)], axis=0)
            parts.append(_dot(wcat, stack.astype(BF16)) + sgub_ref[...])
        s_gate = jnp.concatenate(parts, axis=0)
    else:
        vn3 = vn.reshape(NS, T, SGU_WIDTH)
        row = lax.broadcasted_iota(jnp.int32, (T, SGU_WIDTH), 0)
        s3 = jnp.zeros((NS, T, SGU_WIDTH), F32) + sgub_ref[...][None]
        for jj in range(T):
            coef = jnp.where(row >= jj, sguw_ref[jj], 0.0)
            s3 = s3 + vn3[:, jj:jj + 1, :] * coef[None]
        s_gate = s3.reshape(R, SGU_WIDTH)
    b_out = u * s_gate

    alow = z_s[:, OFF_A:OFF_A + GLA_RANK_PAD].astype(BF16)
    xg = _dot(alow, wa2_ref[...]) + ba_ref[...]
    lg_s[...] = (jnp.minimum(xg, 0.0) - jnp.log1p(jnp.exp(-jnp.abs(xg)))) * (1.0 / GLA_GATE_NORM)

    tri = (lax.broadcasted_iota(jnp.int32, (C, C), 0)
           >= lax.broadcasted_iota(jnp.int32, (C, C), 1)).astype(BF16)
    ones_c = jnp.ones((C, GLA_DV), BF16)
    khead = lax.broadcasted_iota(jnp.int32, (1, GLA_KW), 1) // GLA_DK
    vhead = lax.broadcasted_iota(jnp.int32, (1, GLA_VW), 1) // GLA_DV
    arow = lax.broadcasted_iota(jnp.int32, (C, GLA_HEADS * C), 0)
    acol = lax.broadcasted_iota(jnp.int32, (C, GLA_HEADS * C), 1) & (C - 1)
    causal = acol <= arow
    srow = lax.broadcasted_iota(jnp.int32, (GLA_KW, GLA_VW), 0) // GLA_DK
    scol = lax.broadcasted_iota(jnp.int32, (GLA_KW, GLA_VW), 1) // GLA_DV
    sdiag = srow == scol

    def gla_step(it, carry):
        r0 = pl.multiple_of(it * C, C)
        sq = it // n_chunks
        rows = pl.ds(r0, C)
        lgc = lg_s[rows, :]
        lg_hi, lg_lo = _split_bf16(lgc)
        bcum = _dot(tri, lg_hi) + _dot(tri, lg_lo)
        blast = bcum[C - 1:C, :]
        eb = jnp.exp(bcum)
        enb = jnp.exp(-bcum)
        q = z_s[rows, OFF_Q:OFF_Q + GLA_KW] * (GLA_DK ** -0.5)
        k = z_s[rows, OFF_K:OFF_K + GLA_KW]
        vv = z_s[rows, OFF_VG:OFF_VG + GLA_VW]
        qe = (q * eb).astype(BF16)
        ke = k * enb
        kd = (ke * jnp.exp(blast)).astype(BF16)
        ke_bd = jnp.concatenate(
            [jnp.where(khead == hh, ke, 0.0) for hh in range(GLA_HEADS)], axis=0).astype(BF16)
        att = jnp.where(causal, _dot_nt(qe, ke_bd), 0.0).astype(BF16)
        v_bd = jnp.concatenate(
            [jnp.where(vhead == hh, vv, 0.0) for hh in range(GLA_HEADS)], axis=0).astype(BF16)
        zblk = jnp.zeros((GLA_DK, GLA_DV), F32)
        s_bd = jnp.concatenate(
            [jnp.concatenate([gla_o_ref[sq, hh] if hc == hh else zblk for hc in range(GLA_HEADS)], axis=1)
             for hh in range(GLA_HEADS)], axis=0)
        o_s[rows, :] = _dot(att, v_bd) + _dot(qe, s_bd.astype(BF16))
        ecol = jnp.exp(_dot_tn(lg_hi, ones_c) + _dot_tn(lg_lo, ones_c))
        ecol = jnp.concatenate([ecol] * GLA_HEADS, axis=1)
        upd = _dot_tn(kd, vv.astype(BF16))
        s_new = ecol * s_bd + jnp.where(sdiag, upd, 0.0)
        for hh in range(GLA_HEADS):
            gla_o_ref[sq, hh] = s_new[hh * GLA_DK:(hh + 1) * GLA_DK, hh * GLA_DV:(hh + 1) * GLA_DV]
        return carry

    lax.fori_loop(0, NS * n_chunks, gla_step, 0)

    gate = z_s[:, OFF_G:OFF_G + GLA_VW]
    o_all = o_s[...]
    c_parts = []
    for hh in range(GLA_HEADS):
        sl = slice(hh * GLA_DV, (hh + 1) * GLA_DV)
        gh = gate[:, sl]
        c_parts.append(_rms(o_all[:, sl], glag_ref[...]) * (gh * jax.nn.sigmoid(gh)))

    mix = jnp.concatenate([a_out, b_out] + c_parts, axis=-1).astype(BF16)
    xo_ref[...] = x + _dot(mix, wout_ref[...])


def _ffn_kernel(x_ref, g_ref, wg_ref, wu_ref, wd_ref, fg_ref, o_ref, *, FC, final):
    x = x_ref[...]
    hf = _rms(x, g_ref[...]).astype(BF16)
    acc = x
    for c in range(D_FF // FC):
        sl = slice(c * FC, (c + 1) * FC)
        gt = _dot(hf, wg_ref[:, sl])
        up = _dot(hf, wu_ref[:, sl])
        act = (gt * jax.nn.sigmoid(gt) * up).astype(BF16)
        acc = acc + _dot(act, wd_ref[sl, :])
    if final:
        acc = _rms(acc, fg_ref[...])
    o_ref[...] = acc


def _wspec(shape, layer):
    nd = len(shape)
    return pl.BlockSpec((None,) + tuple(shape), lambda i, j, _l=layer, _n=nd: (_l,) + (0,) * _n,
                        pipeline_mode=pl.Buffered(1))


def _mix_call(layer, x2, pool0, gla0, wts, *, nseq, L, NS, T, pos0, state_layer):
    R = NS * T
    n_chunk = L // T
    grid = (nseq // NS, n_chunk)
    (ng, win, poolw, pscale, sgug, sguw, sgub, wa2, ba, glag, wout) = wts
    sl = state_layer
    in_specs = [
        pl.BlockSpec((R, D_MODEL), lambda i, j: (i * n_chunk + j, 0)),
        pl.BlockSpec((None, NS, POOL_BUF, POOL_WIDTH), lambda i, j: (sl, i, 0, 0)),
        pl.BlockSpec((None, NS, GLA_HEADS, GLA_DK, GLA_DV), lambda i, j: (sl, i, 0, 0, 0)),
        _wspec((1, D_MODEL), layer),
        _wspec((D_MODEL, IN_PAD), layer),
        _wspec((POOL_WIDTH, POOL_WIDTH), layer),
        _wspec((1, POOL_WIDTH), layer),
        _wspec((1, SGU_WIDTH), layer),
        _wspec(sguw.shape[1:], layer),
        _wspec(sgub.shape[1:], layer),
        _wspec((GLA_RANK_PAD, GLA_KW), layer),
        _wspec((1, GLA_KW), layer),
        _wspec((1, GLA_DV), layer),
        _wspec((D_MODEL, D_MODEL), layer),
    ]
    out_specs = [
        pl.BlockSpec((R, D_MODEL), lambda i, j: (i * n_chunk + j, 0)),
        pl.BlockSpec((NS, POOL_BUF, POOL_WIDTH), lambda i, j: (i, 0, 0)),
        pl.BlockSpec((NS, GLA_HEADS, GLA_DK, GLA_DV), lambda i, j: (i, 0, 0, 0)),
        pl.BlockSpec((R, SGU_WIDTH), lambda i, j: (i * n_chunk + j, 0)),
    ]
    out_shape = [
        jax.ShapeDtypeStruct((nseq * L, D_MODEL), F32),
        jax.ShapeDtypeStruct((nseq, POOL_BUF, POOL_WIDTH), F32),
        jax.ShapeDtypeStruct((nseq, GLA_HEADS, GLA_DK, GLA_DV), F32),
        jax.ShapeDtypeStruct((nseq * L, SGU_WIDTH), F32),
    ]
    scratch = [
        pltpu.VMEM((R, IN_PAD), F32),
        pltpu.VMEM((NS, POOL_PAD + T, POOL_WIDTH), F32),
        pltpu.VMEM((R, GLA_KW), F32),
        pltpu.VMEM((R, GLA_VW), F32),
    ]
    return pl.pallas_call(
        functools.partial(_mix_kernel, NS=NS, T=T, pos0=pos0),
        grid=grid, in_specs=in_specs, out_specs=out_specs, out_shape=out_shape,
        scratch_shapes=scratch,
        compiler_params=pltpu.CompilerParams(
            dimension_semantics=("arbitrary", "arbitrary"), vmem_limit_bytes=VMEM_LIMIT),
        name=f"mix_T{T}",
    )(x2, pool0, gla0, ng, win, poolw, pscale, sgug, sguw, sgub, wa2, ba, glag, wout)


def _ffn_call(layer, x2, wts, final_g, *, TM, FC, final):
    n = x2.shape[0]
    g, wg, wu, wd = wts
    return pl.pallas_call(
        functools.partial(_ffn_kernel, FC=FC, final=final),
        grid=(n // TM, 1),
        in_specs=[
            pl.BlockSpec((TM, D_MODEL), lambda i, j: (i, 0)),
            _wspec((1, D_MODEL), layer),
            _wspec((D_MODEL, D_FF), layer),
            _wspec((D_MODEL, D_FF), layer),
            _wspec((D_FF, D_MODEL), layer),
            pl.BlockSpec((1, D_MODEL), lambda i, j: (0, 0)),
        ],
        out_specs=pl.BlockSpec((TM, D_MODEL), lambda i, j: (i, 0)),
        out_shape=jax.ShapeDtypeStruct((n, D_MODEL), F32),
        compiler_params=pltpu.CompilerParams(
            dimension_semantics=("arbitrary", "arbitrary"), vmem_limit_bytes=VMEM_LIMIT),
        name=f"ffn_{n}",
    )(x2, g, wg, wu, wd, final_g)


def kernel(x_prompt, x_sample, state_pool, state_gla, attn_norm_g, w_in, pool_w, pool_scale, sgu_norm_g, sgu_ws, sgu_b, gla_wa2, gla_ba, gla_norm_g, w_out, ffn_norm_g, w_gate, w_up, w_down, final_norm_g):
    bp, seq, _ = x_prompt.shape
    bs, dseq, _ = x_sample.shape
    past_len = 16384
    assert seq % 512 == 0 and dseq == 8 and bs % 32 == 0

    ng = attn_norm_g[:, None, :]
    win = jnp.pad(w_in, ((0, 0), (0, 0), (0, IN_PAD - w_in.shape[-1]))).astype(BF16)
    eye_g = jnp.eye(len(POOL_WINDOWS), dtype=F32)
    poolw = jnp.einsum('lgcd,gh->lgchd', pool_w, eye_g).reshape(DEPTH, POOL_WIDTH, POOL_WIDTH).astype(BF16)
    pscale = pool_scale[:, None, :]
    sgug = sgu_norm_g.reshape(DEPTH, 1, SGU_WIDTH)
    sguw_cat = sgu_ws.transpose(0, 2, 1, 3).reshape(DEPTH, SGU_CHUNK, SGU_HEADS * SGU_CHUNK)
    sgub_tile = jnp.repeat(sgu_b.transpose(0, 2, 1), SGU_HEAD_DIM, axis=-1)
    sguw_dec = jnp.repeat(sgu_ws[:, :, :dseq, :dseq].transpose(0, 3, 2, 1), SGU_HEAD_DIM, axis=-1)
    sgub_dec = sgub_tile[:, :dseq]
    wa2 = jnp.pad(gla_wa2, ((0, 0), (0, GLA_RANK_PAD - GLA_RANK), (0, 0))).astype(BF16)
    ba = gla_ba[:, None, :]
    glag = gla_norm_g[:, None, :]
    wout = w_out.astype(BF16)
    fng = ffn_norm_g[:, None, :]
    wg = w_gate.astype(BF16)
    wu = w_up.astype(BF16)
    wd = w_down.astype(BF16)
    fin = final_norm_g[None, :]

    xp = x_prompt.reshape(bp * seq, D_MODEL)
    xs = x_sample.reshape(bs * dseq, D_MODEL)
    pool0_p = jnp.zeros((1, bp, POOL_BUF, POOL_WIDTH), F32)
    gla0_p = jnp.zeros((1, bp, GLA_HEADS, GLA_DK, GLA_DV), F32)

    pool_p, gla_p, pool_s, gla_s, sgu_s = [], [], [], [], []
    for l in range(DEPTH):
        last = l == DEPTH - 1
        common = (ng, win, poolw, pscale, sgug)
        tail = (wa2, ba, glag, wout)
        xp, nb_p, ns_p, _ = _mix_call(l, xp, pool0_p, gla0_p, common + (sguw_cat, sgub_tile) + tail,
                                      nseq=bp, L=seq, NS=1, T=512, pos0=0, state_layer=0)
        xs, nb_s, ns_s, v_s = _mix_call(l, xs, state_pool, state_gla, common + (sguw_dec, sgub_dec) + tail,
                                        nseq=bs, L=dseq, NS=32, T=dseq, pos0=past_len, state_layer=l)
        ffn_w = (fng, wg, wu, wd)
        xp = _ffn_call(l, xp, ffn_w, fin, TM=512, FC=1408, final=last)
        xs = _ffn_call(l, xs, ffn_w, fin, TM=512, FC=1408, final=last)
        pool_p.append(nb_p)
        gla_p.append(ns_p)
        pool_s.append(nb_s)
        gla_s.append(ns_s)
        sgu_s.append(v_s.reshape(bs, dseq, SGU_WIDTH))

    return (xp.reshape(bp, seq, D_MODEL), xs.reshape(bs, dseq, D_MODEL),
            jnp.stack(pool_p), jnp.stack(gla_p), jnp.stack(pool_s), jnp.stack(gla_s), jnp.stack(sgu_s))
```

```python
import functools

import jax
import jax.numpy as jnp
from jax import lax
from jax.experimental import pallas as pl
from jax.experimental.pallas import tpu as pltpu

F32 = jnp.float32
BF16 = jnp.bfloat16

D_MODEL = 1024
DEPTH = 4
POOL_WIDTH = 256
POOL_WINDOWS = (2, 4, 8, 16)
POOL_GROUP_DIM = 64
POOL_BUF = 15
POOL_PAD = 16
SGU_WIDTH = 256
SGU_HEADS = 4
SGU_HEAD_DIM = 64
SGU_CHUNK = 128
GLA_HEADS = 4
GLA_DK = 64
GLA_DV = 128
GLA_KW = GLA_HEADS * GLA_DK
GLA_VW = GLA_HEADS * GLA_DV
GLA_RANK = 16
GLA_RANK_PAD = 128
GLA_GATE_NORM = 16.0
GLA_CHUNK = 64
CUM_ROWS = 256
SEQ_UNROLL = 4
D_FF = 2816
EPS = 1e-6

OFF_P, OFF_U, OFF_V, OFF_Q, OFF_K, OFF_VG, OFF_G, OFF_A = 0, 256, 512, 768, 1024, 1280, 1792, 2304
IN_MAIN = 2304
IN_PAD = IN_MAIN + GLA_RANK_PAD

VMEM_LIMIT = 56 * 1024 * 1024


def _rms(x, g):
    return x * lax.rsqrt(jnp.mean(x * x, axis=-1, keepdims=True) + EPS) * g


def _dot(a, b):
    return jnp.dot(a, b, preferred_element_type=F32)


def _dot_nt(a, b):
    return lax.dot_general(a, b, (((1,), (1,)), ((), ())), preferred_element_type=F32)


def _dot_tn(a, b):
    return lax.dot_general(a, b, (((0,), (0,)), ((), ())), preferred_element_type=F32)


def _split_bf16(x):
    hi = x.astype(BF16)
    lo = (x - hi.astype(F32)).astype(BF16)
    return hi, lo


def _mix_kernel(x_ref, pool0_ref, gla0_ref, ng_ref, win_ref, poolw_ref, pscale_ref,
                sgug_ref, sguw_ref, sgub_ref, wa2_ref, ba_ref, glag_ref, wout_ref,
                xo_ref, pool_o_ref, gla_o_ref, vn_o_ref,
                z_s, pz_s, qe_s, ke_s, kd_s, ecol_s, o_s, *, NS, T, pos0):
    j = pl.program_id(1)
    R = NS * T
    C = min(GLA_CHUNK, T)
    n_chunks = T // C

    x = x_ref[...]
    h = _rms(x, ng_ref[...]).astype(BF16)
    z_s[...] = _dot(h, win_ref[...])

    @pl.when(j == 0)
    def _():
        pz_s[:, 0:1, :] = jnp.zeros((NS, 1, POOL_WIDTH), F32)
        pz_s[:, 1:POOL_PAD, :] = pool0_ref[...]
        gla_o_ref[...] = gla0_ref[...]

    p3 = z_s[:, OFF_P:OFF_P + POOL_WIDTH].reshape(NS, T, POOL_WIDTH)
    pz_s[:, POOL_PAD:POOL_PAD + T, :] = p3
    e = pz_s[...]
    w2 = e + pltpu.roll(e, 1, 1)
    w4 = w2 + pltpu.roll(w2, 2, 1)
    w8 = w4 + pltpu.roll(w4, 4, 1)
    w16 = w8 + pltpu.roll(w8, 8, 1)
    grp = lax.broadcasted_iota(jnp.int32, (1, 1, POOL_WIDTH), 2) // POOL_GROUP_DIM
    wsum = jnp.where(grp == 0, w2, jnp.where(grp == 1, w4, jnp.where(grp == 2, w8, w16)))
    wsum = wsum[:, POOL_PAD:POOL_PAD + T, :]
    win = jnp.where(grp == 0, 2, jnp.where(grp == 1, 4, jnp.where(grp == 2, 8, 16)))
    pos = pos0 + j * T + lax.broadcasted_iota(jnp.int32, (1, T, 1), 1)
    cnt = jnp.minimum(pos + 1, win).astype(F32)
    d = (wsum / cnt - p3).reshape(R, POOL_WIDTH)
    a_out = _dot(d.astype(BF16), poolw_ref[...]) * pscale_ref[...]
    pool_o_ref[...] = pz_s[:, T + 1:T + POOL_PAD, :]
    if T >= POOL_PAD:
        pz_s[:, 0:POOL_PAD, :] = pz_s[:, T:T + POOL_PAD, :]

    v = z_s[:, OFF_V:OFF_V + SGU_WIDTH]
    u = z_s[:, OFF_U:OFF_U + SGU_WIDTH]
    r_i = lax.broadcasted_iota(jnp.int32, (SGU_WIDTH, SGU_WIDTH), 0) // SGU_HEAD_DIM
    c_i = lax.broadcasted_iota(jnp.int32, (SGU_WIDTH, SGU_WIDTH), 1) // SGU_HEAD_DIM
    head_ones = (r_i == c_i).astype(BF16)
    vsq_hi, vsq_lo = _split_bf16(v * v)
    ss = _dot(vsq_hi, head_ones) + _dot(vsq_lo, head_ones)
    vn = v * lax.rsqrt(ss * (1.0 / SGU_HEAD_DIM) + EPS) * sgug_ref[...]
    vn_o_ref[...] = vn
    lane_head = lax.broadcasted_iota(jnp.int32, (1, SGU_WIDTH), 1) // SGU_HEAD_DIM
    if T >= SGU_CHUNK:
        row = lax.broadcasted_iota(jnp.int32, (SGU_CHUNK, SGU_HEADS * SGU_CHUNK), 0)
        col = lax.broadcasted_iota(jnp.int32, (SGU_CHUNK, SGU_HEADS * SGU_CHUNK), 1)
        wcat = jnp.where((col & (SGU_CHUNK - 1)) <= row, sguw_ref[...], 0.0).astype(BF16)
        parts = []
        for c in range(R // SGU_CHUNK):
            vc = vn[c * SGU_CHUNK:(c + 1) * SGU_CHUNK]
            stack = jnp.concatenate(
                [jnp.where(lane_head == hh, vc, 0.0) for hh in range(SGU_HEADS)], axis=0)
            parts.append(_dot(wcat, stack.astype(BF16)) + sgub_ref[...])
        s_gate = jnp.concatenate(parts, axis=0)
    else:
        vn3 = vn.reshape(NS, T, SGU_WIDTH)
        row = lax.broadcasted_iota(jnp.int32, (T, SGU_WIDTH), 0)
        s3 = jnp.zeros((NS, T, SGU_WIDTH), F32) + sgub_ref[...][None]
        for jj in range(T):
            coef = jnp.where(row >= jj, sguw_ref[jj], 0.0)
            s3 = s3 + vn3[:, jj:jj + 1, :] * coef[None]
        s_gate = s3.reshape(R, SGU_WIDTH)
    b_out = u * s_gate

    alow = z_s[:, OFF_A:OFF_A + GLA_RANK_PAD].astype(BF16)
    xg = _dot(alow, wa2_ref[...]) + ba_ref[...]
    lg = (jnp.minimum(xg, 0.0) - jnp.log1p(jnp.exp(-jnp.abs(xg)))) * (1.0 / GLA_GATE_NORM)
    lg_hi, lg_lo = _split_bf16(lg)

    n_units = R // C
    rr = lax.broadcasted_iota(jnp.int32, (CUM_ROWS, CUM_ROWS), 0)
    cc = lax.broadcasted_iota(jnp.int32, (CUM_ROWS, CUM_ROWS), 1)
    tri_bd = ((rr // C == cc // C) & (rr >= cc)).astype(BF16)
    bcum = jnp.concatenate(
        [_dot(tri_bd, lg_hi[g * CUM_ROWS:(g + 1) * CUM_ROWS]) + _dot(tri_bd, lg_lo[g * CUM_ROWS:(g + 1) * CUM_ROWS])
         for g in range(R // CUM_ROWS)], axis=0)
    b3 = bcum.reshape(n_units, C, GLA_KW)
    btot = jnp.broadcast_to(b3[:, C - 1:C, :], (n_units, C, GLA_KW)).reshape(R, GLA_KW)
    ke_all = z_s[:, OFF_K:OFF_K + GLA_KW] * jnp.exp(-bcum)
    qe_s[...] = z_s[:, OFF_Q:OFF_Q + GLA_KW] * (GLA_DK ** -0.5) * jnp.exp(bcum)
    ke_s[...] = ke_all
    kd_s[...] = ke_all * jnp.exp(btot)
    sel2 = (lax.broadcasted_iota(jnp.int32, (2 * C, 2 * GLA_DV), 0) // C
            == lax.broadcasted_iota(jnp.int32, (2 * C, 2 * GLA_DV), 1) // GLA_DV).astype(BF16)
    for p in range(n_units // 2):
        prow = slice(p * 2 * C, (p + 1) * 2 * C)
        ecol_s[:, p * 2 * GLA_DV:(p + 1) * 2 * GLA_DV] = jnp.exp(
            _dot_tn(lg_hi[prow], sel2) + _dot_tn(lg_lo[prow], sel2))

    khead = lax.broadcasted_iota(jnp.int32, (1, GLA_KW), 1) // GLA_DK
    vhead = lax.broadcasted_iota(jnp.int32, (1, GLA_VW), 1) // GLA_DV
    arow = lax.broadcasted_iota(jnp.int32, (C, GLA_HEADS * C), 0)
    acol = lax.broadcasted_iota(jnp.int32, (C, GLA_HEADS * C), 1) & (C - 1)
    causal = acol <= arow
    zblk = jnp.zeros((GLA_DK, GLA_DV), F32)

    def gla_unit(r0, e0, s_blocks):
        rows = pl.ds(r0, C)
        qe = qe_s[rows, :].astype(BF16)
        ke = ke_s[rows, :]
        kd = kd_s[rows, :].astype(BF16)
        vv = z_s[rows, OFF_VG:OFF_VG + GLA_VW]
        ke_bd = jnp.concatenate(
            [jnp.where(khead == hh, ke, 0.0) for hh in range(GLA_HEADS)], axis=0).astype(BF16)
        att = jnp.where(causal, _dot_nt(qe, ke_bd), 0.0).astype(BF16)
        v_bd = jnp.concatenate(
            [jnp.where(vhead == hh, vv, 0.0) for hh in range(GLA_HEADS)], axis=0).astype(BF16)
        s_bd = jnp.concatenate(
            [jnp.concatenate([s_blocks[hh] if hc == hh else zblk for hc in range(GLA_HEADS)], axis=1)
             for hh in range(GLA_HEADS)], axis=0)
        o_s[rows, :] = _dot(att, v_bd) + _dot(qe, s_bd.astype(BF16))
        ecol = ecol_s[:, pl.ds(e0, GLA_DV)]
        upd = _dot_tn(kd, vv.astype(BF16))
        return [ecol[hh * GLA_DK:(hh + 1) * GLA_DK] * s_blocks[hh]
                + upd[hh * GLA_DK:(hh + 1) * GLA_DK, hh * GLA_DV:(hh + 1) * GLA_DV]
                for hh in range(GLA_HEADS)]

    if NS == 1:
        s_blocks = [gla_o_ref[0, hh] for hh in range(GLA_HEADS)]
        for c in range(n_units):
            s_blocks = gla_unit(c * C, c * GLA_DV, s_blocks)
        for hh in range(GLA_HEADS):
            gla_o_ref[0, hh] = s_blocks[hh]
    else:
        assert n_units == NS and NS % SEQ_UNROLL == 0

        def seq_group(g, carry):
            for i in range(SEQ_UNROLL):
                sq = g * SEQ_UNROLL + i
                new = gla_unit(pl.multiple_of(sq * C, C), pl.multiple_of(sq * GLA_DV, GLA_DV),
                               [gla_o_ref[sq, hh] for hh in range(GLA_HEADS)])
                for hh in range(GLA_HEADS):
                    gla_o_ref[sq, hh] = new[hh]
            return carry

        lax.fori_loop(0, NS // SEQ_UNROLL, seq_group, 0)

    gate = z_s[:, OFF_G:OFF_G + GLA_VW]
    o_all = o_s[...]
    c_parts = []
    for hh in range(GLA_HEADS):
        sl = slice(hh * GLA_DV, (hh + 1) * GLA_DV)
        gh = gate[:, sl]
        c_parts.append(_rms(o_all[:, sl], glag_ref[...]) * (gh * jax.nn.sigmoid(gh)))

    mix = jnp.concatenate([a_out, b_out] + c_parts, axis=-1).astype(BF16)
    xo_ref[...] = x + _dot(mix, wout_ref[...])


def _ffn_kernel(x_ref, g_ref, wg_ref, wu_ref, wd_ref, fg_ref, o_ref, *, FC, final):
    x = x_ref[...]
    hf = _rms(x, g_ref[...]).astype(BF16)
    acc = x
    for c in range(D_FF // FC):
        sl = slice(c * FC, (c + 1) * FC)
        gt = _dot(hf, wg_ref[:, sl])
        up = _dot(hf, wu_ref[:, sl])
        act = (gt * jax.nn.sigmoid(gt) * up).astype(BF16)
        acc = acc + _dot(act, wd_ref[sl, :])
    if final:
        acc = _rms(acc, fg_ref[...])
    o_ref[...] = acc


def _wspec(shape, layer):
    nd = len(shape)
    return pl.BlockSpec((None,) + tuple(shape), lambda i, j, _l=layer, _n=nd: (_l,) + (0,) * _n,
                        pipeline_mode=pl.Buffered(1))


def _mix_call(layer, x2, pool0, gla0, wts, *, nseq, L, NS, T, pos0, state_layer):
    R = NS * T
    n_chunk = L // T
    grid = (nseq // NS, n_chunk)
    (ng, win, poolw, pscale, sgug, sguw, sgub, wa2, ba, glag, wout) = wts
    sl = state_layer
    in_specs = [
        pl.BlockSpec((R, D_MODEL), lambda i, j: (i * n_chunk + j, 0)),
        pl.BlockSpec((None, NS, POOL_BUF, POOL_WIDTH), lambda i, j: (sl, i, 0, 0)),
        pl.BlockSpec((None, NS, GLA_HEADS, GLA_DK, GLA_DV), lambda i, j: (sl, i, 0, 0, 0)),
        _wspec((1, D_MODEL), layer),
        _wspec((D_MODEL, IN_PAD), layer),
        _wspec((POOL_WIDTH, POOL_WIDTH), layer),
        _wspec((1, POOL_WIDTH), layer),
        _wspec((1, SGU_WIDTH), layer),
        _wspec(sguw.shape[1:], layer),
        _wspec(sgub.shape[1:], layer),
        _wspec((GLA_RANK_PAD, GLA_KW), layer),
        _wspec((1, GLA_KW), layer),
        _wspec((1, GLA_DV), layer),
        _wspec((D_MODEL, D_MODEL), layer),
    ]
    out_specs = [
        pl.BlockSpec((R, D_MODEL), lambda i, j: (i * n_chunk + j, 0)),
        pl.BlockSpec((NS, POOL_BUF, POOL_WIDTH), lambda i, j: (i, 0, 0)),
        pl.BlockSpec((NS, GLA_HEADS, GLA_DK, GLA_DV), lambda i, j: (i, 0, 0, 0)),
        pl.BlockSpec((R, SGU_WIDTH), lambda i, j: (i * n_chunk + j, 0)),
    ]
    out_shape = [
        jax.ShapeDtypeStruct((nseq * L, D_MODEL), F32),
        jax.ShapeDtypeStruct((nseq, POOL_BUF, POOL_WIDTH), F32),
        jax.ShapeDtypeStruct((nseq, GLA_HEADS, GLA_DK, GLA_DV), F32),
        jax.ShapeDtypeStruct((nseq * L, SGU_WIDTH), F32),
    ]
    scratch = [
        pltpu.VMEM((R, IN_PAD), F32),
        pltpu.VMEM((NS, POOL_PAD + T, POOL_WIDTH), F32),
        pltpu.VMEM((R, GLA_KW), F32),
        pltpu.VMEM((R, GLA_KW), F32),
        pltpu.VMEM((R, GLA_KW), F32),
        pltpu.VMEM((GLA_KW, (R // min(GLA_CHUNK, T)) * GLA_DV), F32),
        pltpu.VMEM((R, GLA_VW), F32),
    ]
    return pl.pallas_call(
        functools.partial(_mix_kernel, NS=NS, T=T, pos0=pos0),
        grid=grid, in_specs=in_specs, out_specs=out_specs, out_shape=out_shape,
        scratch_shapes=scratch,
        compiler_params=pltpu.CompilerParams(
            dimension_semantics=("arbitrary", "arbitrary"), vmem_limit_bytes=VMEM_LIMIT),
        name=f"mix_T{T}",
    )(x2, pool0, gla0, ng, win, poolw, pscale, sgug, sguw, sgub, wa2, ba, glag, wout)


def _ffn_call(layer, x2, wts, final_g, *, TM, FC, final):
    n = x2.shape[0]
    g, wg, wu, wd = wts
    return pl.pallas_call(
        functools.partial(_ffn_kernel, FC=FC, final=final),
        grid=(n // TM, 1),
        in_specs=[
            pl.BlockSpec((TM, D_MODEL), lambda i, j: (i, 0)),
            _wspec((1, D_MODEL), layer),
            _wspec((D_MODEL, D_FF), layer),
            _wspec((D_MODEL, D_FF), layer),
            _wspec((D_FF, D_MODEL), layer),
            pl.BlockSpec((1, D_MODEL), lambda i, j: (0, 0)),
        ],
        out_specs=pl.BlockSpec((TM, D_MODEL), lambda i, j: (i, 0)),
        out_shape=jax.ShapeDtypeStruct((n, D_MODEL), F32),
        compiler_params=pltpu.CompilerParams(
            dimension_semantics=("arbitrary", "arbitrary"), vmem_limit_bytes=VMEM_LIMIT),
        name=f"ffn_{n}",
    )(x2, g, wg, wu, wd, final_g)


def kernel(x_prompt, x_sample, state_pool, state_gla, attn_norm_g, w_in, pool_w, pool_scale, sgu_norm_g, sgu_ws, sgu_b, gla_wa2, gla_ba, gla_norm_g, w_out, ffn_norm_g, w_gate, w_up, w_down, final_norm_g):
    bp, seq, _ = x_prompt.shape
    bs, dseq, _ = x_sample.shape
    past_len = 16384
    assert seq % 512 == 0 and dseq == 8 and bs % 32 == 0

    ng = attn_norm_g[:, None, :]
    win = jnp.pad(w_in, ((0, 0), (0, 0), (0, IN_PAD - w_in.shape[-1]))).astype(BF16)
    eye_g = jnp.eye(len(POOL_WINDOWS), dtype=F32)
    poolw = jnp.einsum('lgcd,gh->lgchd', pool_w, eye_g).reshape(DEPTH, POOL_WIDTH, POOL_WIDTH).astype(BF16)
    pscale = pool_scale[:, None, :]
    sgug = sgu_norm_g.reshape(DEPTH, 1, SGU_WIDTH)
    sguw_cat = sgu_ws.transpose(0, 2, 1, 3).reshape(DEPTH, SGU_CHUNK, SGU_HEADS * SGU_CHUNK)
    sgub_tile = jnp.repeat(sgu_b.transpose(0, 2, 1), SGU_HEAD_DIM, axis=-1)
    sguw_dec = jnp.repeat(sgu_ws[:, :, :dseq, :dseq].transpose(0, 3, 2, 1), SGU_HEAD_DIM, axis=-1)
    sgub_dec = sgub_tile[:, :dseq]
    wa2 = jnp.pad(gla_wa2, ((0, 0), (0, GLA_RANK_PAD - GLA_RANK), (0, 0))).astype(BF16)
    ba = gla_ba[:, None, :]
    glag = gla_norm_g[:, None, :]
    wout = w_out.astype(BF16)
    fng = ffn_norm_g[:, None, :]
    wg = w_gate.astype(BF16)
    wu = w_up.astype(BF16)
    wd = w_down.astype(BF16)
    fin = final_norm_g[None, :]

    xp = x_prompt.reshape(bp * seq, D_MODEL)
    xs = x_sample.reshape(bs * dseq, D_MODEL)
    pool0_p = jnp.zeros((1, bp, POOL_BUF, POOL_WIDTH), F32)
    gla0_p = jnp.zeros((1, bp, GLA_HEADS, GLA_DK, GLA_DV), F32)

    pool_p, gla_p, pool_s, gla_s, sgu_s = [], [], [], [], []
    for l in range(DEPTH):
        last = l == DEPTH - 1
        common = (ng, win, poolw, pscale, sgug)
        tail = (wa2, ba, glag, wout)
        xp, nb_p, ns_p, _ = _mix_call(l, xp, pool0_p, gla0_p, common + (sguw_cat, sgub_tile) + tail,
                                      nseq=bp, L=seq, NS=1, T=512, pos0=0, state_layer=0)
        xs, nb_s, ns_s, v_s = _mix_call(l, xs, state_pool, state_gla, common + (sguw_dec, sgub_dec) + tail,
                                        nseq=bs, L=dseq, NS=32, T=dseq, pos0=past_len, state_layer=l)
        ffn_w = (fng, wg, wu, wd)
        xp = _ffn_call(l, xp, ffn_w, fin, TM=512, FC=1408, final=last)
        xs = _ffn_call(l, xs, ffn_w, fin, TM=512, FC=1408, final=last)
        pool_p.append(nb_p)
        gla_p.append(ns_p)
        pool_s.append(nb_s)
        gla_s.append(ns_s)
        sgu_s.append(v_s.reshape(bs, dseq, SGU_WIDTH))

    return (xp.reshape(bp, seq, D_MODEL), xs.reshape(bs, dseq, D_MODEL),
            jnp.stack(pool_p), jnp.stack(gla_p), jnp.stack(pool_s), jnp.stack(gla_s), jnp.stack(sgu_s))
```

```python
import functools

import jax
import jax.numpy as jnp
from jax import lax
from jax.experimental import pallas as pl
from jax.experimental.pallas import tpu as pltpu

F32 = jnp.float32
BF16 = jnp.bfloat16

D_MODEL = 1024
DEPTH = 4
POOL_WIDTH = 256
POOL_WINDOWS = (2, 4, 8, 16)
POOL_GROUP_DIM = 64
POOL_BUF = 15
POOL_PAD = 16
SGU_WIDTH = 256
SGU_HEADS = 4
SGU_HEAD_DIM = 64
SGU_CHUNK = 128
GLA_HEADS = 4
GLA_DK = 64
GLA_DV = 128
GLA_KW = GLA_HEADS * GLA_DK
GLA_VW = GLA_HEADS * GLA_DV
GLA_RANK = 16
GLA_RANK_PAD = 128
GLA_GATE_NORM = 16.0
GLA_CHUNK = 64
CUM_ROWS = 256
SEQ_UNROLL = 4
D_FF = 2816
EPS = 1e-6

OFF_P, OFF_U, OFF_V, OFF_Q, OFF_K, OFF_VG, OFF_G, OFF_A = 0, 256, 512, 768, 1024, 1280, 1792, 2304
IN_MAIN = 2304
FFN_CHUNKS = ((0, 1536), (1536, 2816))

PAST_LEN = 16384
PROMPT_TILE = 512
SAMPLE_SEQS = 32
FFN_ROWS = 512
VMEM_LIMIT = 56 * 1024 * 1024


def _rms(x, g):
    return x * lax.rsqrt(jnp.mean(x * x, axis=-1, keepdims=True) + EPS) * g


def _dot(a, b):
    return jnp.dot(a, b, preferred_element_type=F32)


def _dot_nt(a, b):
    return lax.dot_general(a, b, (((1,), (1,)), ((), ())), preferred_element_type=F32)


def _dot_tn(a, b):
    return lax.dot_general(a, b, (((0,), (0,)), ((), ())), preferred_element_type=F32)


def _split_bf16(x):
    hi = x.astype(BF16)
    lo = (x - hi.astype(F32)).astype(BF16)
    return hi, lo


def _mix_kernel(*refs, NS, T, pos0, emit_vn, n_prev):
    (x_ref, pool0_ref, gla0_ref, ng_ref, win_ref, wina_ref, poolw_ref, pscale_ref,
     sgug_ref, sguw_ref, sgub_ref, wa2_ref, ba_ref, glag_ref, wout_ref) = refs[:15]
    refs = refs[15 + n_prev:]
    xo_ref, pool_o_ref, gla_o_ref = refs[:3]
    vn_o_ref = refs[3] if emit_vn else None
    z_s, pz_s, qe_s, ke_s, kd_s, ecol_s, o_s = refs[-7:]
    j = pl.program_id(1)
    R = NS * T
    C = min(GLA_CHUNK, T)
    n_chunks = T // C

    x = x_ref[...]
    h = _rms(x, ng_ref[...]).astype(BF16)
    z_s[...] = _dot(h, win_ref[...])
    alow = _dot(h, wina_ref[...]).astype(BF16)

    @pl.when(j == 0)
    def _():
        pz_s[:, 0:1, :] = jnp.zeros((NS, 1, POOL_WIDTH), F32)
        pz_s[:, 1:POOL_PAD, :] = pool0_ref[...]
        gla_o_ref[...] = gla0_ref[...]

    p3 = z_s[:, OFF_P:OFF_P + POOL_WIDTH].reshape(NS, T, POOL_WIDTH)
    pz_s[:, POOL_PAD:POOL_PAD + T, :] = p3
    e = pz_s[...]
    w2 = e + pltpu.roll(e, 1, 1)
    w4 = w2 + pltpu.roll(w2, 2, 1)
    w8 = w4 + pltpu.roll(w4, 4, 1)
    w16 = w8 + pltpu.roll(w8, 8, 1)
    grp = lax.broadcasted_iota(jnp.int32, (1, 1, POOL_WIDTH), 2) // POOL_GROUP_DIM
    wsum = jnp.where(grp == 0, w2, jnp.where(grp == 1, w4, jnp.where(grp == 2, w8, w16)))
    wsum = wsum[:, POOL_PAD:POOL_PAD + T, :]
    win = jnp.where(grp == 0, 2, jnp.where(grp == 1, 4, jnp.where(grp == 2, 8, 16)))
    pos = pos0 + j * T + lax.broadcasted_iota(jnp.int32, (1, T, 1), 1)
    cnt = jnp.minimum(pos + 1, win).astype(F32)
    d = (wsum / cnt - p3).reshape(R, POOL_WIDTH)
    a_out = _dot(d.astype(BF16), poolw_ref[...]) * pscale_ref[...]
    pool_o_ref[...] = pz_s[:, T + 1:T + POOL_PAD, :]
    if T >= POOL_PAD:
        pz_s[:, 0:POOL_PAD, :] = pz_s[:, T:T + POOL_PAD, :]

    v = z_s[:, OFF_V:OFF_V + SGU_WIDTH]
    u = z_s[:, OFF_U:OFF_U + SGU_WIDTH]
    r_i = lax.broadcasted_iota(jnp.int32, (SGU_WIDTH, SGU_WIDTH), 0) // SGU_HEAD_DIM
    c_i = lax.broadcasted_iota(jnp.int32, (SGU_WIDTH, SGU_WIDTH), 1) // SGU_HEAD_DIM
    head_ones = (r_i == c_i).astype(BF16)
    vsq_hi, vsq_lo = _split_bf16(v * v)
    ss = _dot(vsq_hi, head_ones) + _dot(vsq_lo, head_ones)
    vn = v * lax.rsqrt(ss * (1.0 / SGU_HEAD_DIM) + EPS) * sgug_ref[...]
    if emit_vn:
        vn_o_ref[...] = vn
    lane_head = lax.broadcasted_iota(jnp.int32, (1, SGU_WIDTH), 1) // SGU_HEAD_DIM
    if T >= SGU_CHUNK:
        row = lax.broadcasted_iota(jnp.int32, (SGU_CHUNK, SGU_HEADS * SGU_CHUNK), 0)
        col = lax.broadcasted_iota(jnp.int32, (SGU_CHUNK, SGU_HEADS * SGU_CHUNK), 1)
        wcat = jnp.where((col & (SGU_CHUNK - 1)) <= row, sguw_ref[...], 0.0).astype(BF16)
        parts = []
        for c in range(R // SGU_CHUNK):
            vc = vn[c * SGU_CHUNK:(c + 1) * SGU_CHUNK]
            stack = jnp.concatenate(
                [jnp.where(lane_head == hh, vc, 0.0) for hh in range(SGU_HEADS)], axis=0)
            parts.append(_dot(wcat, stack.astype(BF16)) + sgub_ref[...])
        s_gate = jnp.concatenate(parts, axis=0)
    else:
        vn3 = vn.reshape(NS, T, SGU_WIDTH)
        row = lax.broadcasted_iota(jnp.int32, (T, SGU_WIDTH), 0)
        s3 = jnp.zeros((NS, T, SGU_WIDTH), F32) + sgub_ref[...][None]
        for jj in range(T):
            coef = jnp.where(row >= jj, sguw_ref[jj], 0.0)
            s3 = s3 + vn3[:, jj:jj + 1, :] * coef[None]
        s_gate = s3.reshape(R, SGU_WIDTH)
    b_out = u * s_gate

    xg = _dot(alow, wa2_ref[...]) + ba_ref[...]
    lg = (jnp.minimum(xg, 0.0) - jnp.log1p(jnp.exp(-jnp.abs(xg)))) * (1.0 / GLA_GATE_NORM)
    lg_hi, lg_lo = _split_bf16(lg)

    n_units = R // C
    rr = lax.broadcasted_iota(jnp.int32, (CUM_ROWS, CUM_ROWS), 0)
    cc = lax.broadcasted_iota(jnp.int32, (CUM_ROWS, CUM_ROWS), 1)
    tri_bd = ((rr // C == cc // C) & (rr >= cc)).astype(BF16)
    bcum = jnp.concatenate(
        [_dot(tri_bd, lg_hi[g * CUM_ROWS:(g + 1) * CUM_ROWS]) + _dot(tri_bd, lg_lo[g * CUM_ROWS:(g + 1) * CUM_ROWS])
         for g in range(R // CUM_ROWS)], axis=0)
    b3 = bcum.reshape(n_units, C, GLA_KW)
    btot = jnp.broadcast_to(b3[:, C - 1:C, :], (n_units, C, GLA_KW)).reshape(R, GLA_KW)
    ke_all = z_s[:, OFF_K:OFF_K + GLA_KW] * jnp.exp(-bcum)
    qe_s[...] = z_s[:, OFF_Q:OFF_Q + GLA_KW] * (GLA_DK ** -0.5) * jnp.exp(bcum)
    ke_s[...] = ke_all
    kd_s[...] = ke_all * jnp.exp(btot)
    sel2 = (lax.broadcasted_iota(jnp.int32, (2 * C, 2 * GLA_DV), 0) // C
            == lax.broadcasted_iota(jnp.int32, (2 * C, 2 * GLA_DV), 1) // GLA_DV).astype(BF16)
    for p in range(n_units // 2):
        prow = slice(p * 2 * C, (p + 1) * 2 * C)
        ecol_s[:, p * 2 * GLA_DV:(p + 1) * 2 * GLA_DV] = jnp.exp(
            _dot_tn(lg_hi[prow], sel2) + _dot_tn(lg_lo[prow], sel2))

    khead = lax.broadcasted_iota(jnp.int32, (1, GLA_KW), 1) // GLA_DK
    vhead = lax.broadcasted_iota(jnp.int32, (1, GLA_VW), 1) // GLA_DV
    arow = lax.broadcasted_iota(jnp.int32, (C, GLA_HEADS * C), 0)
    acol = lax.broadcasted_iota(jnp.int32, (C, GLA_HEADS * C), 1) & (C - 1)
    causal = acol <= arow
    zblk = jnp.zeros((GLA_DK, GLA_DV), F32)

    def gla_unit(r0, e0, s_blocks):
        rows = pl.ds(r0, C)
        qe = qe_s[rows, :].astype(BF16)
        ke = ke_s[rows, :]
        kd = kd_s[rows, :].astype(BF16)
        vv = z_s[rows, OFF_VG:OFF_VG + GLA_VW]
        ke_bd = jnp.concatenate(
            [jnp.where(khead == hh, ke, 0.0) for hh in range(GLA_HEADS)], axis=0).astype(BF16)
        att = jnp.where(causal, _dot_nt(qe, ke_bd), 0.0).astype(BF16)
        v_bd = jnp.concatenate(
            [jnp.where(vhead == hh, vv, 0.0) for hh in range(GLA_HEADS)], axis=0).astype(BF16)
        s_bd = jnp.concatenate(
            [jnp.concatenate([s_blocks[hh] if hc == hh else zblk for hc in range(GLA_HEADS)], axis=1)
             for hh in range(GLA_HEADS)], axis=0)
        o_s[rows, :] = _dot(att, v_bd) + _dot(qe, s_bd.astype(BF16))
        ecol = ecol_s[:, pl.ds(e0, GLA_DV)]
        upd = _dot_tn(kd, vv.astype(BF16))
        return [ecol[hh * GLA_DK:(hh + 1) * GLA_DK] * s_blocks[hh]
                + upd[hh * GLA_DK:(hh + 1) * GLA_DK, hh * GLA_DV:(hh + 1) * GLA_DV]
                for hh in range(GLA_HEADS)]

    if NS == 1:
        s_blocks = [gla_o_ref[0, hh] for hh in range(GLA_HEADS)]
        for c in range(n_units):
            s_blocks = gla_unit(c * C, c * GLA_DV, s_blocks)
        for hh in range(GLA_HEADS):
            gla_o_ref[0, hh] = s_blocks[hh]
    else:
        assert n_units == NS and NS % SEQ_UNROLL == 0

        def seq_group(g, carry):
            for i in range(SEQ_UNROLL):
                sq = g * SEQ_UNROLL + i
                new = gla_unit(pl.multiple_of(sq * C, C), pl.multiple_of(sq * GLA_DV, GLA_DV),
                               [gla_o_ref[sq, hh] for hh in range(GLA_HEADS)])
                for hh in range(GLA_HEADS):
                    gla_o_ref[sq, hh] = new[hh]
            return carry

        lax.fori_loop(0, NS // SEQ_UNROLL, seq_group, 0)

    gate = z_s[:, OFF_G:OFF_G + GLA_VW]
    o_all = o_s[...]
    c_parts = []
    for hh in range(GLA_HEADS):
        sl = slice(hh * GLA_DV, (hh + 1) * GLA_DV)
        gh = gate[:, sl]
        c_parts.append(_rms(o_all[:, sl], glag_ref[...]) * (gh * jax.nn.sigmoid(gh)))

    mix = jnp.concatenate([a_out, b_out] + c_parts, axis=-1).astype(BF16)
    xo_ref[...] = x + _dot(mix, wout_ref[...])


def _ffn_kernel(xp_ref, xs_ref, g_ref, wg_ref, wu_ref, wd_ref, fg_ref, op_ref, os_ref, *, n_prompt_steps, final):
    def ffn(x_ref, o_ref):
        x = x_ref[...]
        hf = _rms(x, g_ref[...]).astype(BF16)
        acc = x
        for lo, hi in FFN_CHUNKS:
            gt = _dot(hf, wg_ref[:, lo:hi])
            up = _dot(hf, wu_ref[:, lo:hi])
            act = (gt * jax.nn.sigmoid(gt) * up).astype(BF16)
            acc = acc + _dot(act, wd_ref[lo:hi, :])
        if final:
            acc = _rms(acc, fg_ref[...])
        o_ref[...] = acc

    i = pl.program_id(0)
    pl.when(i < n_prompt_steps)(lambda: ffn(xp_ref, op_ref))
    pl.when(i >= n_prompt_steps)(lambda: ffn(xs_ref, os_ref))


def _wspec(shape, layer):
    nd = len(shape)
    return pl.BlockSpec((None,) + tuple(shape), lambda *g, _l=layer, _n=nd: (_l,) + (0,) * _n,
                        pipeline_mode=pl.Buffered(1))


def _mix_call(layer, x2, pool0, gla0, wts, prev, *, nseq, L, NS, T, pos0, state_layer, emit_vn):
    R = NS * T
    n_chunk = L // T
    grid = (nseq // NS, n_chunk)
    (ng, win, wina, poolw, pscale, sgug, sguw, sgub, wa2, ba, glag, wout) = wts
    sl = state_layer
    in_specs = [
        pl.BlockSpec((R, D_MODEL), lambda i, j: (i * n_chunk + j, 0)),
        pl.BlockSpec((None, NS, POOL_BUF, POOL_WIDTH), lambda i, j: (sl, i, 0, 0)),
        pl.BlockSpec((None, NS, GLA_HEADS, GLA_DK, GLA_DV), lambda i, j: (sl, i, 0, 0, 0)),
        _wspec((1, D_MODEL), layer),
        _wspec((D_MODEL, IN_MAIN), layer),
        _wspec((D_MODEL, GLA_RANK_PAD), layer),
        _wspec((POOL_WIDTH, POOL_WIDTH), layer),
        _wspec((1, POOL_WIDTH), layer),
        _wspec((1, SGU_WIDTH), layer),
        _wspec(sguw.shape[1:], layer),
        _wspec(sgub.shape[1:], layer),
        _wspec((GLA_RANK_PAD, GLA_KW), layer),
        _wspec((1, GLA_KW), layer),
        _wspec((1, GLA_DV), layer),
        _wspec((D_MODEL, D_MODEL), layer),
    ]
    operands = [x2, pool0, gla0, ng, win, wina, poolw, pscale, sgug, sguw, sgub, wa2, ba, glag, wout]
    out_specs = [
        pl.BlockSpec((R, D_MODEL), lambda i, j: (i * n_chunk + j, 0)),
        pl.BlockSpec((None, NS, POOL_BUF, POOL_WIDTH), lambda i, j: (layer, i, 0, 0)),
        pl.BlockSpec((None, NS, GLA_HEADS, GLA_DK, GLA_DV), lambda i, j: (layer, i, 0, 0, 0)),
    ]
    out_shape = [
        jax.ShapeDtypeStruct((nseq * L, D_MODEL), F32),
        jax.ShapeDtypeStruct((DEPTH, nseq, POOL_BUF, POOL_WIDTH), F32),
        jax.ShapeDtypeStruct((DEPTH, nseq, GLA_HEADS, GLA_DK, GLA_DV), F32),
    ]
    if emit_vn:
        out_specs.append(pl.BlockSpec((None, R, SGU_WIDTH), lambda i, j: (layer, i * n_chunk + j, 0)))
        out_shape.append(jax.ShapeDtypeStruct((DEPTH, nseq * L, SGU_WIDTH), F32))
    aliases = {}
    if prev is not None:
        for k, arr in enumerate(prev):
            aliases[len(operands)] = 1 + k
            operands.append(arr)
            in_specs.append(pl.BlockSpec(memory_space=pl.ANY))
    scratch = [
        pltpu.VMEM((R, IN_MAIN), F32),
        pltpu.VMEM((NS, POOL_PAD + T, POOL_WIDTH), F32),
        pltpu.VMEM((R, GLA_KW), F32),
        pltpu.VMEM((R, GLA_KW), F32),
        pltpu.VMEM((R, GLA_KW), F32),
        pltpu.VMEM((GLA_KW, (R // min(GLA_CHUNK, T)) * GLA_DV), F32),
        pltpu.VMEM((R, GLA_VW), F32),
    ]
    outs = pl.pallas_call(
        functools.partial(_mix_kernel, NS=NS, T=T, pos0=pos0, emit_vn=emit_vn,
                          n_prev=0 if prev is None else len(prev)),
        grid=grid, in_specs=in_specs, out_specs=out_specs, out_shape=out_shape,
        scratch_shapes=scratch, input_output_aliases=aliases,
        compiler_params=pltpu.CompilerParams(
            dimension_semantics=("arbitrary", "arbitrary"), vmem_limit_bytes=VMEM_LIMIT),
        name=f"mix_T{T}",
    )(*operands)
    return outs[0], tuple(outs[1:])


def _ffn_call(layer, xp, xs, wts, final_g, *, TM, final):
    n_p, n_s = xp.shape[0] // TM, xs.shape[0] // TM
    g, wg, wu, wd = wts
    p_idx = lambda i: (jnp.minimum(i, n_p - 1), 0)
    s_idx = lambda i: (jnp.maximum(i - n_p, 0), 0)
    return pl.pallas_call(
        functools.partial(_ffn_kernel, n_prompt_steps=n_p, final=final),
        grid=(n_p + n_s,),
        in_specs=[
            pl.BlockSpec((TM, D_MODEL), p_idx),
            pl.BlockSpec((TM, D_MODEL), s_idx),
            _wspec((1, D_MODEL), layer),
            _wspec((D_MODEL, D_FF), layer),
            _wspec((D_MODEL, D_FF), layer),
            _wspec((D_FF, D_MODEL), layer),
            pl.BlockSpec((1, D_MODEL), lambda i: (0, 0)),
        ],
        out_specs=[pl.BlockSpec((TM, D_MODEL), p_idx), pl.BlockSpec((TM, D_MODEL), s_idx)],
        out_shape=[jax.ShapeDtypeStruct(xp.shape, F32), jax.ShapeDtypeStruct(xs.shape, F32)],
        compiler_params=pltpu.CompilerParams(
            dimension_semantics=("arbitrary",), vmem_limit_bytes=VMEM_LIMIT),
        name="ffn",
    )(xp, xs, g, wg, wu, wd, final_g)


def kernel(x_prompt, x_sample, state_pool, state_gla, attn_norm_g, w_in, pool_w, pool_scale, sgu_norm_g, sgu_ws, sgu_b, gla_wa2, gla_ba, gla_norm_g, w_out, ffn_norm_g, w_gate, w_up, w_down, final_norm_g):
    bp, seq, _ = x_prompt.shape
    bs, dseq, _ = x_sample.shape
    assert seq % PROMPT_TILE == 0 and dseq == 8 and bs % SAMPLE_SEQS == 0

    ng = attn_norm_g[:, None, :]
    win = w_in[:, :, :IN_MAIN].astype(BF16)
    wina = jnp.pad(w_in[:, :, IN_MAIN:].astype(BF16), ((0, 0), (0, 0), (0, GLA_RANK_PAD - GLA_RANK)))
    eye_g = jnp.eye(len(POOL_WINDOWS), dtype=F32)
    poolw = jnp.einsum('lgcd,gh->lgchd', pool_w, eye_g).reshape(DEPTH, POOL_WIDTH, POOL_WIDTH).astype(BF16)
    pscale = pool_scale[:, None, :]
    sgug = sgu_norm_g.reshape(DEPTH, 1, SGU_WIDTH)
    sguw_cat = sgu_ws.transpose(0, 2, 1, 3).reshape(DEPTH, SGU_CHUNK, SGU_HEADS * SGU_CHUNK)
    sgub_tile = jnp.repeat(sgu_b.transpose(0, 2, 1), SGU_HEAD_DIM, axis=-1)
    sguw_dec = jnp.repeat(sgu_ws[:, :, :dseq, :dseq].transpose(0, 3, 2, 1), SGU_HEAD_DIM, axis=-1)
    sgub_dec = sgub_tile[:, :dseq]
    wa2 = jnp.pad(gla_wa2, ((0, 0), (0, GLA_RANK_PAD - GLA_RANK), (0, 0))).astype(BF16)
    ba = gla_ba[:, None, :]
    glag = gla_norm_g[:, None, :]
    wout = w_out.astype(BF16)
    fng = ffn_norm_g[:, None, :]
    wg = w_gate.astype(BF16)
    wu = w_up.astype(BF16)
    wd = w_down.astype(BF16)
    fin = final_norm_g[None, :]

    xp = x_prompt.reshape(bp * seq, D_MODEL)
    xs = x_sample.reshape(bs * dseq, D_MODEL)
    pool0_p = jnp.zeros((1, bp, POOL_BUF, POOL_WIDTH), F32)
    gla0_p = jnp.zeros((1, bp, GLA_HEADS, GLA_DK, GLA_DV), F32)

    st_p = st_s = None
    for l in range(DEPTH):
        common = (ng, win, wina, poolw, pscale, sgug)
        tail = (wa2, ba, glag, wout)
        xp, st_p = _mix_call(l, xp, pool0_p, gla0_p, common + (sguw_cat, sgub_tile) + tail, st_p,
                             nseq=bp, L=seq, NS=1, T=PROMPT_TILE, pos0=0, state_layer=0, emit_vn=False)
        xs, st_s = _mix_call(l, xs, state_pool, state_gla, common + (sguw_dec, sgub_dec) + tail, st_s,
                             nseq=bs, L=dseq, NS=SAMPLE_SEQS, T=dseq, pos0=PAST_LEN, state_layer=l, emit_vn=True)
        xp, xs = _ffn_call(l, xp, xs, (fng, wg, wu, wd), fin, TM=FFN_ROWS, final=l == DEPTH - 1)

    return (xp.reshape(bp, seq, D_MODEL), xs.reshape(bs, dseq, D_MODEL),
            st_p[0], st_p[1], st_s[0], st_s[1], st_s[2].reshape(DEPTH, bs, dseq, SGU_WIDTH))
```

```python
import functools

import jax
import jax.numpy as jnp
from jax import lax
from jax.experimental import pallas as pl
from jax.experimental.pallas import tpu as pltpu

F32 = jnp.float32
BF16 = jnp.bfloat16

D_MODEL = 1024
DEPTH = 4
POOL_WIDTH = 256
POOL_WINDOWS = (2, 4, 8, 16)
POOL_GROUP_DIM = 64
POOL_BUF = 15
POOL_PAD = 16
SGU_WIDTH = 256
SGU_HEADS = 4
SGU_HEAD_DIM = 64
SGU_CHUNK = 128
GLA_HEADS = 4
GLA_DK = 64
GLA_DV = 128
GLA_KW = GLA_HEADS * GLA_DK
GLA_VW = GLA_HEADS * GLA_DV
GLA_RANK = 16
GLA_RANK_PAD = 128
GLA_GATE_NORM = 16.0
GLA_CHUNK = 64
MXU_COLS = 256
PART_ROWS = 256
SEQ_UNROLL = 8
D_FF = 2816
EPS = 1e-6

OFF_P, OFF_U, OFF_V, OFF_Q, OFF_K, OFF_VG, OFF_G, OFF_A = 0, 256, 512, 768, 1024, 1280, 1792, 2304
IN_MAIN = 2304
FFN_CHUNKS = ((0, 1536), (1536, 2816))

PAST_LEN = 16384
PROMPT_TILE = 1024
SAMPLE_SEQS = 32
FFN_ROWS = 512
VMEM_LIMIT = 56 * 1024 * 1024


def _rms(x, g):
    return x * lax.rsqrt(jnp.mean(x * x, axis=-1, keepdims=True) + EPS) * g


def _dot(a, b):
    return jnp.dot(a, b, preferred_element_type=F32)


def _dot_nt(a, b):
    return lax.dot_general(a, b, (((1,), (1,)), ((), ())), preferred_element_type=F32)


def _dot_tn(a, b):
    return lax.dot_general(a, b, (((0,), (0,)), ((), ())), preferred_element_type=F32)


def _split_bf16(x):
    hi = x.astype(BF16)
    lo = (x - hi.astype(F32)).astype(BF16)
    return hi, lo


def _mix_kernel(*refs, NS, T, pos0, emit_vn, n_prev):
    (x_ref, pool0_ref, gla0_ref, ng_ref, win_ref, wina_ref, poolw_ref, pscale_ref,
     sgug_ref, sguw_ref, sgub_ref, wa2_ref, ba_ref, glag_ref, wout_ref) = refs[:15]
    refs = refs[15 + n_prev:]
    xo_ref, pool_o_ref, gla_o_ref = refs[:3]
    vn_o_ref = refs[3] if emit_vn else None
    z_s, pz_s, qe_s, ke_s, kd_s, ecol_s, o_s = refs[-7:]
    j = pl.program_id(1)
    R = NS * T
    P = PART_ROWS
    TP = P // NS
    C = min(GLA_CHUNK, T)
    units = P // C
    assert R % P == 0 and (NS == 1 or R == P)

    @pl.when(j == 0)
    def _():
        pz_s[:, 0:1, :] = jnp.zeros((NS, 1, POOL_WIDTH), F32)
        pz_s[:, 1:POOL_PAD, :] = pool0_ref[...]
        gla_o_ref[...] = gla0_ref[...]

    grp = lax.broadcasted_iota(jnp.int32, (1, 1, POOL_WIDTH), 2) // POOL_GROUP_DIM
    win = jnp.where(grp == 0, 2, jnp.where(grp == 1, 4, jnp.where(grp == 2, 8, 16)))
    r_i = lax.broadcasted_iota(jnp.int32, (SGU_WIDTH, SGU_WIDTH), 0) // SGU_HEAD_DIM
    c_i = lax.broadcasted_iota(jnp.int32, (SGU_WIDTH, SGU_WIDTH), 1) // SGU_HEAD_DIM
    head_ones = (r_i == c_i).astype(BF16)
    lane_head = lax.broadcasted_iota(jnp.int32, (1, SGU_WIDTH), 1) // SGU_HEAD_DIM
    if T >= SGU_CHUNK:
        row = lax.broadcasted_iota(jnp.int32, (SGU_CHUNK, SGU_HEADS * SGU_CHUNK), 0)
        col = lax.broadcasted_iota(jnp.int32, (SGU_CHUNK, SGU_HEADS * SGU_CHUNK), 1)
        wcat = jnp.where((col & (SGU_CHUNK - 1)) <= row, sguw_ref[...], 0.0).astype(BF16)
    rr = lax.broadcasted_iota(jnp.int32, (P, P), 0)
    cc = lax.broadcasted_iota(jnp.int32, (P, P), 1)
    tri_bd = ((rr // C == cc // C) & (rr >= cc)).astype(BF16)
    sel2 = (lax.broadcasted_iota(jnp.int32, (8, 2 * GLA_DV), 0) // 2
            == lax.broadcasted_iota(jnp.int32, (8, 2 * GLA_DV), 1) // GLA_DV).astype(BF16)
    khead = lax.broadcasted_iota(jnp.int32, (1, GLA_KW), 1) // GLA_DK
    vhead = lax.broadcasted_iota(jnp.int32, (1, GLA_VW), 1) // GLA_DV
    arow = lax.broadcasted_iota(jnp.int32, (C, GLA_HEADS * C), 0)
    acol = lax.broadcasted_iota(jnp.int32, (C, GLA_HEADS * C), 1) & (C - 1)
    causal = acol <= arow
    zblk = jnp.zeros((GLA_DK, GLA_DV), F32)

    def gla_unit_stages(r0, e0, read_state, write_state):
        rows = pl.ds(r0, C)
        qe = qe_s[rows, :].astype(BF16)
        ke = ke_s[rows, :]
        kd = kd_s[rows, :].astype(BF16)
        vv = z_s[rows, OFF_VG:OFF_VG + GLA_VW]
        ke_bd = jnp.concatenate(
            [jnp.where(khead == hh, ke, 0.0) for hh in range(GLA_HEADS)], axis=0).astype(BF16)
        att_raw = _dot_nt(qe, ke_bd)
        upd = _dot_tn(kd, vv.astype(BF16))
        yield
        att = jnp.where(causal, att_raw, 0.0).astype(BF16)
        v_bd = jnp.concatenate(
            [jnp.where(vhead == hh, vv, 0.0) for hh in range(GLA_HEADS)], axis=0).astype(BF16)
        o_intra = _dot(att, v_bd)
        yield
        s_blocks = read_state()
        s_bd = jnp.concatenate(
            [jnp.concatenate([s_blocks[hh] if hc == hh else zblk for hc in range(GLA_HEADS)], axis=1)
             for hh in range(GLA_HEADS)], axis=0)
        o_s[rows, :] = o_intra + _dot(qe, s_bd.astype(BF16))
        ecol = ecol_s[:, pl.ds(e0, GLA_DV)]
        write_state([ecol[hh * GLA_DK:(hh + 1) * GLA_DK] * s_blocks[hh]
                     + upd[hh * GLA_DK:(hh + 1) * GLA_DK, hh * GLA_DV:(hh + 1) * GLA_DV]
                     for hh in range(GLA_HEADS)])

    def step(gen):
        try:
            next(gen)
        except StopIteration:
            pass

    def part_stages(part):
        r0p = part * P
        prow = slice(r0p, r0p + P)
        t0 = part * TP
        x = x_ref[prow, :]
        h = _rms(x, ng_ref[...]).astype(BF16)
        yield
        for n in range(IN_MAIN // MXU_COLS):
            cs = slice(n * MXU_COLS, (n + 1) * MXU_COLS)
            z_s[prow, cs] = _dot(h, win_ref[:, cs])
            yield
        alow = _dot(h, wina_ref[...]).astype(BF16)

        p3 = z_s[prow, OFF_P:OFF_P + POOL_WIDTH].reshape(NS, TP, POOL_WIDTH)
        pz_s[:, POOL_PAD + t0:POOL_PAD + t0 + TP, :] = p3
        e = pz_s[:, t0:t0 + POOL_PAD + TP, :]
        w2 = e + pltpu.roll(e, 1, 1)
        w4 = w2 + pltpu.roll(w2, 2, 1)
        w8 = w4 + pltpu.roll(w4, 4, 1)
        w16 = w8 + pltpu.roll(w8, 8, 1)
        wsum = jnp.where(grp == 0, w2, jnp.where(grp == 1, w4, jnp.where(grp == 2, w8, w16)))
        wsum = wsum[:, POOL_PAD:POOL_PAD + TP, :]
        pos = pos0 + j * T + t0 + lax.broadcasted_iota(jnp.int32, (1, TP, 1), 1)
        cnt = jnp.minimum(pos + 1, win).astype(F32)
        d = (wsum / cnt - p3).reshape(P, POOL_WIDTH)
        a_out = _dot(d.astype(BF16), poolw_ref[...]) * pscale_ref[...]
        yield

        v = z_s[prow, OFF_V:OFF_V + SGU_WIDTH]
        u = z_s[prow, OFF_U:OFF_U + SGU_WIDTH]
        vsq_hi, vsq_lo = _split_bf16(v * v)
        ss = _dot(vsq_hi, head_ones) + _dot(vsq_lo, head_ones)
        vn = v * lax.rsqrt(ss * (1.0 / SGU_HEAD_DIM) + EPS) * sgug_ref[...]
        if emit_vn:
            vn_o_ref[prow, :] = vn
        yield
        if T >= SGU_CHUNK:
            parts = []
            for c in range(P // SGU_CHUNK):
                vc = vn[c * SGU_CHUNK:(c + 1) * SGU_CHUNK]
                stack = jnp.concatenate(
                    [jnp.where(lane_head == hh, vc, 0.0) for hh in range(SGU_HEADS)], axis=0)
                parts.append(_dot(wcat, stack.astype(BF16)) + sgub_ref[...])
            s_gate = jnp.concatenate(parts, axis=0)
        else:
            vn3 = vn.reshape(NS, T, SGU_WIDTH)
            srow = lax.broadcasted_iota(jnp.int32, (T, SGU_WIDTH), 0)
            s3 = jnp.zeros((NS, T, SGU_WIDTH), F32) + sgub_ref[...][None]
            for jj in range(T):
                coef = jnp.where(srow >= jj, sguw_ref[jj], 0.0)
                s3 = s3 + vn3[:, jj:jj + 1, :] * coef[None]
            s_gate = s3.reshape(P, SGU_WIDTH)
        b_out = u * s_gate
        yield

        xg = _dot(alow, wa2_ref[...]) + ba_ref[...]
        lg = (jnp.minimum(xg, 0.0) - jnp.log(1.0 + jnp.exp(-jnp.abs(xg)))) * (1.0 / GLA_GATE_NORM)
        lg_hi, lg_lo = _split_bf16(lg)
        bcum = _dot(tri_bd, lg_hi) + _dot(tri_bd, lg_lo)
        b3 = bcum.reshape(units, C, GLA_KW)
        btot = jnp.broadcast_to(b3[:, C - 1:C, :], (units, C, GLA_KW)).reshape(P, GLA_KW)
        ke_all = z_s[prow, OFF_K:OFF_K + GLA_KW] * jnp.exp(-bcum)
        qe_s[prow, :] = z_s[prow, OFF_Q:OFF_Q + GLA_KW] * (GLA_DK ** -0.5) * jnp.exp(bcum)
        ke_s[prow, :] = ke_all
        kd_s[prow, :] = ke_all * jnp.exp(btot)
        yield
        etot = jnp.exp(b3[:, C - 1, :])
        et_hi = etot.astype(BF16).astype(F32)
        et_lo = etot - et_hi
        zpad = jnp.zeros((4, GLA_KW), F32)
        for pp in range(units // 2):
            a_rows = jnp.concatenate([et_hi[2 * pp:2 * pp + 1], et_lo[2 * pp:2 * pp + 1],
                                      et_hi[2 * pp + 1:2 * pp + 2], et_lo[2 * pp + 1:2 * pp + 2], zpad], axis=0)
            e0 = (part * units + 2 * pp) * GLA_DV
            ecol_s[:, e0:e0 + 2 * GLA_DV] = _dot_tn(a_rows.astype(BF16), sel2)
        yield

        if NS == 1:
            state = [[gla_o_ref[0, hh] for hh in range(GLA_HEADS)]]
            ug = [gla_unit_stages(r0p + c * C, (part * units + c) * GLA_DV,
                                  lambda: state[0], lambda new: state.__setitem__(0, new)) for c in range(units)]
            step(ug[0])
            step(ug[0])
            for c in range(units):
                if c + 1 < units:
                    step(ug[c + 1])
                step(ug[c])
                if c + 1 < units:
                    step(ug[c + 1])
                yield
            for hh in range(GLA_HEADS):
                gla_o_ref[0, hh] = state[0][hh]
        else:
            assert units == NS and NS % SEQ_UNROLL == 0

            def seq_group(g, carry):
                def unit(sq):
                    def write(new):
                        for hh in range(GLA_HEADS):
                            gla_o_ref[sq, hh] = new[hh]
                    return gla_unit_stages(pl.multiple_of(sq * C, C), pl.multiple_of(sq * GLA_DV, GLA_DV),
                                           lambda: [gla_o_ref[sq, hh] for hh in range(GLA_HEADS)], write)
                ug = [unit(g * SEQ_UNROLL + i) for i in range(SEQ_UNROLL)]
                for _ in range(3):
                    for u_gen in ug:
                        step(u_gen)
                return carry

            lax.fori_loop(0, NS // SEQ_UNROLL, seq_group, 0)

        gate = z_s[prow, OFF_G:OFF_G + GLA_VW]
        o_all = o_s[prow, :]
        c_parts = []
        for hh in range(GLA_HEADS):
            sl = slice(hh * GLA_DV, (hh + 1) * GLA_DV)
            gh = gate[:, sl]
            c_parts.append(_rms(o_all[:, sl], glag_ref[...]) * (gh * jax.nn.sigmoid(gh)))

        mix = jnp.concatenate([a_out, b_out] + c_parts, axis=-1).astype(BF16)
        yield
        for n in range(D_MODEL // MXU_COLS):
            cs = slice(n * MXU_COLS, (n + 1) * MXU_COLS)
            xo_ref[prow, cs] = x[:, cs] + _dot(mix, wout_ref[:, cs])
            yield


    n_proj = 1 + IN_MAIN // MXU_COLS
    gens = [part_stages(p) for p in range(R // P)]
    done = [False] * len(gens)

    def advance(p, n=1):
        for _ in range(n):
            if not done[p]:
                try:
                    next(gens[p])
                except StopIteration:
                    done[p] = True

    n_mix = 6 + units if NS == 1 else 7
    advance(0, n_proj)
    for p in range(len(gens)):
        for _ in range(n_mix):
            advance(p)
            if p + 1 < len(gens):
                advance(p + 1)
            if p > 0:
                advance(p - 1)
    for p in range(len(gens)):
        while not done[p]:
            advance(p)

    pool_o_ref[...] = pz_s[:, T + 1:T + POOL_PAD, :]
    if T >= POOL_PAD:
        pz_s[:, 0:POOL_PAD, :] = pz_s[:, T:T + POOL_PAD, :]


def _ffn_kernel(xp_ref, xs_ref, g_ref, wg_ref, wu_ref, wd_ref, fg_ref, op_ref, os_ref, *, n_prompt_steps, final):
    def ffn(x_ref, o_ref):
        x = x_ref[...]
        hf = _rms(x, g_ref[...]).astype(BF16)
        acc = x
        for lo, hi in FFN_CHUNKS:
            gt = _dot(hf, wg_ref[:, lo:hi])
            up = _dot(hf, wu_ref[:, lo:hi])
            act = (gt * jax.nn.sigmoid(gt) * up).astype(BF16)
            acc = acc + _dot(act, wd_ref[lo:hi, :])
        if final:
            acc = _rms(acc, fg_ref[...])
        o_ref[...] = acc

    i = pl.program_id(0)
    pl.when(i < n_prompt_steps)(lambda: ffn(xp_ref, op_ref))
    pl.when(i >= n_prompt_steps)(lambda: ffn(xs_ref, os_ref))


def _wspec(shape, layer):
    nd = len(shape)
    return pl.BlockSpec((None,) + tuple(shape), lambda *g, _l=layer, _n=nd: (_l,) + (0,) * _n,
                        pipeline_mode=pl.Buffered(1))


def _mix_call(layer, x2, pool0, gla0, wts, prev, *, nseq, L, NS, T, pos0, state_layer, emit_vn):
    R = NS * T
    n_chunk = L // T
    grid = (nseq // NS, n_chunk)
    (ng, win, wina, poolw, pscale, sgug, sguw, sgub, wa2, ba, glag, wout) = wts
    sl = state_layer
    in_specs = [
        pl.BlockSpec((R, D_MODEL), lambda i, j: (i * n_chunk + j, 0)),
        pl.BlockSpec((None, NS, POOL_BUF, POOL_WIDTH), lambda i, j: (sl, i, 0, 0)),
        pl.BlockSpec((None, NS, GLA_HEADS, GLA_DK, GLA_DV), lambda i, j: (sl, i, 0, 0, 0)),
        _wspec((1, D_MODEL), layer),
        _wspec((D_MODEL, IN_MAIN), layer),
        _wspec((D_MODEL, GLA_RANK_PAD), layer),
        _wspec((POOL_WIDTH, POOL_WIDTH), layer),
        _wspec((1, POOL_WIDTH), layer),
        _wspec((1, SGU_WIDTH), layer),
        _wspec(sguw.shape[1:], layer),
        _wspec(sgub.shape[1:], layer),
        _wspec((GLA_RANK_PAD, GLA_KW), layer),
        _wspec((1, GLA_KW), layer),
        _wspec((1, GLA_DV), layer),
        _wspec((D_MODEL, D_MODEL), layer),
    ]
    operands = [x2, pool0, gla0, ng, win, wina, poolw, pscale, sgug, sguw, sgub, wa2, ba, glag, wout]
    out_specs = [
        pl.BlockSpec((R, D_MODEL), lambda i, j: (i * n_chunk + j, 0)),
        pl.BlockSpec((None, NS, POOL_BUF, POOL_WIDTH), lambda i, j: (layer, i, 0, 0)),
        pl.BlockSpec((None, NS, GLA_HEADS, GLA_DK, GLA_DV), lambda i, j: (layer, i, 0, 0, 0)),
    ]
    out_shape = [
        jax.ShapeDtypeStruct((nseq * L, D_MODEL), F32),
        jax.ShapeDtypeStruct((DEPTH, nseq, POOL_BUF, POOL_WIDTH), F32),
        jax.ShapeDtypeStruct((DEPTH, nseq, GLA_HEADS, GLA_DK, GLA_DV), F32),
    ]
    if emit_vn:
        out_specs.append(pl.BlockSpec((None, R, SGU_WIDTH), lambda i, j: (layer, i * n_chunk + j, 0)))
        out_shape.append(jax.ShapeDtypeStruct((DEPTH, nseq * L, SGU_WIDTH), F32))
    aliases = {}
    if prev is not None:
        for k, arr in enumerate(prev):
            aliases[len(operands)] = 1 + k
            operands.append(arr)
            in_specs.append(pl.BlockSpec(memory_space=pl.ANY))
    scratch = [
        pltpu.VMEM((R, IN_MAIN), F32),
        pltpu.VMEM((NS, POOL_PAD + T, POOL_WIDTH), F32),
        pltpu.VMEM((R, GLA_KW), F32),
        pltpu.VMEM((R, GLA_KW), F32),
        pltpu.VMEM((R, GLA_KW), F32),
        pltpu.VMEM((GLA_KW, (R // min(GLA_CHUNK, T)) * GLA_DV), F32),
        pltpu.VMEM((R, GLA_VW), F32),
    ]
    outs = pl.pallas_call(
        functools.partial(_mix_kernel, NS=NS, T=T, pos0=pos0, emit_vn=emit_vn,
                          n_prev=0 if prev is None else len(prev)),
        grid=grid, in_specs=in_specs, out_specs=out_specs, out_shape=out_shape,
        scratch_shapes=scratch, input_output_aliases=aliases,
        compiler_params=pltpu.CompilerParams(
            dimension_semantics=("arbitrary", "arbitrary"), vmem_limit_bytes=VMEM_LIMIT),
        name=f"mix_T{T}",
    )(*operands)
    return outs[0], tuple(outs[1:])


def _ffn_call(layer, xp, xs, wts, final_g, *, TM, final):
    n_p, n_s = xp.shape[0] // TM, xs.shape[0] // TM
    g, wg, wu, wd = wts
    p_idx = lambda i: (jnp.minimum(i, n_p - 1), 0)
    s_idx = lambda i: (jnp.maximum(i - n_p, 0), 0)
    return pl.pallas_call(
        functools.partial(_ffn_kernel, n_prompt_steps=n_p, final=final),
        grid=(n_p + n_s,),
        in_specs=[
            pl.BlockSpec((TM, D_MODEL), p_idx),
            pl.BlockSpec((TM, D_MODEL), s_idx),
            _wspec((1, D_MODEL), layer),
            _wspec((D_MODEL, D_FF), layer),
            _wspec((D_MODEL, D_FF), layer),
            _wspec((D_FF, D_MODEL), layer),
            pl.BlockSpec((1, D_MODEL), lambda i: (0, 0)),
        ],
        out_specs=[pl.BlockSpec((TM, D_MODEL), p_idx), pl.BlockSpec((TM, D_MODEL), s_idx)],
        out_shape=[jax.ShapeDtypeStruct(xp.shape, F32), jax.ShapeDtypeStruct(xs.shape, F32)],
        compiler_params=pltpu.CompilerParams(
            dimension_semantics=("arbitrary",), vmem_limit_bytes=VMEM_LIMIT),
        name="ffn",
    )(xp, xs, g, wg, wu, wd, final_g)


def kernel(x_prompt, x_sample, state_pool, state_gla, attn_norm_g, w_in, pool_w, pool_scale, sgu_norm_g, sgu_ws, sgu_b, gla_wa2, gla_ba, gla_norm_g, w_out, ffn_norm_g, w_gate, w_up, w_down, final_norm_g):
    bp, seq, _ = x_prompt.shape
    bs, dseq, _ = x_sample.shape
    assert seq % PROMPT_TILE == 0 and dseq == 8 and bs % SAMPLE_SEQS == 0

    ng = attn_norm_g[:, None, :]
    win = w_in.astype(BF16)
    wina = jnp.pad(w_in[:, :, IN_MAIN:], ((0, 0), (0, 0), (0, GLA_RANK_PAD - GLA_RANK))).astype(BF16)
    eye_g = jnp.eye(len(POOL_WINDOWS), dtype=F32)
    poolw = jnp.einsum('lgcd,gh->lgchd', pool_w, eye_g).reshape(DEPTH, POOL_WIDTH, POOL_WIDTH).astype(BF16)
    pscale = pool_scale[:, None, :]
    sgug = sgu_norm_g.reshape(DEPTH, 1, SGU_WIDTH)
    sguw_cat = sgu_ws.transpose(0, 2, 1, 3).reshape(DEPTH, SGU_CHUNK, SGU_HEADS * SGU_CHUNK)
    sgub_tile = jnp.repeat(sgu_b.transpose(0, 2, 1), SGU_HEAD_DIM, axis=-1)
    sguw_dec = jnp.repeat(sgu_ws[:, :, :dseq, :dseq].transpose(0, 3, 2, 1), SGU_HEAD_DIM, axis=-1)
    sgub_dec = sgub_tile[:, :dseq]
    wa2 = jnp.pad(gla_wa2, ((0, 0), (0, GLA_RANK_PAD - GLA_RANK), (0, 0))).astype(BF16)
    ba = gla_ba[:, None, :]
    glag = gla_norm_g[:, None, :]
    wout = w_out.astype(BF16)
    fng = ffn_norm_g[:, None, :]
    wg = w_gate.astype(BF16)
    wu = w_up.astype(BF16)
    wd = w_down.astype(BF16)
    fin = final_norm_g[None, :]

    xp = x_prompt.reshape(bp * seq, D_MODEL)
    xs = x_sample.reshape(bs * dseq, D_MODEL)
    pool0_p = jnp.zeros((1, bp, POOL_BUF, POOL_WIDTH), F32)
    gla0_p = jnp.zeros((1, bp, GLA_HEADS, GLA_DK, GLA_DV), F32)

    st_p = st_s = None
    for l in range(DEPTH):
        common = (ng, win, wina, poolw, pscale, sgug)
        tail = (wa2, ba, glag, wout)
        xp, st_p = _mix_call(l, xp, pool0_p, gla0_p, common + (sguw_cat, sgub_tile) + tail, st_p,
                             nseq=bp, L=seq, NS=1, T=PROMPT_TILE, pos0=0, state_layer=0, emit_vn=False)
        xs, st_s = _mix_call(l, xs, state_pool, state_gla, common + (sguw_dec, sgub_dec) + tail, st_s,
                             nseq=bs, L=dseq, NS=SAMPLE_SEQS, T=dseq, pos0=PAST_LEN, state_layer=l, emit_vn=True)
        xp, xs = _ffn_call(l, xp, xs, (fng, wg, wu, wd), fin, TM=FFN_ROWS, final=l == DEPTH - 1)

    return (xp.reshape(bp, seq, D_MODEL), xs.reshape(bs, dseq, D_MODEL),
            st_p[0], st_p[1], st_s[0], st_s[1], st_s[2].reshape(DEPTH, bs, dseq, SGU_WIDTH))
```

```python
import functools

import jax
import jax.numpy as jnp
from jax import lax
from jax.experimental import pallas as pl
from jax.experimental.pallas import tpu as pltpu

F32 = jnp.float32
BF16 = jnp.bfloat16

D_MODEL = 1024
DEPTH = 4
POOL_WIDTH = 256
POOL_WINDOWS = (2, 4, 8, 16)
POOL_GROUP_DIM = 64
POOL_BUF = 15
POOL_PAD = 16
SGU_WIDTH = 256
SGU_HEADS = 4
SGU_HEAD_DIM = 64
SGU_CHUNK = 128
GLA_HEADS = 4
GLA_DK = 64
GLA_DV = 128
GLA_KW = GLA_HEADS * GLA_DK
GLA_VW = GLA_HEADS * GLA_DV
GLA_RANK = 16
GLA_RANK_PAD = 128
GLA_GATE_NORM = 16.0
GLA_CHUNK = 64
MXU_COLS = 256
PART_ROWS = 256
SEQ_UNROLL = 8
D_FF = 2816
EPS = 1e-6

OFF_P, OFF_U, OFF_V, OFF_Q, OFF_K, OFF_VG, OFF_G, OFF_A = 0, 256, 512, 768, 1024, 1280, 1792, 2304
IN_MAIN = 2304
FFN_CHUNKS = ((0, 1536), (1536, 2816))

PAST_LEN = 16384
PROMPT_TILE = 1024
SAMPLE_SEQS = 32
FFN_ROWS = 512
VMEM_LIMIT = 56 * 1024 * 1024


def _rms(x, g):
    return x * lax.rsqrt(jnp.mean(x * x, axis=-1, keepdims=True) + EPS) * g


def _dot(a, b):
    return jnp.dot(a, b, preferred_element_type=F32)


def _dot_nt(a, b):
    return lax.dot_general(a, b, (((1,), (1,)), ((), ())), preferred_element_type=F32)


def _dot_tn(a, b):
    return lax.dot_general(a, b, (((0,), (0,)), ((), ())), preferred_element_type=F32)


def _split_bf16(x):
    hi = x.astype(BF16)
    lo = (x - hi.astype(F32)).astype(BF16)
    return hi, lo


def _mix_kernel(*refs, NS, T, pos0, emit_vn, n_prev):
    (x_ref, pool0_ref, gla0_ref, ng_ref, win_ref, wina_ref, poolw_ref, pscale_ref,
     sgug_ref, sguw_ref, sgub_ref, wa2_ref, ba_ref, glag_ref, wout_ref) = refs[:15]
    refs = refs[15 + n_prev:]
    xo_ref, pool_o_ref, gla_o_ref = refs[:3]
    vn_o_ref = refs[3] if emit_vn else None
    z_s, pz_s, qe_s, ke_s, kd_s, ecol_s, o_s = refs[-7:]
    j = pl.program_id(1)
    R = NS * T
    P = PART_ROWS
    TP = P // NS
    C = min(GLA_CHUNK, T)
    units = P // C
    assert R % P == 0 and (NS == 1 or R == P)

    @pl.when(j == 0)
    def _():
        pz_s[:, 0:1, :] = jnp.zeros((NS, 1, POOL_WIDTH), F32)
        pz_s[:, 1:POOL_PAD, :] = pool0_ref[...]
        gla_o_ref[...] = gla0_ref[...]

    grp = lax.broadcasted_iota(jnp.int32, (1, 1, POOL_WIDTH), 2) // POOL_GROUP_DIM
    win = jnp.where(grp == 0, 2, jnp.where(grp == 1, 4, jnp.where(grp == 2, 8, 16)))
    r_i = lax.broadcasted_iota(jnp.int32, (SGU_WIDTH, SGU_WIDTH), 0) // SGU_HEAD_DIM
    c_i = lax.broadcasted_iota(jnp.int32, (SGU_WIDTH, SGU_WIDTH), 1) // SGU_HEAD_DIM
    head_ones = (r_i == c_i).astype(BF16)
    lane_head = lax.broadcasted_iota(jnp.int32, (1, SGU_WIDTH), 1) // SGU_HEAD_DIM
    if T >= SGU_CHUNK:
        row = lax.broadcasted_iota(jnp.int32, (SGU_CHUNK, SGU_HEADS * SGU_CHUNK), 0)
        col = lax.broadcasted_iota(jnp.int32, (SGU_CHUNK, SGU_HEADS * SGU_CHUNK), 1)
        wcat = jnp.where((col & (SGU_CHUNK - 1)) <= row, sguw_ref[...], 0.0).astype(BF16)
    rr = lax.broadcasted_iota(jnp.int32, (P, P), 0)
    cc = lax.broadcasted_iota(jnp.int32, (P, P), 1)
    tri_bd = ((rr // C == cc // C) & (rr >= cc)).astype(BF16)
    sel2 = (lax.broadcasted_iota(jnp.int32, (8, 2 * GLA_DV), 0) // 2
            == lax.broadcasted_iota(jnp.int32, (8, 2 * GLA_DV), 1) // GLA_DV).astype(BF16)
    khead = lax.broadcasted_iota(jnp.int32, (1, GLA_KW), 1) // GLA_DK
    vhead = lax.broadcasted_iota(jnp.int32, (1, GLA_VW), 1) // GLA_DV
    arow = lax.broadcasted_iota(jnp.int32, (C, GLA_HEADS * C), 0)
    acol = lax.broadcasted_iota(jnp.int32, (C, GLA_HEADS * C), 1) & (C - 1)
    causal = acol <= arow
    zblk = jnp.zeros((GLA_DK, GLA_DV), F32)

    def gla_unit_stages(r0, e0, read_state, write_state):
        rows = pl.ds(r0, C)
        qe = qe_s[rows, :].astype(BF16)
        ke = ke_s[rows, :]
        kd = kd_s[rows, :]
        vv = z_s[rows, OFF_VG:OFF_VG + GLA_VW]
        ke_bd = jnp.concatenate(
            [jnp.where(khead == hh, ke, 0.0) for hh in range(GLA_HEADS)], axis=0).astype(BF16)
        att_raw = _dot_nt(qe, ke_bd)
        v_bd = jnp.concatenate(
            [jnp.where(vhead == hh, vv, 0.0) for hh in range(GLA_HEADS)], axis=0).astype(BF16)
        kd_st = jnp.concatenate(
            [kd[:, hh * GLA_DK:(hh + 1) * GLA_DK] for hh in range(GLA_HEADS)], axis=0).astype(BF16)
        upd = _dot_tn(kd_st, v_bd)
        yield
        att = jnp.where(causal, att_raw, 0.0).astype(BF16)
        o_intra = _dot(att, v_bd)
        yield
        s_blocks = read_state()
        s_bd = jnp.concatenate(
            [jnp.concatenate([s_blocks[hh] if hc == hh else zblk for hc in range(GLA_HEADS)], axis=1)
             for hh in range(GLA_HEADS)], axis=0)
        o_s[rows, :] = o_intra + _dot(qe, s_bd.astype(BF16))
        ecol = ecol_s[:, pl.ds(e0, GLA_DV)]
        write_state([ecol[hh * GLA_DK:(hh + 1) * GLA_DK] * s_blocks[hh] + upd[:, hh * GLA_DV:(hh + 1) * GLA_DV]
                     for hh in range(GLA_HEADS)])

    def step(gen):
        try:
            next(gen)
        except StopIteration:
            pass

    def part_stages(part):
        r0p = part * P
        prow = slice(r0p, r0p + P)
        t0 = part * TP
        x = x_ref[prow, :]
        h = _rms(x, ng_ref[...]).astype(BF16)
        yield
        for n in range(IN_MAIN // MXU_COLS):
            cs = slice(n * MXU_COLS, (n + 1) * MXU_COLS)
            z_s[prow, cs] = _dot(h, win_ref[:, cs])
            yield
        alow = _dot(h, wina_ref[...]).astype(BF16)

        p3 = z_s[prow, OFF_P:OFF_P + POOL_WIDTH].reshape(NS, TP, POOL_WIDTH)
        pz_s[:, POOL_PAD + t0:POOL_PAD + t0 + TP, :] = p3
        e = pz_s[:, t0:t0 + POOL_PAD + TP, :]
        w2 = e + pltpu.roll(e, 1, 1)
        w4 = w2 + pltpu.roll(w2, 2, 1)
        w8 = w4 + pltpu.roll(w4, 4, 1)
        w16 = w8 + pltpu.roll(w8, 8, 1)
        wsum = jnp.where(grp == 0, w2, jnp.where(grp == 1, w4, jnp.where(grp == 2, w8, w16)))
        wsum = wsum[:, POOL_PAD:POOL_PAD + TP, :]
        pos = pos0 + j * T + t0 + lax.broadcasted_iota(jnp.int32, (1, TP, 1), 1)
        cnt = jnp.minimum(pos + 1, win).astype(F32)
        d = (wsum / cnt - p3).reshape(P, POOL_WIDTH)
        a_out = _dot(d.astype(BF16), poolw_ref[...]) * pscale_ref[...]
        yield

        v = z_s[prow, OFF_V:OFF_V + SGU_WIDTH]
        u = z_s[prow, OFF_U:OFF_U + SGU_WIDTH]
        vsq_hi, vsq_lo = _split_bf16(v * v)
        ss = _dot(vsq_hi, head_ones) + _dot(vsq_lo, head_ones)
        vn = v * lax.rsqrt(ss * (1.0 / SGU_HEAD_DIM) + EPS) * sgug_ref[...]
        if emit_vn:
            vn_o_ref[prow, :] = vn
        yield
        if T >= SGU_CHUNK:
            parts = []
            for c in range(P // SGU_CHUNK):
                vc = vn[c * SGU_CHUNK:(c + 1) * SGU_CHUNK]
                stack = jnp.concatenate(
                    [jnp.where(lane_head == hh, vc, 0.0) for hh in range(SGU_HEADS)], axis=0)
                parts.append(_dot(wcat, stack.astype(BF16)) + sgub_ref[...])
            s_gate = jnp.concatenate(parts, axis=0)
        else:
            vn3 = vn.reshape(NS, T, SGU_WIDTH)
            srow = lax.broadcasted_iota(jnp.int32, (T, SGU_WIDTH), 0)
            s3 = jnp.zeros((NS, T, SGU_WIDTH), F32) + sgub_ref[...][None]
            for jj in range(T):
                coef = jnp.where(srow >= jj, sguw_ref[jj], 0.0)
                s3 = s3 + vn3[:, jj:jj + 1, :] * coef[None]
            s_gate = s3.reshape(P, SGU_WIDTH)
        b_out = u * s_gate
        yield

        xg = _dot(alow, wa2_ref[...]) + ba_ref[...]
        lg = (jnp.minimum(xg, 0.0) - jnp.log(1.0 + jnp.exp(-jnp.abs(xg)))) * (1.0 / GLA_GATE_NORM)
        lg_hi, lg_lo = _split_bf16(lg)
        bcum = _dot(tri_bd, lg_hi) + _dot(tri_bd, lg_lo)
        b3 = bcum.reshape(units, C, GLA_KW)
        btot = jnp.broadcast_to(b3[:, C - 1:C, :], (units, C, GLA_KW)).reshape(P, GLA_KW)
        ke_all = z_s[prow, OFF_K:OFF_K + GLA_KW] * jnp.exp(-bcum)
        qe_s[prow, :] = z_s[prow, OFF_Q:OFF_Q + GLA_KW] * (GLA_DK ** -0.5) * jnp.exp(bcum)
        ke_s[prow, :] = ke_all
        kd_s[prow, :] = ke_all * jnp.exp(btot)
        yield
        etot = jnp.exp(b3[:, C - 1, :])
        et_hi = etot.astype(BF16).astype(F32)
        et_lo = etot - et_hi
        zpad = jnp.zeros((4, GLA_KW), F32)
        for pp in range(units // 2):
            a_rows = jnp.concatenate([et_hi[2 * pp:2 * pp + 1], et_lo[2 * pp:2 * pp + 1],
                                      et_hi[2 * pp + 1:2 * pp + 2], et_lo[2 * pp + 1:2 * pp + 2], zpad], axis=0)
            e0 = (part * units + 2 * pp) * GLA_DV
            ecol_s[:, e0:e0 + 2 * GLA_DV] = _dot_tn(a_rows.astype(BF16), sel2)
        yield

        if NS == 1:
            state = [[gla_o_ref[0, hh] for hh in range(GLA_HEADS)]]
            ug = [gla_unit_stages(r0p + c * C, (part * units + c) * GLA_DV,
                                  lambda: state[0], lambda new: state.__setitem__(0, new)) for c in range(units)]
            step(ug[0])
            step(ug[0])
            for c in range(units):
                if c + 1 < units:
                    step(ug[c + 1])
                step(ug[c])
                if c + 1 < units:
                    step(ug[c + 1])
                yield
            for hh in range(GLA_HEADS):
                gla_o_ref[0, hh] = state[0][hh]
        else:
            assert units == NS and NS % SEQ_UNROLL == 0

            def seq_group(g, carry):
                def unit(sq):
                    def write(new):
                        for hh in range(GLA_HEADS):
                            gla_o_ref[sq, hh] = new[hh]
                    return gla_unit_stages(pl.multiple_of(sq * C, C), pl.multiple_of(sq * GLA_DV, GLA_DV),
                                           lambda: [gla_o_ref[sq, hh] for hh in range(GLA_HEADS)], write)
                ug = [unit(g * SEQ_UNROLL + i) for i in range(SEQ_UNROLL)]
                for _ in range(3):
                    for u_gen in ug:
                        step(u_gen)
                return carry

            lax.fori_loop(0, NS // SEQ_UNROLL, seq_group, 0)

        gate = z_s[prow, OFF_G:OFF_G + GLA_VW]
        o_all = o_s[prow, :]
        c_parts = []
        for hh in range(GLA_HEADS):
            sl = slice(hh * GLA_DV, (hh + 1) * GLA_DV)
            gh = gate[:, sl]
            c_parts.append(_rms(o_all[:, sl], glag_ref[...]) * (gh * jax.nn.sigmoid(gh)))

        mix = jnp.concatenate([a_out, b_out] + c_parts, axis=-1).astype(BF16)
        yield
        for n in range(D_MODEL // MXU_COLS):
            cs = slice(n * MXU_COLS, (n + 1) * MXU_COLS)
            xo_ref[prow, cs] = x[:, cs] + _dot(mix, wout_ref[:, cs])
            yield


    n_proj = 1 + IN_MAIN // MXU_COLS
    gens = [part_stages(p) for p in range(R // P)]
    done = [False] * len(gens)

    def advance(p, n=1):
        for _ in range(n):
            if not done[p]:
                try:
                    next(gens[p])
                except StopIteration:
                    done[p] = True

    n_mix = 6 + units if NS == 1 else 7
    advance(0, n_proj)
    for p in range(len(gens)):
        for _ in range(n_mix):
            advance(p)
            if p + 1 < len(gens):
                advance(p + 1)
            if p > 0:
                advance(p - 1)
    for p in range(len(gens)):
        while not done[p]:
            advance(p)

    pool_o_ref[...] = pz_s[:, T + 1:T + POOL_PAD, :]
    if T >= POOL_PAD:
        pz_s[:, 0:POOL_PAD, :] = pz_s[:, T:T + POOL_PAD, :]


def _ffn_kernel(xp_ref, xs_ref, g_ref, wg_ref, wu_ref, wd_ref, fg_ref, op_ref, os_ref, *, n_prompt_steps, final):
    def part_stages(x_ref, o_ref, rows):
        x = x_ref[rows, :]
        hf = _rms(x, g_ref[...]).astype(BF16)
        yield
        acc = x
        for lo, hi in FFN_CHUNKS:
            gt = _dot(hf, wg_ref[:, lo:hi])
            yield
            up = _dot(hf, wu_ref[:, lo:hi])
            yield
            act = (gt * jax.nn.sigmoid(gt) * up).astype(BF16)
            acc = acc + _dot(act, wd_ref[lo:hi, :])
            yield
        if final:
            acc = _rms(acc, fg_ref[...])
        o_ref[rows, :] = acc

    def ffn(x_ref, o_ref):
        n_rows = x_ref.shape[0]
        gens = [part_stages(x_ref, o_ref, slice(r, r + PART_ROWS)) for r in range(0, n_rows, PART_ROWS)]
        live = list(gens)
        lag = 0
        while live:
            for g in list(live[:lag + 1]):
                try:
                    next(g)
                except StopIteration:
                    live.remove(g)
            lag += 1

    i = pl.program_id(0)
    pl.when(i < n_prompt_steps)(lambda: ffn(xp_ref, op_ref))
    pl.when(i >= n_prompt_steps)(lambda: ffn(xs_ref, os_ref))


def _wspec(shape, layer):
    nd = len(shape)
    return pl.BlockSpec((None,) + tuple(shape), lambda *g, _l=layer, _n=nd: (_l,) + (0,) * _n,
                        pipeline_mode=pl.Buffered(1))


def _mix_call(layer, x2, pool0, gla0, wts, prev, *, nseq, L, NS, T, pos0, state_layer, emit_vn):
    R = NS * T
    n_chunk = L // T
    grid = (nseq // NS, n_chunk)
    (ng, win, wina, poolw, pscale, sgug, sguw, sgub, wa2, ba, glag, wout) = wts
    sl = state_layer
    in_specs = [
        pl.BlockSpec((R, D_MODEL), lambda i, j: (i * n_chunk + j, 0)),
        pl.BlockSpec((None, NS, POOL_BUF, POOL_WIDTH), lambda i, j: (sl, i, 0, 0)),
        pl.BlockSpec((None, NS, GLA_HEADS, GLA_DK, GLA_DV), lambda i, j: (sl, i, 0, 0, 0)),
        _wspec((1, D_MODEL), layer),
        _wspec((D_MODEL, IN_MAIN), layer),
        _wspec((D_MODEL, GLA_RANK_PAD), layer),
        _wspec((POOL_WIDTH, POOL_WIDTH), layer),
        _wspec((1, POOL_WIDTH), layer),
        _wspec((1, SGU_WIDTH), layer),
        _wspec(sguw.shape[1:], layer),
        _wspec(sgub.shape[1:], layer),
        _wspec((GLA_RANK_PAD, GLA_KW), layer),
        _wspec((1, GLA_KW), layer),
        _wspec((1, GLA_DV), layer),
        _wspec((D_MODEL, D_MODEL), layer),
    ]
    operands = [x2, pool0, gla0, ng, win, wina, poolw, pscale, sgug, sguw, sgub, wa2, ba, glag, wout]
    out_specs = [
        pl.BlockSpec((R, D_MODEL), lambda i, j: (i * n_chunk + j, 0)),
        pl.BlockSpec((None, NS, POOL_BUF, POOL_WIDTH), lambda i, j: (layer, i, 0, 0)),
        pl.BlockSpec((None, NS, GLA_HEADS, GLA_DK, GLA_DV), lambda i, j: (layer, i, 0, 0, 0)),
    ]
    out_shape = [
        jax.ShapeDtypeStruct((nseq * L, D_MODEL), F32),
        jax.ShapeDtypeStruct((DEPTH, nseq, POOL_BUF, POOL_WIDTH), F32),
        jax.ShapeDtypeStruct((DEPTH, nseq, GLA_HEADS, GLA_DK, GLA_DV), F32),
    ]
    if emit_vn:
        out_specs.append(pl.BlockSpec((None, R, SGU_WIDTH), lambda i, j: (layer, i * n_chunk + j, 0)))
        out_shape.append(jax.ShapeDtypeStruct((DEPTH, nseq * L, SGU_WIDTH), F32))
    aliases = {}
    if prev is not None:
        for k, arr in enumerate(prev):
            aliases[len(operands)] = 1 + k
            operands.append(arr)
            in_specs.append(pl.BlockSpec(memory_space=pl.ANY))
    scratch = [
        pltpu.VMEM((R, IN_MAIN), F32),
        pltpu.VMEM((NS, POOL_PAD + T, POOL_WIDTH), F32),
        pltpu.VMEM((R, GLA_KW), F32),
        pltpu.VMEM((R, GLA_KW), F32),
        pltpu.VMEM((R, GLA_KW), F32),
        pltpu.VMEM((GLA_KW, (R // min(GLA_CHUNK, T)) * GLA_DV), F32),
        pltpu.VMEM((R, GLA_VW), F32),
    ]
    outs = pl.pallas_call(
        functools.partial(_mix_kernel, NS=NS, T=T, pos0=pos0, emit_vn=emit_vn,
                          n_prev=0 if prev is None else len(prev)),
        grid=grid, in_specs=in_specs, out_specs=out_specs, out_shape=out_shape,
        scratch_shapes=scratch, input_output_aliases=aliases,
        compiler_params=pltpu.CompilerParams(
            dimension_semantics=("arbitrary", "arbitrary"), vmem_limit_bytes=VMEM_LIMIT),
        name=f"mix_T{T}",
    )(*operands)
    return outs[0], tuple(outs[1:])


def _ffn_call(layer, xp, xs, wts, final_g, *, TM, final):
    n_p, n_s = xp.shape[0] // TM, xs.shape[0] // TM
    g, wg, wu, wd = wts
    p_idx = lambda i: (jnp.minimum(i, n_p - 1), 0)
    s_idx = lambda i: (jnp.maximum(i - n_p, 0), 0)
    return pl.pallas_call(
        functools.partial(_ffn_kernel, n_prompt_steps=n_p, final=final),
        grid=(n_p + n_s,),
        in_specs=[
            pl.BlockSpec((TM, D_MODEL), p_idx),
            pl.BlockSpec((TM, D_MODEL), s_idx),
            _wspec((1, D_MODEL), layer),
            _wspec((D_MODEL, D_FF), layer),
            _wspec((D_MODEL, D_FF), layer),
            _wspec((D_FF, D_MODEL), layer),
            pl.BlockSpec((1, D_MODEL), lambda i: (0, 0)),
        ],
        out_specs=[pl.BlockSpec((TM, D_MODEL), p_idx), pl.BlockSpec((TM, D_MODEL), s_idx)],
        out_shape=[jax.ShapeDtypeStruct(xp.shape, F32), jax.ShapeDtypeStruct(xs.shape, F32)],
        compiler_params=pltpu.CompilerParams(
            dimension_semantics=("arbitrary",), vmem_limit_bytes=VMEM_LIMIT),
        name="ffn",
    )(xp, xs, g, wg, wu, wd, final_g)


def kernel(x_prompt, x_sample, state_pool, state_gla, attn_norm_g, w_in, pool_w, pool_scale, sgu_norm_g, sgu_ws, sgu_b, gla_wa2, gla_ba, gla_norm_g, w_out, ffn_norm_g, w_gate, w_up, w_down, final_norm_g):
    bp, seq, _ = x_prompt.shape
    bs, dseq, _ = x_sample.shape
    assert seq % PROMPT_TILE == 0 and dseq == 8 and bs % SAMPLE_SEQS == 0

    ng = attn_norm_g[:, None, :]
    win = w_in.astype(BF16)
    wina = jnp.pad(w_in[:, :, IN_MAIN:], ((0, 0), (0, 0), (0, GLA_RANK_PAD - GLA_RANK))).astype(BF16)
    eye_g = jnp.eye(len(POOL_WINDOWS), dtype=F32)
    poolw = jnp.einsum('lgcd,gh->lgchd', pool_w, eye_g).reshape(DEPTH, POOL_WIDTH, POOL_WIDTH).astype(BF16)
    pscale = pool_scale[:, None, :]
    sgug = sgu_norm_g.reshape(DEPTH, 1, SGU_WIDTH)
    sguw_cat = sgu_ws.transpose(0, 2, 1, 3).reshape(DEPTH, SGU_CHUNK, SGU_HEADS * SGU_CHUNK)
    sgub_tile = jnp.repeat(sgu_b.transpose(0, 2, 1), SGU_HEAD_DIM, axis=-1)
    sguw_dec = jnp.repeat(sgu_ws[:, :, :dseq, :dseq].transpose(0, 3, 2, 1), SGU_HEAD_DIM, axis=-1)
    sgub_dec = sgub_tile[:, :dseq]
    wa2 = jnp.pad(gla_wa2, ((0, 0), (0, GLA_RANK_PAD - GLA_RANK), (0, 0))).astype(BF16)
    ba = gla_ba[:, None, :]
    glag = gla_norm_g[:, None, :]
    wout = w_out.astype(BF16)
    fng = ffn_norm_g[:, None, :]
    wg = w_gate.astype(BF16)
    wu = w_up.astype(BF16)
    wd = w_down.astype(BF16)
    fin = final_norm_g[None, :]

    xp = x_prompt.reshape(bp * seq, D_MODEL)
    xs = x_sample.reshape(bs * dseq, D_MODEL)
    pool0_p = jnp.zeros((1, bp, POOL_BUF, POOL_WIDTH), F32)
    gla0_p = jnp.zeros((1, bp, GLA_HEADS, GLA_DK, GLA_DV), F32)

    st_p = st_s = None
    for l in range(DEPTH):
        common = (ng, win, wina, poolw, pscale, sgug)
        tail = (wa2, ba, glag, wout)
        xp, st_p = _mix_call(l, xp, pool0_p, gla0_p, common + (sguw_cat, sgub_tile) + tail, st_p,
                             nseq=bp, L=seq, NS=1, T=PROMPT_TILE, pos0=0, state_layer=0, emit_vn=False)
        xs, st_s = _mix_call(l, xs, state_pool, state_gla, common + (sguw_dec, sgub_dec) + tail, st_s,
                             nseq=bs, L=dseq, NS=SAMPLE_SEQS, T=dseq, pos0=PAST_LEN, state_layer=l, emit_vn=True)
        xp, xs = _ffn_call(l, xp, xs, (fng, wg, wu, wd), fin, TM=FFN_ROWS, final=l == DEPTH - 1)

    return (xp.reshape(bp, seq, D_MODEL), xs.reshape(bs, dseq, D_MODEL),
            st_p[0], st_p[1], st_s[0], st_s[1], st_s[2].reshape(DEPTH, bs, dseq, SGU_WIDTH))
```

```python
import functools

import jax
import jax.numpy as jnp
from jax import lax
from jax.experimental import pallas as pl
from jax.experimental.pallas import tpu as pltpu

F32 = jnp.float32
BF16 = jnp.bfloat16

D_MODEL = 1024
DEPTH = 4
POOL_WIDTH = 256
POOL_WINDOWS = (2, 4, 8, 16)
POOL_GROUP_DIM = 64
POOL_BUF = 15
POOL_PAD = 16
SGU_WIDTH = 256
SGU_HEADS = 4
SGU_HEAD_DIM = 64
SGU_CHUNK = 128
GLA_HEADS = 4
GLA_DK = 64
GLA_DV = 128
GLA_KW = GLA_HEADS * GLA_DK
GLA_VW = GLA_HEADS * GLA_DV
GLA_RANK = 16
GLA_RANK_PAD = 128
GLA_GATE_NORM = 16.0
GLA_CHUNK = 64
MXU_COLS = 256
PART_ROWS = 256
SEQ_UNROLL = 8
D_FF = 2816
EPS = 1e-6

OFF_P, OFF_U, OFF_V, OFF_Q, OFF_K, OFF_VG, OFF_G, OFF_A = 0, 256, 512, 768, 1024, 1280, 1792, 2304
IN_MAIN = 2304
FFN_CHUNKS = ((0, 1536), (1536, 2816))

PAST_LEN = 16384
PROMPT_TILE = 1024
SAMPLE_SEQS = 32
FFN_ROWS = 512
VMEM_LIMIT = 56 * 1024 * 1024


def _rms(x, g):
    return x * lax.rsqrt(jnp.mean(x * x, axis=-1, keepdims=True) + EPS) * g


def _dot(a, b):
    return jnp.dot(a, b, preferred_element_type=F32)


def _dot_nt(a, b):
    return lax.dot_general(a, b, (((1,), (1,)), ((), ())), preferred_element_type=F32)


def _dot_tn(a, b):
    return lax.dot_general(a, b, (((0,), (0,)), ((), ())), preferred_element_type=F32)


def _split_bf16(x):
    hi = x.astype(BF16)
    lo = (x - hi.astype(F32)).astype(BF16)
    return hi, lo


def _mix_kernel(*refs, NS, T, pos0, emit_vn, n_prev):
    (x_ref, pool0_ref, gla0_ref, ng_ref, win_ref, wina_ref, poolw_ref, pscale_ref,
     sgug_ref, sguw_ref, sgub_ref, wa2_ref, ba_ref, glag_ref, wout_ref) = refs[:15]
    refs = refs[15 + n_prev:]
    xo_ref, pool_o_ref, gla_o_ref = refs[:3]
    vn_o_ref = refs[3] if emit_vn else None
    z_s, pz_s, qe_s, ke_s, kd_s, ecol_s, o_s = refs[-7:]
    j = pl.program_id(1)
    R = NS * T
    P = PART_ROWS
    TP = P // NS
    C = min(GLA_CHUNK, T)
    units = P // C
    assert R % P == 0 and (NS == 1 or R == P)

    @pl.when(j == 0)
    def _():
        pz_s[:, 0:1, :] = jnp.zeros((NS, 1, POOL_WIDTH), F32)
        pz_s[:, 1:POOL_PAD, :] = pool0_ref[...]
        gla_o_ref[...] = gla0_ref[...]

    grp = lax.broadcasted_iota(jnp.int32, (1, 1, POOL_WIDTH), 2) // POOL_GROUP_DIM
    win = jnp.where(grp == 0, 2, jnp.where(grp == 1, 4, jnp.where(grp == 2, 8, 16)))
    r_i = lax.broadcasted_iota(jnp.int32, (SGU_WIDTH, SGU_WIDTH), 0) // SGU_HEAD_DIM
    c_i = lax.broadcasted_iota(jnp.int32, (SGU_WIDTH, SGU_WIDTH), 1) // SGU_HEAD_DIM
    head_ones = (r_i == c_i).astype(BF16)
    lane_head = lax.broadcasted_iota(jnp.int32, (1, SGU_WIDTH), 1) // SGU_HEAD_DIM
    if T >= SGU_CHUNK:
        row = lax.broadcasted_iota(jnp.int32, (SGU_CHUNK, SGU_HEADS * SGU_CHUNK), 0)
        col = lax.broadcasted_iota(jnp.int32, (SGU_CHUNK, SGU_HEADS * SGU_CHUNK), 1)
        wcat = jnp.where((col & (SGU_CHUNK - 1)) <= row, sguw_ref[...], 0.0).astype(BF16)
    rr = lax.broadcasted_iota(jnp.int32, (P, P), 0)
    cc = lax.broadcasted_iota(jnp.int32, (P, P), 1)
    tri_bd = ((rr // C == cc // C) & (rr >= cc)).astype(BF16)
    sel2 = (lax.broadcasted_iota(jnp.int32, (8, 2 * GLA_DV), 0) // 2
            == lax.broadcasted_iota(jnp.int32, (8, 2 * GLA_DV), 1) // GLA_DV).astype(BF16)
    khead = lax.broadcasted_iota(jnp.int32, (1, GLA_KW), 1) // GLA_DK
    vhead = lax.broadcasted_iota(jnp.int32, (1, GLA_VW), 1) // GLA_DV
    arow = lax.broadcasted_iota(jnp.int32, (C, GLA_HEADS * C), 0)
    acol = lax.broadcasted_iota(jnp.int32, (C, GLA_HEADS * C), 1) & (C - 1)
    causal = acol <= arow
    zblk = jnp.zeros((GLA_DK, GLA_DV), F32)

    def gla_unit_stages(r0, e0, read_state, write_state):
        rows = pl.ds(r0, C)
        qe = qe_s[rows, :].astype(BF16)
        ke = ke_s[rows, :]
        kd = kd_s[rows, :]
        vv = z_s[rows, OFF_VG:OFF_VG + GLA_VW]
        ke_bd = jnp.concatenate(
            [jnp.where(khead == hh, ke, 0.0) for hh in range(GLA_HEADS)], axis=0).astype(BF16)
        att_raw = _dot_nt(qe, ke_bd)
        v_bd = jnp.concatenate(
            [jnp.where(vhead == hh, vv, 0.0) for hh in range(GLA_HEADS)], axis=0).astype(BF16)
        kd_st = jnp.concatenate(
            [kd[:, hh * GLA_DK:(hh + 1) * GLA_DK] for hh in range(GLA_HEADS)], axis=0).astype(BF16)
        upd = _dot_tn(kd_st, v_bd)
        yield
        att = jnp.where(causal, att_raw, 0.0).astype(BF16)
        o_intra = _dot(att, v_bd)
        yield
        s_blocks = read_state()
        s_bd = jnp.concatenate(
            [jnp.concatenate([s_blocks[hh] if hc == hh else zblk for hc in range(GLA_HEADS)], axis=1)
             for hh in range(GLA_HEADS)], axis=0)
        o_s[rows, :] = o_intra + _dot(qe, s_bd.astype(BF16))
        ecol = ecol_s[:, pl.ds(e0, GLA_DV)]
        write_state([ecol[hh * GLA_DK:(hh + 1) * GLA_DK] * s_blocks[hh] + upd[:, hh * GLA_DV:(hh + 1) * GLA_DV]
                     for hh in range(GLA_HEADS)])

    def step(gen):
        try:
            next(gen)
        except StopIteration:
            pass

    def part_stages(part):
        r0p = part * P
        prow = slice(r0p, r0p + P)
        t0 = part * TP
        x = x_ref[prow, :]
        h = _rms(x, ng_ref[...]).astype(BF16)
        yield
        for n in range(IN_MAIN // MXU_COLS):
            cs = slice(n * MXU_COLS, (n + 1) * MXU_COLS)
            z_s[prow, cs] = _dot(h, win_ref[:, cs])
            yield
        alow = _dot(h, wina_ref[...]).astype(BF16)

        p3 = z_s[prow, OFF_P:OFF_P + POOL_WIDTH].reshape(NS, TP, POOL_WIDTH)
        pz_s[:, POOL_PAD + t0:POOL_PAD + t0 + TP, :] = p3
        e = pz_s[:, t0:t0 + POOL_PAD + TP, :]
        w2 = e + pltpu.roll(e, 1, 1)
        w4 = w2 + pltpu.roll(w2, 2, 1)
        w8 = w4 + pltpu.roll(w4, 4, 1)
        w16 = w8 + pltpu.roll(w8, 8, 1)
        wsum = jnp.where(grp == 0, w2, jnp.where(grp == 1, w4, jnp.where(grp == 2, w8, w16)))
        wsum = wsum[:, POOL_PAD:POOL_PAD + TP, :]
        pos = pos0 + j * T + t0 + lax.broadcasted_iota(jnp.int32, (1, TP, 1), 1)
        cnt = jnp.minimum(pos + 1, win).astype(F32)
        d = (wsum / cnt - p3).reshape(P, POOL_WIDTH)
        a_out = _dot(d.astype(BF16), poolw_ref[...]) * pscale_ref[...]
        yield

        v = z_s[prow, OFF_V:OFF_V + SGU_WIDTH]
        u = z_s[prow, OFF_U:OFF_U + SGU_WIDTH]
        vsq_hi, vsq_lo = _split_bf16(v * v)
        ss = _dot(vsq_hi, head_ones) + _dot(vsq_lo, head_ones)
        vn = v * lax.rsqrt(ss * (1.0 / SGU_HEAD_DIM) + EPS) * sgug_ref[...]
        if emit_vn:
            vn_o_ref[prow, :] = vn
        yield
        if T >= SGU_CHUNK:
            parts = []
            for c in range(P // SGU_CHUNK):
                vc = vn[c * SGU_CHUNK:(c + 1) * SGU_CHUNK]
                stack = jnp.concatenate(
                    [jnp.where(lane_head == hh, vc, 0.0) for hh in range(SGU_HEADS)], axis=0)
                parts.append(_dot(wcat, stack.astype(BF16)) + sgub_ref[...])
            s_gate = jnp.concatenate(parts, axis=0)
        else:
            vn3 = vn.reshape(NS, T, SGU_WIDTH)
            srow = lax.broadcasted_iota(jnp.int32, (T, SGU_WIDTH), 0)
            s3 = jnp.zeros((NS, T, SGU_WIDTH), F32) + sgub_ref[...][None]
            for jj in range(T):
                coef = jnp.where(srow >= jj, sguw_ref[jj], 0.0)
                s3 = s3 + vn3[:, jj:jj + 1, :] * coef[None]
            s_gate = s3.reshape(P, SGU_WIDTH)
        b_out = u * s_gate
        yield

        xg = _dot(alow, wa2_ref[...]) + ba_ref[...]
        lg = (jnp.minimum(xg, 0.0) - jnp.log(1.0 + jnp.exp(-jnp.abs(xg)))) * (1.0 / GLA_GATE_NORM)
        lg_hi, lg_lo = _split_bf16(lg)
        bcum = _dot(tri_bd, lg_hi) + _dot(tri_bd, lg_lo)
        b3 = bcum.reshape(units, C, GLA_KW)
        btot = jnp.broadcast_to(b3[:, C - 1:C, :], (units, C, GLA_KW)).reshape(P, GLA_KW)
        ke_all = z_s[prow, OFF_K:OFF_K + GLA_KW] * jnp.exp(-bcum)
        qe_s[prow, :] = z_s[prow, OFF_Q:OFF_Q + GLA_KW] * (GLA_DK ** -0.5) * jnp.exp(bcum)
        ke_s[prow, :] = ke_all
        kd_s[prow, :] = ke_all * jnp.exp(btot)
        yield
        etot = jnp.exp(b3[:, C - 1, :])
        et_hi = etot.astype(BF16).astype(F32)
        et_lo = etot - et_hi
        zpad = jnp.zeros((4, GLA_KW), F32)
        for pp in range(units // 2):
            a_rows = jnp.concatenate([et_hi[2 * pp:2 * pp + 1], et_lo[2 * pp:2 * pp + 1],
                                      et_hi[2 * pp + 1:2 * pp + 2], et_lo[2 * pp + 1:2 * pp + 2], zpad], axis=0)
            e0 = (part * units + 2 * pp) * GLA_DV
            ecol_s[:, e0:e0 + 2 * GLA_DV] = _dot_tn(a_rows.astype(BF16), sel2)
        yield

        if NS == 1:
            state = [[gla_o_ref[0, hh] for hh in range(GLA_HEADS)]]
            ug = [gla_unit_stages(r0p + c * C, (part * units + c) * GLA_DV,
                                  lambda: state[0], lambda new: state.__setitem__(0, new)) for c in range(units)]
            step(ug[0])
            step(ug[0])
            for c in range(units):
                if c + 1 < units:
                    step(ug[c + 1])
                step(ug[c])
                if c + 1 < units:
                    step(ug[c + 1])
                yield
            for hh in range(GLA_HEADS):
                gla_o_ref[0, hh] = state[0][hh]
        else:
            assert units == NS and NS % SEQ_UNROLL == 0

            def seq_group(g, carry):
                def unit(sq):
                    def write(new):
                        for hh in range(GLA_HEADS):
                            gla_o_ref[sq, hh] = new[hh]
                    return gla_unit_stages(pl.multiple_of(sq * C, C), pl.multiple_of(sq * GLA_DV, GLA_DV),
                                           lambda: [gla_o_ref[sq, hh] for hh in range(GLA_HEADS)], write)
                ug = [unit(g * SEQ_UNROLL + i) for i in range(SEQ_UNROLL)]
                for _ in range(3):
                    for u_gen in ug:
                        step(u_gen)
                return carry

            lax.fori_loop(0, NS // SEQ_UNROLL, seq_group, 0)

        gate = z_s[prow, OFF_G:OFF_G + GLA_VW]
        o_all = o_s[prow, :]
        c_parts = []
        for hh in range(GLA_HEADS):
            sl = slice(hh * GLA_DV, (hh + 1) * GLA_DV)
            gh = gate[:, sl]
            c_parts.append(_rms(o_all[:, sl], glag_ref[...]) * (gh * jax.nn.sigmoid(gh)))

        mix = jnp.concatenate([a_out, b_out] + c_parts, axis=-1).astype(BF16)
        yield
        for n in range(D_MODEL // MXU_COLS):
            cs = slice(n * MXU_COLS, (n + 1) * MXU_COLS)
            xo_ref[prow, cs] = x[:, cs] + _dot(mix, wout_ref[:, cs])
            yield


    n_proj = 1 + IN_MAIN // MXU_COLS
    gens = [part_stages(p) for p in range(R // P)]
    done = [False] * len(gens)

    def advance(p, n=1):
        for _ in range(n):
            if not done[p]:
                try:
                    next(gens[p])
                except StopIteration:
                    done[p] = True

    n_mix = 6 + units if NS == 1 else 7
    advance(0, n_proj)
    for p in range(len(gens)):
        for _ in range(n_mix):
            advance(p)
            if p + 1 < len(gens):
                advance(p + 1)
            if p > 0:
                advance(p - 1)
    for p in range(len(gens)):
        while not done[p]:
            advance(p)

    pool_o_ref[...] = pz_s[:, T + 1:T + POOL_PAD, :]
    if T >= POOL_PAD:
        pz_s[:, 0:POOL_PAD, :] = pz_s[:, T:T + POOL_PAD, :]


def _ffn_kernel(xp_ref, xs_ref, g_ref, wg_ref, wu_ref, wd_ref, fg_ref, op_ref, os_ref, *, n_prompt_steps, final):
    def part_stages(x_ref, o_ref, rows):
        x = x_ref[rows, :]
        hf = _rms(x, g_ref[...]).astype(BF16)
        yield
        acc = x
        for lo, hi in FFN_CHUNKS:
            gt = _dot(hf, wg_ref[:, lo:hi])
            yield
            up = _dot(hf, wu_ref[:, lo:hi])
            yield
            act = (gt * jax.nn.sigmoid(gt) * up).astype(BF16)
            acc = acc + _dot(act, wd_ref[lo:hi, :])
            yield
        if final:
            acc = _rms(acc, fg_ref[...])
        o_ref[rows, :] = acc

    def ffn(x_ref, o_ref):
        n_rows = x_ref.shape[0]
        gens = [part_stages(x_ref, o_ref, slice(r, r + PART_ROWS)) for r in range(0, n_rows, PART_ROWS)]
        live = list(gens)
        lag = 0
        while live:
            for g in list(live[:lag + 1]):
                try:
                    next(g)
                except StopIteration:
                    live.remove(g)
            lag += 1

    i = pl.program_id(0)
    pl.when(i < n_prompt_steps)(lambda: ffn(xp_ref, op_ref))
    pl.when(i >= n_prompt_steps)(lambda: ffn(xs_ref, os_ref))


def _wspec(shape, layer):
    nd = len(shape)
    return pl.BlockSpec((None,) + tuple(shape), lambda *g, _l=layer, _n=nd: (_l,) + (0,) * _n,
                        pipeline_mode=pl.Buffered(1))


def _mix_call(layer, x2, pool0, gla0, wts, prev, *, nseq, L, NS, T, pos0, state_layer, emit_vn):
    R = NS * T
    n_chunk = L // T
    grid = (nseq // NS, n_chunk)
    (ng, win, wina, poolw, pscale, sgug, sguw, sgub, wa2, ba, glag, wout) = wts
    sl = state_layer
    in_specs = [
        pl.BlockSpec((R, D_MODEL), lambda i, j: (i * n_chunk + j, 0)),
        pl.BlockSpec((None, NS, POOL_BUF, POOL_WIDTH), lambda i, j: (sl, i, 0, 0)),
        pl.BlockSpec((None, NS, GLA_HEADS, GLA_DK, GLA_DV), lambda i, j: (sl, i, 0, 0, 0)),
        _wspec((1, D_MODEL), layer),
        _wspec((D_MODEL, IN_MAIN), layer),
        _wspec((D_MODEL, GLA_RANK_PAD), layer),
        _wspec((POOL_WIDTH, POOL_WIDTH), layer),
        _wspec((1, POOL_WIDTH), layer),
        _wspec((1, SGU_WIDTH), layer),
        _wspec(sguw.shape[1:], layer),
        _wspec(sgub.shape[1:], layer),
        _wspec((GLA_RANK_PAD, GLA_KW), layer),
        _wspec((1, GLA_KW), layer),
        _wspec((1, GLA_DV), layer),
        _wspec((D_MODEL, D_MODEL), layer),
    ]
    operands = [x2, pool0, gla0, ng, win, wina, poolw, pscale, sgug, sguw, sgub, wa2, ba, glag, wout]
    out_specs = [
        pl.BlockSpec((R, D_MODEL), lambda i, j: (i * n_chunk + j, 0)),
        pl.BlockSpec((None, NS, POOL_BUF, POOL_WIDTH), lambda i, j: (layer, i, 0, 0)),
        pl.BlockSpec((None, NS, GLA_HEADS, GLA_DK, GLA_DV), lambda i, j: (layer, i, 0, 0, 0)),
    ]
    out_shape = [
        jax.ShapeDtypeStruct((nseq * L, D_MODEL), F32),
        jax.ShapeDtypeStruct((DEPTH, nseq, POOL_BUF, POOL_WIDTH), F32),
        jax.ShapeDtypeStruct((DEPTH, nseq, GLA_HEADS, GLA_DK, GLA_DV), F32),
    ]
    if emit_vn:
        out_specs.append(pl.BlockSpec((None, R, SGU_WIDTH), lambda i, j: (layer, i * n_chunk + j, 0)))
        out_shape.append(jax.ShapeDtypeStruct((DEPTH, nseq * L, SGU_WIDTH), F32))
    assert len(prev) == len(out_shape) - 1
    aliases = {}
    for k, arr in enumerate(prev):
        aliases[len(operands)] = 1 + k
        operands.append(arr)
        in_specs.append(pl.BlockSpec(memory_space=pl.ANY))
    scratch = [
        pltpu.VMEM((R, IN_MAIN), F32),
        pltpu.VMEM((NS, POOL_PAD + T, POOL_WIDTH), F32),
        pltpu.VMEM((R, GLA_KW), F32),
        pltpu.VMEM((R, GLA_KW), F32),
        pltpu.VMEM((R, GLA_KW), F32),
        pltpu.VMEM((GLA_KW, (R // min(GLA_CHUNK, T)) * GLA_DV), F32),
        pltpu.VMEM((R, GLA_VW), F32),
    ]
    outs = pl.pallas_call(
        functools.partial(_mix_kernel, NS=NS, T=T, pos0=pos0, emit_vn=emit_vn, n_prev=len(prev)),
        grid=grid, in_specs=in_specs, out_specs=out_specs, out_shape=out_shape,
        scratch_shapes=scratch, input_output_aliases=aliases,
        compiler_params=pltpu.CompilerParams(
            dimension_semantics=("arbitrary", "arbitrary"), vmem_limit_bytes=VMEM_LIMIT),
        name=f"mix_T{T}",
    )(*operands)
    return outs[0], tuple(outs[1:])


def _ffn_call(layer, xp, xs, wts, final_g, *, TM, final):
    n_p, n_s = xp.shape[0] // TM, xs.shape[0] // TM
    g, wg, wu, wd = wts
    p_idx = lambda i: (jnp.minimum(i, n_p - 1), 0)
    s_idx = lambda i: (jnp.maximum(i - n_p, 0), 0)
    return pl.pallas_call(
        functools.partial(_ffn_kernel, n_prompt_steps=n_p, final=final),
        grid=(n_p + n_s,),
        in_specs=[
            pl.BlockSpec((TM, D_MODEL), p_idx),
            pl.BlockSpec((TM, D_MODEL), s_idx),
            _wspec((1, D_MODEL), layer),
            _wspec((D_MODEL, D_FF), layer),
            _wspec((D_MODEL, D_FF), layer),
            _wspec((D_FF, D_MODEL), layer),
            pl.BlockSpec((1, D_MODEL), lambda i: (0, 0)),
        ],
        out_specs=[pl.BlockSpec((TM, D_MODEL), p_idx), pl.BlockSpec((TM, D_MODEL), s_idx)],
        out_shape=[jax.ShapeDtypeStruct(xp.shape, F32), jax.ShapeDtypeStruct(xs.shape, F32)],
        compiler_params=pltpu.CompilerParams(
            dimension_semantics=("arbitrary",), vmem_limit_bytes=VMEM_LIMIT),
        name="ffn",
    )(xp, xs, g, wg, wu, wd, final_g)


def kernel(x_prompt, x_sample, state_pool, state_gla, attn_norm_g, w_in, pool_w, pool_scale, sgu_norm_g, sgu_ws, sgu_b, gla_wa2, gla_ba, gla_norm_g, w_out, ffn_norm_g, w_gate, w_up, w_down, final_norm_g):
    bp, seq, _ = x_prompt.shape
    bs, dseq, _ = x_sample.shape
    assert seq % PROMPT_TILE == 0 and dseq == 8 and bs % SAMPLE_SEQS == 0

    ng = attn_norm_g[:, None, :]
    win = w_in.astype(BF16)
    wina = jnp.pad(w_in[:, :, IN_MAIN:], ((0, 0), (0, 0), (0, GLA_RANK_PAD - GLA_RANK))).astype(BF16)
    eye_g = jnp.eye(len(POOL_WINDOWS), dtype=F32)
    poolw = jnp.einsum('lgcd,gh->lgchd', pool_w, eye_g).reshape(DEPTH, POOL_WIDTH, POOL_WIDTH).astype(BF16)
    pscale = pool_scale[:, None, :]
    sgug = sgu_norm_g.reshape(DEPTH, 1, SGU_WIDTH)
    sguw_cat = sgu_ws.transpose(0, 2, 1, 3).reshape(DEPTH, SGU_CHUNK, SGU_HEADS * SGU_CHUNK)
    sgub_tile = jnp.repeat(sgu_b.transpose(0, 2, 1), SGU_HEAD_DIM, axis=-1)
    sguw_dec = jnp.repeat(sgu_ws[:, :, :dseq, :dseq].transpose(0, 3, 2, 1), SGU_HEAD_DIM, axis=-1)
    sgub_dec = sgub_tile[:, :dseq]
    wa2 = jnp.pad(gla_wa2, ((0, 0), (0, GLA_RANK_PAD - GLA_RANK), (0, 0))).astype(BF16)
    ba = gla_ba[:, None, :]
    glag = gla_norm_g[:, None, :]
    wout = w_out.astype(BF16)
    fng = ffn_norm_g[:, None, :]
    wg = w_gate.astype(BF16)
    wu = w_up.astype(BF16)
    wd = w_down.astype(BF16)
    fin = final_norm_g[None, :]

    xp = x_prompt.reshape(bp * seq, D_MODEL)
    xs = x_sample.reshape(bs * dseq, D_MODEL)
    pool0_p = jnp.zeros((1, bp, POOL_BUF, POOL_WIDTH), F32)
    gla0_p = jnp.zeros((1, bp, GLA_HEADS, GLA_DK, GLA_DV), F32)

    st_p = (jnp.zeros((DEPTH, bp, POOL_BUF, POOL_WIDTH), F32), jnp.zeros((DEPTH, bp, GLA_HEADS, GLA_DK, GLA_DV), F32))
    st_s = (jnp.zeros((DEPTH, bs, POOL_BUF, POOL_WIDTH), F32), jnp.zeros((DEPTH, bs, GLA_HEADS, GLA_DK, GLA_DV), F32),
            jnp.zeros((DEPTH, bs * dseq, SGU_WIDTH), F32))
    for l in range(DEPTH):
        common = (ng, win, wina, poolw, pscale, sgug)
        tail = (wa2, ba, glag, wout)
        xp, st_p = _mix_call(l, xp, pool0_p, gla0_p, common + (sguw_cat, sgub_tile) + tail, st_p,
                             nseq=bp, L=seq, NS=1, T=PROMPT_TILE, pos0=0, state_layer=0, emit_vn=False)
        xs, st_s = _mix_call(l, xs, state_pool, state_gla, common + (sguw_dec, sgub_dec) + tail, st_s,
                             nseq=bs, L=dseq, NS=SAMPLE_SEQS, T=dseq, pos0=PAST_LEN, state_layer=l, emit_vn=True)
        xp, xs = _ffn_call(l, xp, xs, (fng, wg, wu, wd), fin, TM=FFN_ROWS, final=l == DEPTH - 1)

    return (xp.reshape(bp, seq, D_MODEL), xs.reshape(bs, dseq, D_MODEL),
            st_p[0], st_p[1], st_s[0], st_s[1], st_s[2].reshape(DEPTH, bs, dseq, SGU_WIDTH))
```

```python
import functools

import jax
import jax.numpy as jnp
from jax import lax
from jax.experimental import pallas as pl
from jax.experimental.pallas import tpu as pltpu

F32 = jnp.float32
BF16 = jnp.bfloat16

D_MODEL = 1024
DEPTH = 4
POOL_WIDTH = 256
POOL_WINDOWS = (2, 4, 8, 16)
POOL_GROUP_DIM = 64
POOL_BUF = 15
POOL_PAD = 16
SGU_WIDTH = 256
SGU_HEADS = 4
SGU_HEAD_DIM = 64
SGU_CHUNK = 128
GLA_HEADS = 4
GLA_DK = 64
GLA_DV = 128
GLA_KW = GLA_HEADS * GLA_DK
GLA_VW = GLA_HEADS * GLA_DV
GLA_RANK = 16
GLA_RANK_PAD = 128
GLA_GATE_NORM = 16.0
GLA_CHUNK = 64
MXU_COLS = 256
PART_ROWS = 256
SEQ_UNROLL = 8
D_FF = 2816
EPS = 1e-6

OFF_P, OFF_U, OFF_V, OFF_Q, OFF_K, OFF_VG, OFF_G, OFF_A = 0, 256, 512, 768, 1024, 1280, 1792, 2304
IN_MAIN = 2304
FFN_CHUNKS = ((0, 1536), (1536, 2816))

PAST_LEN = 16384
PROMPT_TILE = 512
SAMPLE_SEQS = 32
FFN_ROWS = 512
VMEM_LIMIT = 56 * 1024 * 1024


def _rms(x, g):
    return x * lax.rsqrt(jnp.mean(x * x, axis=-1, keepdims=True) + EPS) * g


def _dot(a, b):
    return jnp.dot(a, b, preferred_element_type=F32)


def _dot_nt(a, b):
    return lax.dot_general(a, b, (((1,), (1,)), ((), ())), preferred_element_type=F32)


def _dot_tn(a, b):
    return lax.dot_general(a, b, (((0,), (0,)), ((), ())), preferred_element_type=F32)


def _split_bf16(x):
    hi = x.astype(BF16)
    lo = (x - hi.astype(F32)).astype(BF16)
    return hi, lo


def _mix_kernel(*refs, NS, T, pos0, emit_vn, n_prev):
    (x_ref, pool0_ref, gla0_ref, ng_ref, win_ref, wina_ref, poolw_ref, pscale_ref,
     sgug_ref, sguw_ref, sgub_ref, wa2_ref, ba_ref, glag_ref, wout_ref) = refs[:15]
    refs = refs[15 + n_prev:]
    xo_ref, pool_o_ref, gla_o_ref = refs[:3]
    vn_o_ref = refs[3] if emit_vn else None
    z_s, pz_s, qe_s, ke_s, kd_s, ecol_s, o_s = refs[-7:]
    j = pl.program_id(1)
    R = NS * T
    P = PART_ROWS
    TP = P // NS
    C = min(GLA_CHUNK, T)
    units = P // C
    assert R % P == 0 and (NS == 1 or R == P)

    @pl.when(j == 0)
    def _():
        pz_s[:, 0:1, :] = jnp.zeros((NS, 1, POOL_WIDTH), F32)
        pz_s[:, 1:POOL_PAD, :] = pool0_ref[...]
        gla_o_ref[...] = gla0_ref[...]

    grp = lax.broadcasted_iota(jnp.int32, (1, 1, POOL_WIDTH), 2) // POOL_GROUP_DIM
    win = jnp.where(grp == 0, 2, jnp.where(grp == 1, 4, jnp.where(grp == 2, 8, 16)))
    r_i = lax.broadcasted_iota(jnp.int32, (SGU_WIDTH, SGU_WIDTH), 0) // SGU_HEAD_DIM
    c_i = lax.broadcasted_iota(jnp.int32, (SGU_WIDTH, SGU_WIDTH), 1) // SGU_HEAD_DIM
    head_ones = (r_i == c_i).astype(BF16)
    lane_head = lax.broadcasted_iota(jnp.int32, (1, SGU_WIDTH), 1) // SGU_HEAD_DIM
    if T >= SGU_CHUNK:
        row = lax.broadcasted_iota(jnp.int32, (SGU_CHUNK, SGU_HEADS * SGU_CHUNK), 0)
        col = lax.broadcasted_iota(jnp.int32, (SGU_CHUNK, SGU_HEADS * SGU_CHUNK), 1)
        wcat = jnp.where((col & (SGU_CHUNK - 1)) <= row, sguw_ref[...], 0.0).astype(BF16)
    rr = lax.broadcasted_iota(jnp.int32, (P, P), 0)
    cc = lax.broadcasted_iota(jnp.int32, (P, P), 1)
    tri_bd = ((rr // C == cc // C) & (rr >= cc)).astype(BF16)
    sel2 = (lax.broadcasted_iota(jnp.int32, (8, 2 * GLA_DV), 0) // 2
            == lax.broadcasted_iota(jnp.int32, (8, 2 * GLA_DV), 1) // GLA_DV).astype(BF16)
    khead = lax.broadcasted_iota(jnp.int32, (1, GLA_KW), 1) // GLA_DK
    vhead = lax.broadcasted_iota(jnp.int32, (1, GLA_VW), 1) // GLA_DV
    arow = lax.broadcasted_iota(jnp.int32, (C, GLA_HEADS * C), 0)
    acol = lax.broadcasted_iota(jnp.int32, (C, GLA_HEADS * C), 1) & (C - 1)
    causal = acol <= arow
    zblk = jnp.zeros((GLA_DK, GLA_DV), F32)

    def gla_unit_stages(r0, e0, read_state, write_state):
        rows = pl.ds(r0, C)
        qe = qe_s[rows, :].astype(BF16)
        ke = ke_s[rows, :]
        kd = kd_s[rows, :]
        vv = z_s[rows, OFF_VG:OFF_VG + GLA_VW]
        ke_bd = jnp.concatenate(
            [jnp.where(khead == hh, ke, 0.0) for hh in range(GLA_HEADS)], axis=0).astype(BF16)
        att_raw = _dot_nt(qe, ke_bd)
        v_bd = jnp.concatenate(
            [jnp.where(vhead == hh, vv, 0.0) for hh in range(GLA_HEADS)], axis=0).astype(BF16)
        kd_st = jnp.concatenate(
            [kd[:, hh * GLA_DK:(hh + 1) * GLA_DK] for hh in range(GLA_HEADS)], axis=0).astype(BF16)
        upd = _dot_tn(kd_st, v_bd)
        yield
        att = jnp.where(causal, att_raw, 0.0).astype(BF16)
        o_intra = _dot(att, v_bd)
        yield
        s_blocks = read_state()
        s_bd = jnp.concatenate(
            [jnp.concatenate([s_blocks[hh] if hc == hh else zblk for hc in range(GLA_HEADS)], axis=1)
             for hh in range(GLA_HEADS)], axis=0)
        o_s[rows, :] = o_intra + _dot(qe, s_bd.astype(BF16))
        ecol = ecol_s[:, pl.ds(e0, GLA_DV)]
        write_state([ecol[hh * GLA_DK:(hh + 1) * GLA_DK] * s_blocks[hh] + upd[:, hh * GLA_DV:(hh + 1) * GLA_DV]
                     for hh in range(GLA_HEADS)])

    def step(gen):
        try:
            next(gen)
        except StopIteration:
            pass

    def part_stages(part):
        r0p = part * P
        prow = slice(r0p, r0p + P)
        t0 = part * TP
        x = x_ref[prow, :]
        h = _rms(x, ng_ref[...]).astype(BF16)
        yield
        for n in range(IN_MAIN // MXU_COLS):
            cs = slice(n * MXU_COLS, (n + 1) * MXU_COLS)
            z_s[prow, cs] = _dot(h, win_ref[:, cs])
            yield
        alow = _dot(h, wina_ref[...]).astype(BF16)

        p3 = z_s[prow, OFF_P:OFF_P + POOL_WIDTH].reshape(NS, TP, POOL_WIDTH)
        pz_s[:, POOL_PAD + t0:POOL_PAD + t0 + TP, :] = p3
        e = pz_s[:, t0:t0 + POOL_PAD + TP, :]
        w2 = e + pltpu.roll(e, 1, 1)
        w4 = w2 + pltpu.roll(w2, 2, 1)
        w8 = w4 + pltpu.roll(w4, 4, 1)
        w16 = w8 + pltpu.roll(w8, 8, 1)
        wsum = jnp.where(grp == 0, w2, jnp.where(grp == 1, w4, jnp.where(grp == 2, w8, w16)))
        wsum = wsum[:, POOL_PAD:POOL_PAD + TP, :]
        pos = pos0 + j * T + t0 + lax.broadcasted_iota(jnp.int32, (1, TP, 1), 1)
        cnt = jnp.minimum(pos + 1, win).astype(F32)
        d = (wsum / cnt - p3).reshape(P, POOL_WIDTH)
        a_out = _dot(d.astype(BF16), poolw_ref[...]) * pscale_ref[...]
        yield

        v = z_s[prow, OFF_V:OFF_V + SGU_WIDTH]
        u = z_s[prow, OFF_U:OFF_U + SGU_WIDTH]
        vsq_hi, vsq_lo = _split_bf16(v * v)
        ss = _dot(vsq_hi, head_ones) + _dot(vsq_lo, head_ones)
        vn = v * lax.rsqrt(ss * (1.0 / SGU_HEAD_DIM) + EPS) * sgug_ref[...]
        if emit_vn:
            vn_o_ref[prow, :] = vn
        yield
        if T >= SGU_CHUNK:
            parts = []
            for c in range(P // SGU_CHUNK):
                vc = vn[c * SGU_CHUNK:(c + 1) * SGU_CHUNK]
                stack = jnp.concatenate(
                    [jnp.where(lane_head == hh, vc, 0.0) for hh in range(SGU_HEADS)], axis=0)
                parts.append(_dot(wcat, stack.astype(BF16)) + sgub_ref[...])
            s_gate = jnp.concatenate(parts, axis=0)
        else:
            vn3 = vn.reshape(NS, T, SGU_WIDTH)
            srow = lax.broadcasted_iota(jnp.int32, (T, SGU_WIDTH), 0)
            s3 = jnp.zeros((NS, T, SGU_WIDTH), F32) + sgub_ref[...][None]
            for jj in range(T):
                coef = jnp.where(srow >= jj, sguw_ref[jj], 0.0)
                s3 = s3 + vn3[:, jj:jj + 1, :] * coef[None]
            s_gate = s3.reshape(P, SGU_WIDTH)
        b_out = u * s_gate
        yield

        xg = _dot(alow, wa2_ref[...]) + ba_ref[...]
        lg = (jnp.minimum(xg, 0.0) - jnp.log(1.0 + jnp.exp(-jnp.abs(xg)))) * (1.0 / GLA_GATE_NORM)
        lg_hi, lg_lo = _split_bf16(lg)
        bcum = _dot(tri_bd, lg_hi) + _dot(tri_bd, lg_lo)
        b3 = bcum.reshape(units, C, GLA_KW)
        btot = jnp.broadcast_to(b3[:, C - 1:C, :], (units, C, GLA_KW)).reshape(P, GLA_KW)
        ke_all = z_s[prow, OFF_K:OFF_K + GLA_KW] * jnp.exp(-bcum)
        qe_s[prow, :] = z_s[prow, OFF_Q:OFF_Q + GLA_KW] * (GLA_DK ** -0.5) * jnp.exp(bcum)
        ke_s[prow, :] = ke_all
        kd_s[prow, :] = ke_all * jnp.exp(btot)
        yield
        etot = jnp.exp(b3[:, C - 1, :])
        et_hi = etot.astype(BF16).astype(F32)
        et_lo = etot - et_hi
        zpad = jnp.zeros((4, GLA_KW), F32)
        for pp in range(units // 2):
            a_rows = jnp.concatenate([et_hi[2 * pp:2 * pp + 1], et_lo[2 * pp:2 * pp + 1],
                                      et_hi[2 * pp + 1:2 * pp + 2], et_lo[2 * pp + 1:2 * pp + 2], zpad], axis=0)
            e0 = (part * units + 2 * pp) * GLA_DV
            ecol_s[:, e0:e0 + 2 * GLA_DV] = _dot_tn(a_rows.astype(BF16), sel2)
        yield

        if NS == 1:
            state = [[gla_o_ref[0, hh] for hh in range(GLA_HEADS)]]
            ug = [gla_unit_stages(r0p + c * C, (part * units + c) * GLA_DV,
                                  lambda: state[0], lambda new: state.__setitem__(0, new)) for c in range(units)]
            step(ug[0])
            step(ug[0])
            for c in range(units):
                if c + 1 < units:
                    step(ug[c + 1])
                step(ug[c])
                if c + 1 < units:
                    step(ug[c + 1])
                yield
            for hh in range(GLA_HEADS):
                gla_o_ref[0, hh] = state[0][hh]
        else:
            assert units == NS and NS % SEQ_UNROLL == 0

            def seq_group(g, carry):
                def unit(sq):
                    def write(new):
                        for hh in range(GLA_HEADS):
                            gla_o_ref[sq, hh] = new[hh]
                    return gla_unit_stages(pl.multiple_of(sq * C, C), pl.multiple_of(sq * GLA_DV, GLA_DV),
                                           lambda: [gla_o_ref[sq, hh] for hh in range(GLA_HEADS)], write)
                ug = [unit(g * SEQ_UNROLL + i) for i in range(SEQ_UNROLL)]
                for _ in range(3):
                    for u_gen in ug:
                        step(u_gen)
                return carry

            lax.fori_loop(0, NS // SEQ_UNROLL, seq_group, 0)

        gate = z_s[prow, OFF_G:OFF_G + GLA_VW]
        o_all = o_s[prow, :]
        c_parts = []
        for hh in range(GLA_HEADS):
            sl = slice(hh * GLA_DV, (hh + 1) * GLA_DV)
            gh = gate[:, sl]
            c_parts.append(_rms(o_all[:, sl], glag_ref[...]) * (gh * jax.nn.sigmoid(gh)))

        mix = jnp.concatenate([a_out, b_out] + c_parts, axis=-1).astype(BF16)
        yield
        for n in range(D_MODEL // MXU_COLS):
            cs = slice(n * MXU_COLS, (n + 1) * MXU_COLS)
            xo_ref[prow, cs] = x[:, cs] + _dot(mix, wout_ref[:, cs])
            yield


    n_proj = 1 + IN_MAIN // MXU_COLS
    gens = [part_stages(p) for p in range(R // P)]
    done = [False] * len(gens)

    def advance(p, n=1):
        for _ in range(n):
            if not done[p]:
                try:
                    next(gens[p])
                except StopIteration:
                    done[p] = True

    n_mix = 6 + units if NS == 1 else 7
    advance(0, n_proj)
    for p in range(len(gens)):
        for _ in range(n_mix):
            advance(p)
            if p + 1 < len(gens):
                advance(p + 1)
            if p > 0:
                advance(p - 1)
    for p in range(len(gens)):
        while not done[p]:
            advance(p)

    pool_o_ref[...] = pz_s[:, T + 1:T + POOL_PAD, :]
    if T >= POOL_PAD:
        pz_s[:, 0:POOL_PAD, :] = pz_s[:, T:T + POOL_PAD, :]


def _ffn_kernel(xp_ref, xs_ref, g_ref, wg_ref, wu_ref, wd_ref, fg_ref, op_ref, os_ref, *, n_prompt_steps, final):
    def part_stages(x_ref, o_ref, rows):
        x = x_ref[rows, :]
        hf = _rms(x, g_ref[...]).astype(BF16)
        yield
        acc = x
        for lo, hi in FFN_CHUNKS:
            gt = _dot(hf, wg_ref[:, lo:hi])
            yield
            up = _dot(hf, wu_ref[:, lo:hi])
            yield
            act = (gt * jax.nn.sigmoid(gt) * up).astype(BF16)
            acc = acc + _dot(act, wd_ref[lo:hi, :])
            yield
        if final:
            acc = _rms(acc, fg_ref[...])
        o_ref[rows, :] = acc

    def ffn(x_ref, o_ref):
        n_rows = x_ref.shape[0]
        gens = [part_stages(x_ref, o_ref, slice(r, r + PART_ROWS)) for r in range(0, n_rows, PART_ROWS)]
        live = list(gens)
        lag = 0
        while live:
            for g in list(live[:lag + 1]):
                try:
                    next(g)
                except StopIteration:
                    live.remove(g)
            lag += 1

    i = pl.program_id(0)
    pl.when(i < n_prompt_steps)(lambda: ffn(xp_ref, op_ref))
    pl.when(i >= n_prompt_steps)(lambda: ffn(xs_ref, os_ref))


def _wspec(shape, layer):
    nd = len(shape)
    return pl.BlockSpec((None,) + tuple(shape), lambda *g, _l=layer, _n=nd: (_l,) + (0,) * _n,
                        pipeline_mode=pl.Buffered(1))


def _mix_call(layer, x2, pool0, gla0, wts, prev, *, nseq, L, NS, T, pos0, state_layer, emit_vn):
    R = NS * T
    n_chunk = L // T
    grid = (nseq // NS, n_chunk)
    (ng, win, wina, poolw, pscale, sgug, sguw, sgub, wa2, ba, glag, wout) = wts
    sl = state_layer
    in_specs = [
        pl.BlockSpec((R, D_MODEL), lambda i, j: (i * n_chunk + j, 0)),
        pl.BlockSpec((None, NS, POOL_BUF, POOL_WIDTH), lambda i, j: (sl, i, 0, 0)),
        pl.BlockSpec((None, NS, GLA_HEADS, GLA_DK, GLA_DV), lambda i, j: (sl, i, 0, 0, 0)),
        _wspec((1, D_MODEL), layer),
        _wspec((D_MODEL, IN_MAIN), layer),
        _wspec((D_MODEL, GLA_RANK_PAD), layer),
        _wspec((POOL_WIDTH, POOL_WIDTH), layer),
        _wspec((1, POOL_WIDTH), layer),
        _wspec((1, SGU_WIDTH), layer),
        _wspec(sguw.shape[1:], layer),
        _wspec(sgub.shape[1:], layer),
        _wspec((GLA_RANK_PAD, GLA_KW), layer),
        _wspec((1, GLA_KW), layer),
        _wspec((1, GLA_DV), layer),
        _wspec((D_MODEL, D_MODEL), layer),
    ]
    operands = [x2, pool0, gla0, ng, win, wina, poolw, pscale, sgug, sguw, sgub, wa2, ba, glag, wout]
    out_specs = [
        pl.BlockSpec((R, D_MODEL), lambda i, j: (i * n_chunk + j, 0)),
        pl.BlockSpec((None, NS, POOL_BUF, POOL_WIDTH), lambda i, j: (layer, i, 0, 0)),
        pl.BlockSpec((None, NS, GLA_HEADS, GLA_DK, GLA_DV), lambda i, j: (layer, i, 0, 0, 0)),
    ]
    out_shape = [
        jax.ShapeDtypeStruct((nseq * L, D_MODEL), F32),
        jax.ShapeDtypeStruct((DEPTH, nseq, POOL_BUF, POOL_WIDTH), F32),
        jax.ShapeDtypeStruct((DEPTH, nseq, GLA_HEADS, GLA_DK, GLA_DV), F32),
    ]
    if emit_vn:
        out_specs.append(pl.BlockSpec((None, R, SGU_WIDTH), lambda i, j: (layer, i * n_chunk + j, 0)))
        out_shape.append(jax.ShapeDtypeStruct((DEPTH, nseq * L, SGU_WIDTH), F32))
    assert len(prev) == len(out_shape) - 1
    aliases = {}
    for k, arr in enumerate(prev):
        aliases[len(operands)] = 1 + k
        operands.append(arr)
        in_specs.append(pl.BlockSpec(memory_space=pl.ANY))
    scratch = [
        pltpu.VMEM((R, IN_MAIN), F32),
        pltpu.VMEM((NS, POOL_PAD + T, POOL_WIDTH), F32),
        pltpu.VMEM((R, GLA_KW), F32),
        pltpu.VMEM((R, GLA_KW), F32),
        pltpu.VMEM((R, GLA_KW), F32),
        pltpu.VMEM((GLA_KW, (R // min(GLA_CHUNK, T)) * GLA_DV), F32),
        pltpu.VMEM((R, GLA_VW), F32),
    ]
    outs = pl.pallas_call(
        functools.partial(_mix_kernel, NS=NS, T=T, pos0=pos0, emit_vn=emit_vn, n_prev=len(prev)),
        grid=grid, in_specs=in_specs, out_specs=out_specs, out_shape=out_shape,
        scratch_shapes=scratch, input_output_aliases=aliases,
        compiler_params=pltpu.CompilerParams(
            dimension_semantics=("arbitrary", "arbitrary"), vmem_limit_bytes=VMEM_LIMIT),
        name=f"mix_T{T}",
    )(*operands)
    return outs[0], tuple(outs[1:])


def _ffn_call(layer, xp, xs, wts, final_g, *, TM, final):
    n_p, n_s = xp.shape[0] // TM, xs.shape[0] // TM
    g, wg, wu, wd = wts
    p_idx = lambda i: (jnp.minimum(i, n_p - 1), 0)
    s_idx = lambda i: (jnp.maximum(i - n_p, 0), 0)
    return pl.pallas_call(
        functools.partial(_ffn_kernel, n_prompt_steps=n_p, final=final),
        grid=(n_p + n_s,),
        in_specs=[
            pl.BlockSpec((TM, D_MODEL), p_idx),
            pl.BlockSpec((TM, D_MODEL), s_idx),
            _wspec((1, D_MODEL), layer),
            _wspec((D_MODEL, D_FF), layer),
            _wspec((D_MODEL, D_FF), layer),
            _wspec((D_FF, D_MODEL), layer),
            pl.BlockSpec((1, D_MODEL), lambda i: (0, 0)),
        ],
        out_specs=[pl.BlockSpec((TM, D_MODEL), p_idx), pl.BlockSpec((TM, D_MODEL), s_idx)],
        out_shape=[jax.ShapeDtypeStruct(xp.shape, F32), jax.ShapeDtypeStruct(xs.shape, F32)],
        compiler_params=pltpu.CompilerParams(
            dimension_semantics=("arbitrary",), vmem_limit_bytes=VMEM_LIMIT),
        name="ffn",
    )(xp, xs, g, wg, wu, wd, final_g)


def kernel(x_prompt, x_sample, state_pool, state_gla, attn_norm_g, w_in, pool_w, pool_scale, sgu_norm_g, sgu_ws, sgu_b, gla_wa2, gla_ba, gla_norm_g, w_out, ffn_norm_g, w_gate, w_up, w_down, final_norm_g):
    bp, seq, _ = x_prompt.shape
    bs, dseq, _ = x_sample.shape
    assert seq % PROMPT_TILE == 0 and dseq == 8 and bs % SAMPLE_SEQS == 0

    ng = attn_norm_g[:, None, :]
    win = w_in.astype(BF16)
    wina = jnp.pad(w_in[:, :, IN_MAIN:], ((0, 0), (0, 0), (0, GLA_RANK_PAD - GLA_RANK))).astype(BF16)
    eye_g = jnp.eye(len(POOL_WINDOWS), dtype=F32)
    poolw = jnp.einsum('lgcd,gh->lgchd', pool_w, eye_g).reshape(DEPTH, POOL_WIDTH, POOL_WIDTH).astype(BF16)
    pscale = pool_scale[:, None, :]
    sgug = sgu_norm_g.reshape(DEPTH, 1, SGU_WIDTH)
    sguw_cat = sgu_ws.transpose(0, 2, 1, 3).reshape(DEPTH, SGU_CHUNK, SGU_HEADS * SGU_CHUNK)
    sgub_tile = jnp.repeat(sgu_b.transpose(0, 2, 1), SGU_HEAD_DIM, axis=-1)
    sguw_dec = jnp.repeat(sgu_ws[:, :, :dseq, :dseq].transpose(0, 3, 2, 1), SGU_HEAD_DIM, axis=-1)
    sgub_dec = sgub_tile[:, :dseq]
    wa2 = jnp.pad(gla_wa2, ((0, 0), (0, GLA_RANK_PAD - GLA_RANK), (0, 0))).astype(BF16)
    ba = gla_ba[:, None, :]
    glag = gla_norm_g[:, None, :]
    wout = w_out.astype(BF16)
    fng = ffn_norm_g[:, None, :]
    wg = w_gate.astype(BF16)
    wu = w_up.astype(BF16)
    wd = w_down.astype(BF16)
    fin = final_norm_g[None, :]

    xp = x_prompt.reshape(bp * seq, D_MODEL)
    xs = x_sample.reshape(bs * dseq, D_MODEL)
    pool0_p = jnp.zeros((1, bp, POOL_BUF, POOL_WIDTH), F32)
    gla0_p = jnp.zeros((1, bp, GLA_HEADS, GLA_DK, GLA_DV), F32)

    st_p = (jnp.zeros((DEPTH, bp, POOL_BUF, POOL_WIDTH), F32), jnp.zeros((DEPTH, bp, GLA_HEADS, GLA_DK, GLA_DV), F32))
    st_s = (jnp.zeros((DEPTH, bs, POOL_BUF, POOL_WIDTH), F32), jnp.zeros((DEPTH, bs, GLA_HEADS, GLA_DK, GLA_DV), F32),
            jnp.zeros((DEPTH, bs * dseq, SGU_WIDTH), F32))
    for l in range(DEPTH):
        common = (ng, win, wina, poolw, pscale, sgug)
        tail = (wa2, ba, glag, wout)
        xp, st_p = _mix_call(l, xp, pool0_p, gla0_p, common + (sguw_cat, sgub_tile) + tail, st_p,
                             nseq=bp, L=seq, NS=1, T=PROMPT_TILE, pos0=0, state_layer=0, emit_vn=False)
        xs, st_s = _mix_call(l, xs, state_pool, state_gla, common + (sguw_dec, sgub_dec) + tail, st_s,
                             nseq=bs, L=dseq, NS=SAMPLE_SEQS, T=dseq, pos0=PAST_LEN, state_layer=l, emit_vn=True)
        xp, xs = _ffn_call(l, xp, xs, (fng, wg, wu, wd), fin, TM=FFN_ROWS, final=l == DEPTH - 1)

    return (xp.reshape(bp, seq, D_MODEL), xs.reshape(bs, dseq, D_MODEL),
            st_p[0], st_p[1], st_s[0], st_s[1], st_s[2].reshape(DEPTH, bs, dseq, SGU_WIDTH))
```

```python
import functools

import jax
import jax.numpy as jnp
from jax import lax
from jax.experimental import pallas as pl
from jax.experimental.pallas import tpu as pltpu

F32 = jnp.float32
BF16 = jnp.bfloat16

D_MODEL = 1024
DEPTH = 4
POOL_WIDTH = 256
POOL_WINDOWS = (2, 4, 8, 16)
POOL_GROUP_DIM = 64
POOL_BUF = 15
POOL_PAD = 16
SGU_WIDTH = 256
SGU_HEADS = 4
SGU_HEAD_DIM = 64
SGU_CHUNK = 128
GLA_HEADS = 4
GLA_DK = 64
GLA_DV = 128
GLA_KW = GLA_HEADS * GLA_DK
GLA_VW = GLA_HEADS * GLA_DV
GLA_RANK = 16
GLA_RANK_PAD = 128
GLA_GATE_NORM = 16.0
GLA_CHUNK = 64
MXU_COLS = 256
PART_ROWS = 256
SEQ_UNROLL = 8
D_FF = 2816
EPS = 1e-6

OFF_P, OFF_U, OFF_V, OFF_Q, OFF_K, OFF_VG, OFF_G, OFF_A = 0, 256, 512, 768, 1024, 1280, 1792, 2304
IN_MAIN = 2304
FFN_CHUNKS = ((0, 1024), (1024, 2048), (2048, 2816))

PAST_LEN = 16384
PROMPT_TILE = 1024
SAMPLE_SEQS = 32
FFN_ROWS = 1024
VMEM_LIMIT = 56 * 1024 * 1024
FFN_VMEM_LIMIT = 58 * 1024 * 1024


def _rms(x, g):
    return x * lax.rsqrt(jnp.mean(x * x, axis=-1, keepdims=True) + EPS) * g


def _dot(a, b):
    return jnp.dot(a, b, preferred_element_type=F32)


def _dot_nt(a, b):
    return lax.dot_general(a, b, (((1,), (1,)), ((), ())), preferred_element_type=F32)


def _dot_tn(a, b):
    return lax.dot_general(a, b, (((0,), (0,)), ((), ())), preferred_element_type=F32)


def _split_bf16(x):
    hi = x.astype(BF16)
    lo = (x - hi.astype(F32)).astype(BF16)
    return hi, lo


def _mix_kernel(*refs, NS, T, pos0, emit_vn, n_prev):
    (x_ref, pool0_ref, gla0_ref, ng_ref, win_ref, wina_ref, poolw_ref, pscale_ref,
     sgug_ref, sguw_ref, sgub_ref, wa2_ref, ba_ref, glag_ref, wout_ref) = refs[:15]
    refs = refs[15 + n_prev:]
    xo_ref, pool_o_ref, gla_o_ref = refs[:3]
    vn_o_ref = refs[3] if emit_vn else None
    seed_refs = refs[3 + emit_vn:-7]
    z_s, pz_s, qe_s, ke_s, kd_s, ecol_s, o_s = refs[-7:]
    j = pl.program_id(1)
    R = NS * T
    P = PART_ROWS
    TP = P // NS
    C = min(GLA_CHUNK, T)
    units = P // C
    assert R % P == 0 and (NS == 1 or R == P)

    @pl.when(j == 0)
    def _():
        pz_s[:, 0:1, :] = jnp.zeros((NS, 1, POOL_WIDTH), F32)
        pz_s[:, 1:POOL_PAD, :] = pool0_ref[...]
        gla_o_ref[...] = gla0_ref[...]

    for seed_ref in seed_refs:
        seed_ref[...] = jnp.zeros(seed_ref.shape, F32)

    grp = lax.broadcasted_iota(jnp.int32, (1, 1, POOL_WIDTH), 2) // POOL_GROUP_DIM
    win = jnp.where(grp == 0, 2, jnp.where(grp == 1, 4, jnp.where(grp == 2, 8, 16)))
    r_i = lax.broadcasted_iota(jnp.int32, (SGU_WIDTH, SGU_WIDTH), 0) // SGU_HEAD_DIM
    c_i = lax.broadcasted_iota(jnp.int32, (SGU_WIDTH, SGU_WIDTH), 1) // SGU_HEAD_DIM
    head_ones = (r_i == c_i).astype(BF16)
    lane_head = lax.broadcasted_iota(jnp.int32, (1, SGU_WIDTH), 1) // SGU_HEAD_DIM
    if T >= SGU_CHUNK:
        row = lax.broadcasted_iota(jnp.int32, (SGU_CHUNK, SGU_HEADS * SGU_CHUNK), 0)
        col = lax.broadcasted_iota(jnp.int32, (SGU_CHUNK, SGU_HEADS * SGU_CHUNK), 1)
        wcat = jnp.where((col & (SGU_CHUNK - 1)) <= row, sguw_ref[...], 0.0).astype(BF16)
    rr = lax.broadcasted_iota(jnp.int32, (P, P), 0)
    cc = lax.broadcasted_iota(jnp.int32, (P, P), 1)
    tri_bd = ((rr // C == cc // C) & (rr >= cc)).astype(BF16)
    sel2 = (lax.broadcasted_iota(jnp.int32, (8, 2 * GLA_DV), 0) // 2
            == lax.broadcasted_iota(jnp.int32, (8, 2 * GLA_DV), 1) // GLA_DV).astype(BF16)
    khead = lax.broadcasted_iota(jnp.int32, (1, GLA_KW), 1) // GLA_DK
    vhead = lax.broadcasted_iota(jnp.int32, (1, GLA_VW), 1) // GLA_DV
    arow = lax.broadcasted_iota(jnp.int32, (C, GLA_HEADS * C), 0)
    acol = lax.broadcasted_iota(jnp.int32, (C, GLA_HEADS * C), 1) & (C - 1)
    causal = acol <= arow
    zblk = jnp.zeros((GLA_DK, GLA_DV), F32)

    def gla_unit_stages(r0, e0, read_state, write_state):
        rows = pl.ds(r0, C)
        qe = qe_s[rows, :].astype(BF16)
        ke = ke_s[rows, :]
        kd = kd_s[rows, :]
        vv = z_s[rows, OFF_VG:OFF_VG + GLA_VW]
        ke_bd = jnp.concatenate(
            [jnp.where(khead == hh, ke, 0.0) for hh in range(GLA_HEADS)], axis=0).astype(BF16)
        att_raw = _dot_nt(qe, ke_bd)
        v_bd = jnp.concatenate(
            [jnp.where(vhead == hh, vv, 0.0) for hh in range(GLA_HEADS)], axis=0).astype(BF16)
        kd_st = jnp.concatenate(
            [kd[:, hh * GLA_DK:(hh + 1) * GLA_DK] for hh in range(GLA_HEADS)], axis=0).astype(BF16)
        upd = _dot_tn(kd_st, v_bd)
        yield
        att = jnp.where(causal, att_raw, 0.0).astype(BF16)
        o_intra = _dot(att, v_bd)
        yield
        s_blocks = read_state()
        s_bd = jnp.concatenate(
            [jnp.concatenate([s_blocks[hh] if hc == hh else zblk for hc in range(GLA_HEADS)], axis=1)
             for hh in range(GLA_HEADS)], axis=0)
        o_s[rows, :] = o_intra + _dot(qe, s_bd.astype(BF16))
        ecol = ecol_s[:, pl.ds(e0, GLA_DV)]
        write_state([ecol[hh * GLA_DK:(hh + 1) * GLA_DK] * s_blocks[hh] + upd[:, hh * GLA_DV:(hh + 1) * GLA_DV]
                     for hh in range(GLA_HEADS)])

    def step(gen):
        try:
            next(gen)
        except StopIteration:
            pass

    def part_stages(part):
        r0p = part * P
        prow = slice(r0p, r0p + P)
        t0 = part * TP
        x = x_ref[prow, :]
        h = _rms(x, ng_ref[...]).astype(BF16)
        yield
        for n in range(IN_MAIN // MXU_COLS):
            cs = slice(n * MXU_COLS, (n + 1) * MXU_COLS)
            z_s[prow, cs] = _dot(h, win_ref[:, cs])
            yield
        alow = _dot(h, wina_ref[...]).astype(BF16)

        p3 = z_s[prow, OFF_P:OFF_P + POOL_WIDTH].reshape(NS, TP, POOL_WIDTH)
        pz_s[:, POOL_PAD + t0:POOL_PAD + t0 + TP, :] = p3
        e = pz_s[:, t0:t0 + POOL_PAD + TP, :]
        w2 = e + pltpu.roll(e, 1, 1)
        w4 = w2 + pltpu.roll(w2, 2, 1)
        w8 = w4 + pltpu.roll(w4, 4, 1)
        w16 = w8 + pltpu.roll(w8, 8, 1)
        wsum = jnp.where(grp == 0, w2, jnp.where(grp == 1, w4, jnp.where(grp == 2, w8, w16)))
        wsum = wsum[:, POOL_PAD:POOL_PAD + TP, :]
        pos = pos0 + j * T + t0 + lax.broadcasted_iota(jnp.int32, (1, TP, 1), 1)
        cnt = jnp.minimum(pos + 1, win).astype(F32)
        d = (wsum / cnt - p3).reshape(P, POOL_WIDTH)
        a_out = _dot(d.astype(BF16), poolw_ref[...]) * pscale_ref[...]
        yield

        v = z_s[prow, OFF_V:OFF_V + SGU_WIDTH]
        u = z_s[prow, OFF_U:OFF_U + SGU_WIDTH]
        vsq_hi, vsq_lo = _split_bf16(v * v)
        ss = _dot(vsq_hi, head_ones) + _dot(vsq_lo, head_ones)
        vn = v * lax.rsqrt(ss * (1.0 / SGU_HEAD_DIM) + EPS) * sgug_ref[...]
        if emit_vn:
            vn_o_ref[prow, :] = vn
        yield
        if T >= SGU_CHUNK:
            parts = []
            for c in range(P // SGU_CHUNK):
                vc = vn[c * SGU_CHUNK:(c + 1) * SGU_CHUNK]
                stack = jnp.concatenate(
                    [jnp.where(lane_head == hh, vc, 0.0) for hh in range(SGU_HEADS)], axis=0)
                parts.append(_dot(wcat, stack.astype(BF16)) + sgub_ref[...])
            s_gate = jnp.concatenate(parts, axis=0)
        else:
            vn3 = vn.reshape(NS, T, SGU_WIDTH)
            srow = lax.broadcasted_iota(jnp.int32, (T, SGU_WIDTH), 0)
            s3 = jnp.zeros((NS, T, SGU_WIDTH), F32) + sgub_ref[...][None]
            for jj in range(T):
                coef = jnp.where(srow >= jj, sguw_ref[jj], 0.0)
                s3 = s3 + vn3[:, jj:jj + 1, :] * coef[None]
            s_gate = s3.reshape(P, SGU_WIDTH)
        b_out = u * s_gate
        yield

        xg = _dot(alow, wa2_ref[...]) + ba_ref[...]
        lg = (jnp.minimum(xg, 0.0) - jnp.log(1.0 + jnp.exp(-jnp.abs(xg)))) * (1.0 / GLA_GATE_NORM)
        lg_hi, lg_lo = _split_bf16(lg)
        bcum = _dot(tri_bd, lg_hi) + _dot(tri_bd, lg_lo)
        b3 = bcum.reshape(units, C, GLA_KW)
        btot = jnp.broadcast_to(b3[:, C - 1:C, :], (units, C, GLA_KW)).reshape(P, GLA_KW)
        ke_all = z_s[prow, OFF_K:OFF_K + GLA_KW] * jnp.exp(-bcum)
        qe_s[prow, :] = z_s[prow, OFF_Q:OFF_Q + GLA_KW] * (GLA_DK ** -0.5) * jnp.exp(bcum)
        ke_s[prow, :] = ke_all
        kd_s[prow, :] = ke_all * jnp.exp(btot)
        yield
        etot = jnp.exp(b3[:, C - 1, :])
        et_hi = etot.astype(BF16).astype(F32)
        et_lo = etot - et_hi
        zpad = jnp.zeros((4, GLA_KW), F32)
        for pp in range(units // 2):
            a_rows = jnp.concatenate([et_hi[2 * pp:2 * pp + 1], et_lo[2 * pp:2 * pp + 1],
                                      et_hi[2 * pp + 1:2 * pp + 2], et_lo[2 * pp + 1:2 * pp + 2], zpad], axis=0)
            e0 = (part * units + 2 * pp) * GLA_DV
            ecol_s[:, e0:e0 + 2 * GLA_DV] = _dot_tn(a_rows.astype(BF16), sel2)
        yield

        if NS == 1:
            state = [[gla_o_ref[0, hh] for hh in range(GLA_HEADS)]]
            ug = [gla_unit_stages(r0p + c * C, (part * units + c) * GLA_DV,
                                  lambda: state[0], lambda new: state.__setitem__(0, new)) for c in range(units)]
            step(ug[0])
            step(ug[0])
            for c in range(units):
                if c + 1 < units:
                    step(ug[c + 1])
                step(ug[c])
                if c + 1 < units:
                    step(ug[c + 1])
                yield
            for hh in range(GLA_HEADS):
                gla_o_ref[0, hh] = state[0][hh]
        else:
            assert units == NS and NS % SEQ_UNROLL == 0

            def seq_group(g, carry):
                def unit(sq):
                    def write(new):
                        for hh in range(GLA_HEADS):
                            gla_o_ref[sq, hh] = new[hh]
                    return gla_unit_stages(pl.multiple_of(sq * C, C), pl.multiple_of(sq * GLA_DV, GLA_DV),
                                           lambda: [gla_o_ref[sq, hh] for hh in range(GLA_HEADS)], write)
                ug = [unit(g * SEQ_UNROLL + i) for i in range(SEQ_UNROLL)]
                for _ in range(3):
                    for u_gen in ug:
                        step(u_gen)
                return carry

            lax.fori_loop(0, NS // SEQ_UNROLL, seq_group, 0)

        gate = z_s[prow, OFF_G:OFF_G + GLA_VW]
        o_all = o_s[prow, :]
        c_parts = []
        for hh in range(GLA_HEADS):
            sl = slice(hh * GLA_DV, (hh + 1) * GLA_DV)
            gh = gate[:, sl]
            c_parts.append(_rms(o_all[:, sl], glag_ref[...]) * (gh * jax.nn.sigmoid(gh)))

        mix = jnp.concatenate([a_out, b_out] + c_parts, axis=-1).astype(BF16)
        yield
        for n in range(D_MODEL // MXU_COLS):
            cs = slice(n * MXU_COLS, (n + 1) * MXU_COLS)
            xo_ref[prow, cs] = x[:, cs] + _dot(mix, wout_ref[:, cs])
            yield


    n_proj = 1 + IN_MAIN // MXU_COLS
    gens = [part_stages(p) for p in range(R // P)]
    done = [False] * len(gens)

    def advance(p, n=1):
        for _ in range(n):
            if not done[p]:
                try:
                    next(gens[p])
                except StopIteration:
                    done[p] = True

    n_mix = 6 + units if NS == 1 else 7
    advance(0, n_proj)
    for p in range(len(gens)):
        for _ in range(n_mix):
            advance(p)
            if p + 1 < len(gens):
                advance(p + 1)
            if p > 0:
                advance(p - 1)
    for p in range(len(gens)):
        while not done[p]:
            advance(p)

    pool_o_ref[...] = pz_s[:, T + 1:T + POOL_PAD, :]
    if T >= POOL_PAD:
        pz_s[:, 0:POOL_PAD, :] = pz_s[:, T:T + POOL_PAD, :]


def _ffn_kernel(xp_ref, xs_ref, g_ref, wg_ref, wu_ref, wd_ref, fg_ref, op_ref, os_ref, *, n_prompt_steps, final):
    def part_stages(x_ref, o_ref, rows):
        x = x_ref[rows, :]
        hf = _rms(x, g_ref[...]).astype(BF16)
        yield
        acc = x
        for lo, hi in FFN_CHUNKS:
            gt = _dot(hf, wg_ref[:, lo:hi])
            yield
            up = _dot(hf, wu_ref[:, lo:hi])
            yield
            act = (gt * jax.nn.sigmoid(gt) * up).astype(BF16)
            acc = acc + _dot(act, wd_ref[lo:hi, :])
            yield
        if final:
            acc = _rms(acc, fg_ref[...])
        o_ref[rows, :] = acc

    def ffn(x_ref, o_ref):
        n_rows = x_ref.shape[0]
        gens = [part_stages(x_ref, o_ref, slice(r, r + PART_ROWS)) for r in range(0, n_rows, PART_ROWS)]
        live = list(gens)
        lag = 0
        while live:
            for g in list(live[:lag + 1]):
                try:
                    next(g)
                except StopIteration:
                    live.remove(g)
            lag += 1

    i = pl.program_id(0)
    pl.when(i < n_prompt_steps)(lambda: ffn(xp_ref, op_ref))
    pl.when(i >= n_prompt_steps)(lambda: ffn(xs_ref, os_ref))


def _wspec(shape, layer):
    nd = len(shape)
    return pl.BlockSpec((None,) + tuple(shape), lambda *g, _l=layer, _n=nd: (_l,) + (0,) * _n,
                        pipeline_mode=pl.Buffered(1))


def _mix_call(layer, x2, pool0, gla0, wts, prev, *, nseq, L, NS, T, pos0, state_layer, emit_vn, seed_shapes=()):
    R = NS * T
    n_chunk = L // T
    grid = (nseq // NS, n_chunk)
    (ng, win, wina, poolw, pscale, sgug, sguw, sgub, wa2, ba, glag, wout) = wts
    sl = state_layer
    in_specs = [
        pl.BlockSpec((R, D_MODEL), lambda i, j: (i * n_chunk + j, 0)),
        pl.BlockSpec((None, NS, POOL_BUF, POOL_WIDTH), lambda i, j: (sl, i, 0, 0)),
        pl.BlockSpec((None, NS, GLA_HEADS, GLA_DK, GLA_DV), lambda i, j: (sl, i, 0, 0, 0)),
        _wspec((1, D_MODEL), layer),
        _wspec((D_MODEL, IN_MAIN), layer),
        _wspec((D_MODEL, GLA_RANK_PAD), layer),
        _wspec((POOL_WIDTH, POOL_WIDTH), layer),
        _wspec((1, POOL_WIDTH), layer),
        _wspec((1, SGU_WIDTH), layer),
        _wspec(sguw.shape[1:], layer),
        _wspec(sgub.shape[1:], layer),
        _wspec((GLA_RANK_PAD, GLA_KW), layer),
        _wspec((1, GLA_KW), layer),
        _wspec((1, GLA_DV), layer),
        _wspec((D_MODEL, D_MODEL), layer),
    ]
    operands = [x2, pool0, gla0, ng, win, wina, poolw, pscale, sgug, sguw, sgub, wa2, ba, glag, wout]
    out_specs = [
        pl.BlockSpec((R, D_MODEL), lambda i, j: (i * n_chunk + j, 0)),
        pl.BlockSpec((None, NS, POOL_BUF, POOL_WIDTH), lambda i, j: (layer, i, 0, 0)),
        pl.BlockSpec((None, NS, GLA_HEADS, GLA_DK, GLA_DV), lambda i, j: (layer, i, 0, 0, 0)),
    ]
    out_shape = [
        jax.ShapeDtypeStruct((nseq * L, D_MODEL), F32),
        jax.ShapeDtypeStruct((DEPTH, nseq, POOL_BUF, POOL_WIDTH), F32),
        jax.ShapeDtypeStruct((DEPTH, nseq, GLA_HEADS, GLA_DK, GLA_DV), F32),
    ]
    if emit_vn:
        out_specs.append(pl.BlockSpec((None, R, SGU_WIDTH), lambda i, j: (layer, i * n_chunk + j, 0)))
        out_shape.append(jax.ShapeDtypeStruct((DEPTH, nseq * L, SGU_WIDTH), F32))
    assert len(prev) == len(out_shape) - 1
    n_steps = grid[0] * grid[1]
    for shp in seed_shapes:
        blk = (shp[0], shp[1] // n_steps) + tuple(shp[2:])
        out_specs.append(pl.BlockSpec(blk, lambda i, j, _n=len(shp): (0, i * n_chunk + j) + (0,) * (_n - 2)))
        out_shape.append(jax.ShapeDtypeStruct(shp, F32))
    aliases = {}
    for k, arr in enumerate(prev):
        aliases[len(operands)] = 1 + k
        operands.append(arr)
        in_specs.append(pl.BlockSpec(memory_space=pl.ANY))
    scratch = [
        pltpu.VMEM((R, IN_MAIN), F32),
        pltpu.VMEM((NS, POOL_PAD + T, POOL_WIDTH), F32),
        pltpu.VMEM((R, GLA_KW), F32),
        pltpu.VMEM((R, GLA_KW), F32),
        pltpu.VMEM((R, GLA_KW), F32),
        pltpu.VMEM((GLA_KW, (R // min(GLA_CHUNK, T)) * GLA_DV), F32),
        pltpu.VMEM((R, GLA_VW), F32),
    ]
    outs = pl.pallas_call(
        functools.partial(_mix_kernel, NS=NS, T=T, pos0=pos0, emit_vn=emit_vn, n_prev=len(prev)),
        grid=grid, in_specs=in_specs, out_specs=out_specs, out_shape=out_shape,
        scratch_shapes=scratch, input_output_aliases=aliases,
        compiler_params=pltpu.CompilerParams(
            dimension_semantics=("arbitrary", "arbitrary"), vmem_limit_bytes=VMEM_LIMIT),
        name=f"mix_T{T}",
    )(*operands)
    n_state = len(prev)
    return outs[0], tuple(outs[1:1 + n_state]), tuple(outs[1 + n_state:])


def _ffn_call(layer, xp, xs, wts, final_g, *, TM, final):
    n_p, n_s = xp.shape[0] // TM, xs.shape[0] // TM
    g, wg, wu, wd = wts
    p_idx = lambda i: (jnp.minimum(i, n_p - 1), 0)
    s_idx = lambda i: (jnp.maximum(i - n_p, 0), 0)
    return pl.pallas_call(
        functools.partial(_ffn_kernel, n_prompt_steps=n_p, final=final),
        grid=(n_p + n_s,),
        in_specs=[
            pl.BlockSpec((TM, D_MODEL), p_idx),
            pl.BlockSpec((TM, D_MODEL), s_idx, pipeline_mode=pl.Buffered(1 if n_s == 1 else 2)),
            _wspec((1, D_MODEL), layer),
            _wspec((D_MODEL, D_FF), layer),
            _wspec((D_MODEL, D_FF), layer),
            _wspec((D_FF, D_MODEL), layer),
            pl.BlockSpec((1, D_MODEL), lambda i: (0, 0)),
        ],
        out_specs=[pl.BlockSpec((TM, D_MODEL), p_idx), pl.BlockSpec((TM, D_MODEL), s_idx)],
        out_shape=[jax.ShapeDtypeStruct(xp.shape, F32), jax.ShapeDtypeStruct(xs.shape, F32)],
        compiler_params=pltpu.CompilerParams(
            dimension_semantics=("arbitrary",), vmem_limit_bytes=FFN_VMEM_LIMIT),
        name="ffn",
    )(xp, xs, g, wg, wu, wd, final_g)


def kernel(x_prompt, x_sample, state_pool, state_gla, attn_norm_g, w_in, pool_w, pool_scale, sgu_norm_g, sgu_ws, sgu_b, gla_wa2, gla_ba, gla_norm_g, w_out, ffn_norm_g, w_gate, w_up, w_down, final_norm_g):
    bp, seq, _ = x_prompt.shape
    bs, dseq, _ = x_sample.shape
    assert seq % PROMPT_TILE == 0 and dseq == 8 and bs % SAMPLE_SEQS == 0

    ng = attn_norm_g[:, None, :]
    win = w_in.astype(BF16)
    wina = jnp.pad(w_in[:, :, IN_MAIN:], ((0, 0), (0, 0), (0, GLA_RANK_PAD - GLA_RANK))).astype(BF16)
    eye_g = jnp.eye(len(POOL_WINDOWS), dtype=F32)
    poolw = jnp.einsum('lgcd,gh->lgchd', pool_w, eye_g).reshape(DEPTH, POOL_WIDTH, POOL_WIDTH).astype(BF16)
    pscale = pool_scale[:, None, :]
    sgug = sgu_norm_g.reshape(DEPTH, 1, SGU_WIDTH)
    sguw_cat = sgu_ws.transpose(0, 2, 1, 3).reshape(DEPTH, SGU_CHUNK, SGU_HEADS * SGU_CHUNK)
    sgub_tile = jnp.repeat(sgu_b.transpose(0, 2, 1), SGU_HEAD_DIM, axis=-1)
    sguw_dec = jnp.repeat(sgu_ws[:, :, :dseq, :dseq].transpose(0, 3, 2, 1), SGU_HEAD_DIM, axis=-1)
    sgub_dec = sgub_tile[:, :dseq]
    wa2 = jnp.pad(gla_wa2, ((0, 0), (0, GLA_RANK_PAD - GLA_RANK), (0, 0))).astype(BF16)
    ba = gla_ba[:, None, :]
    glag = gla_norm_g[:, None, :]
    wout = w_out.astype(BF16)
    fng = ffn_norm_g[:, None, :]
    wg = w_gate.astype(BF16)
    wu = w_up.astype(BF16)
    wd = w_down.astype(BF16)
    fin = final_norm_g[None, :]

    xp = x_prompt.reshape(bp * seq, D_MODEL)
    xs = x_sample.reshape(bs * dseq, D_MODEL)
    pool0_p = jnp.zeros((1, bp, POOL_BUF, POOL_WIDTH), F32)
    gla0_p = jnp.zeros((1, bp, GLA_HEADS, GLA_DK, GLA_DV), F32)

    st_p = (jnp.zeros((DEPTH, bp, POOL_BUF, POOL_WIDTH), F32), jnp.zeros((DEPTH, bp, GLA_HEADS, GLA_DK, GLA_DV), F32))
    seeds_s = ((DEPTH, bs, POOL_BUF, POOL_WIDTH), (DEPTH, bs, GLA_HEADS, GLA_DK, GLA_DV), (DEPTH, bs * dseq, SGU_WIDTH))
    st_s = None
    for l in range(DEPTH):
        common = (ng, win, wina, poolw, pscale, sgug)
        tail = (wa2, ba, glag, wout)
        xp, st_p, seeds = _mix_call(l, xp, pool0_p, gla0_p, common + (sguw_cat, sgub_tile) + tail, st_p,
                                    nseq=bp, L=seq, NS=1, T=PROMPT_TILE, pos0=0, state_layer=0, emit_vn=False,
                                    seed_shapes=seeds_s if l == 0 else ())
        st_s = seeds if l == 0 else st_s
        xs, st_s, _ = _mix_call(l, xs, state_pool, state_gla, common + (sguw_dec, sgub_dec) + tail, st_s,
                                nseq=bs, L=dseq, NS=SAMPLE_SEQS, T=dseq, pos0=PAST_LEN, state_layer=l, emit_vn=True)
        xp, xs = _ffn_call(l, xp, xs, (fng, wg, wu, wd), fin, TM=FFN_ROWS, final=l == DEPTH - 1)

    return (xp.reshape(bp, seq, D_MODEL), xs.reshape(bs, dseq, D_MODEL),
            st_p[0], st_p[1], st_s[0], st_s[1], st_s[2].reshape(DEPTH, bs, dseq, SGU_WIDTH))
```

```python
import functools

import jax
import jax.numpy as jnp
from jax import lax
from jax.experimental import pallas as pl
from jax.experimental.pallas import tpu as pltpu

F32 = jnp.float32
BF16 = jnp.bfloat16

D_MODEL = 1024
DEPTH = 4
POOL_WIDTH = 256
POOL_WINDOWS = (2, 4, 8, 16)
POOL_GROUP_DIM = 64
POOL_BUF = 15
POOL_PAD = 16
SGU_WIDTH = 256
SGU_HEADS = 4
SGU_HEAD_DIM = 64
SGU_CHUNK = 128
GLA_HEADS = 4
GLA_DK = 64
GLA_DV = 128
GLA_KW = GLA_HEADS * GLA_DK
GLA_VW = GLA_HEADS * GLA_DV
GLA_RANK = 16
GLA_RANK_PAD = 128
GLA_GATE_NORM = 16.0
GLA_CHUNK = 64
MXU_COLS = 256
PART_ROWS = 256
SEQ_UNROLL = 8
D_FF = 2816
EPS = 1e-6

OFF_P, OFF_U, OFF_V, OFF_Q, OFF_K, OFF_VG, OFF_G, OFF_A = 0, 256, 512, 768, 1024, 1280, 1792, 2304
IN_MAIN = 2304
FFN_CHUNKS = ((0, 1536), (1536, 2816))

PAST_LEN = 16384
PROMPT_TILE = 1024
SAMPLE_SEQS = 32
FFN_ROWS = 512
VMEM_LIMIT = 56 * 1024 * 1024


def _rms(x, g):
    return x * lax.rsqrt(jnp.mean(x * x, axis=-1, keepdims=True) + EPS) * g


def _dot(a, b):
    return jnp.dot(a, b, preferred_element_type=F32)


def _dot_nt(a, b):
    return lax.dot_general(a, b, (((1,), (1,)), ((), ())), preferred_element_type=F32)


def _dot_tn(a, b):
    return lax.dot_general(a, b, (((0,), (0,)), ((), ())), preferred_element_type=F32)


def _split_bf16(x):
    hi = x.astype(BF16)
    lo = (x - hi.astype(F32)).astype(BF16)
    return hi, lo


def _mix_kernel(*refs, NS, T, pos0, emit_vn, n_prev):
    (x_ref, pool0_ref, gla0_ref, ng_ref, win_ref, wina_ref, poolw_ref, pscale_ref,
     sgug_ref, sguw_ref, sgub_ref, wa2_ref, ba_ref, glag_ref, wout_ref) = refs[:15]
    refs = refs[15 + n_prev:]
    xo_ref, pool_o_ref, gla_o_ref = refs[:3]
    vn_o_ref = refs[3] if emit_vn else None
    seed_refs = refs[3 + emit_vn:-7]
    z_s, pz_s, qe_s, ke_s, kd_s, ecol_s, o_s = refs[-7:]
    j = pl.program_id(1)
    R = NS * T
    P = PART_ROWS
    TP = P // NS
    C = min(GLA_CHUNK, T)
    units = P // C
    assert R % P == 0 and (NS == 1 or R == P)

    @pl.when(j == 0)
    def _():
        pz_s[:, 0:1, :] = jnp.zeros((NS, 1, POOL_WIDTH), F32)
        pz_s[:, 1:POOL_PAD, :] = pool0_ref[...]
        gla_o_ref[...] = gla0_ref[...]

    for seed_ref in seed_refs:
        seed_ref[...] = jnp.zeros(seed_ref.shape, F32)

    grp = lax.broadcasted_iota(jnp.int32, (1, 1, POOL_WIDTH), 2) // POOL_GROUP_DIM
    win = jnp.where(grp == 0, 2, jnp.where(grp == 1, 4, jnp.where(grp == 2, 8, 16)))
    r_i = lax.broadcasted_iota(jnp.int32, (SGU_WIDTH, SGU_WIDTH), 0) // SGU_HEAD_DIM
    c_i = lax.broadcasted_iota(jnp.int32, (SGU_WIDTH, SGU_WIDTH), 1) // SGU_HEAD_DIM
    head_ones = (r_i == c_i).astype(BF16)
    lane_head = lax.broadcasted_iota(jnp.int32, (1, SGU_WIDTH), 1) // SGU_HEAD_DIM
    if T >= SGU_CHUNK:
        row = lax.broadcasted_iota(jnp.int32, (SGU_CHUNK, SGU_HEADS * SGU_CHUNK), 0)
        col = lax.broadcasted_iota(jnp.int32, (SGU_CHUNK, SGU_HEADS * SGU_CHUNK), 1)
        wcat = jnp.where((col & (SGU_CHUNK - 1)) <= row, sguw_ref[...], 0.0).astype(BF16)
    rr = lax.broadcasted_iota(jnp.int32, (P, P), 0)
    cc = lax.broadcasted_iota(jnp.int32, (P, P), 1)
    tri_bd = ((rr // C == cc // C) & (rr >= cc)).astype(BF16)
    sel2 = (lax.broadcasted_iota(jnp.int32, (8, 2 * GLA_DV), 0) // 2
            == lax.broadcasted_iota(jnp.int32, (8, 2 * GLA_DV), 1) // GLA_DV).astype(BF16)
    khead = lax.broadcasted_iota(jnp.int32, (1, GLA_KW), 1) // GLA_DK
    vhead = lax.broadcasted_iota(jnp.int32, (1, GLA_VW), 1) // GLA_DV
    arow = lax.broadcasted_iota(jnp.int32, (C, GLA_HEADS * C), 0)
    acol = lax.broadcasted_iota(jnp.int32, (C, GLA_HEADS * C), 1) & (C - 1)
    causal = acol <= arow
    zblk = jnp.zeros((GLA_DK, GLA_DV), F32)

    def gla_unit_stages(r0, e0, read_state, write_state):
        rows = pl.ds(r0, C)
        qe = qe_s[rows, :].astype(BF16)
        ke = ke_s[rows, :]
        kd = kd_s[rows, :]
        vv = z_s[rows, OFF_VG:OFF_VG + GLA_VW]
        ke_bd = jnp.concatenate(
            [jnp.where(khead == hh, ke, 0.0) for hh in range(GLA_HEADS)], axis=0).astype(BF16)
        att_raw = _dot_nt(qe, ke_bd)
        v_bd = jnp.concatenate(
            [jnp.where(vhead == hh, vv, 0.0) for hh in range(GLA_HEADS)], axis=0).astype(BF16)
        kd_st = jnp.concatenate(
            [kd[:, hh * GLA_DK:(hh + 1) * GLA_DK] for hh in range(GLA_HEADS)], axis=0).astype(BF16)
        upd = _dot_tn(kd_st, v_bd)
        yield
        att = jnp.where(causal, att_raw, 0.0).astype(BF16)
        o_intra = _dot(att, v_bd)
        yield
        s_blocks = read_state()
        s_bd = jnp.concatenate(
            [jnp.concatenate([s_blocks[hh] if hc == hh else zblk for hc in range(GLA_HEADS)], axis=1)
             for hh in range(GLA_HEADS)], axis=0)
        o_s[rows, :] = o_intra + _dot(qe, s_bd.astype(BF16))
        ecol = ecol_s[:, pl.ds(e0, GLA_DV)]
        write_state([ecol[hh * GLA_DK:(hh + 1) * GLA_DK] * s_blocks[hh] + upd[:, hh * GLA_DV:(hh + 1) * GLA_DV]
                     for hh in range(GLA_HEADS)])

    def step(gen):
        try:
            next(gen)
        except StopIteration:
            pass

    def part_stages(part):
        r0p = part * P
        prow = slice(r0p, r0p + P)
        t0 = part * TP
        x = x_ref[prow, :]
        h = _rms(x, ng_ref[...]).astype(BF16)
        yield
        for n in range(IN_MAIN // MXU_COLS):
            cs = slice(n * MXU_COLS, (n + 1) * MXU_COLS)
            z_s[prow, cs] = _dot(h, win_ref[:, cs])
            yield
        alow = _dot(h, wina_ref[...]).astype(BF16)

        p3 = z_s[prow, OFF_P:OFF_P + POOL_WIDTH].reshape(NS, TP, POOL_WIDTH)
        pz_s[:, POOL_PAD + t0:POOL_PAD + t0 + TP, :] = p3
        e = pz_s[:, t0:t0 + POOL_PAD + TP, :]
        w2 = e + pltpu.roll(e, 1, 1)
        w4 = w2 + pltpu.roll(w2, 2, 1)
        w8 = w4 + pltpu.roll(w4, 4, 1)
        w16 = w8 + pltpu.roll(w8, 8, 1)
        wsum = jnp.where(grp == 0, w2, jnp.where(grp == 1, w4, jnp.where(grp == 2, w8, w16)))
        wsum = wsum[:, POOL_PAD:POOL_PAD + TP, :]
        pos = pos0 + j * T + t0 + lax.broadcasted_iota(jnp.int32, (1, TP, 1), 1)
        cnt = jnp.minimum(pos + 1, win).astype(F32)
        d = (wsum / cnt - p3).reshape(P, POOL_WIDTH)
        a_out = _dot(d.astype(BF16), poolw_ref[...]) * pscale_ref[...]
        yield

        v = z_s[prow, OFF_V:OFF_V + SGU_WIDTH]
        u = z_s[prow, OFF_U:OFF_U + SGU_WIDTH]
        vsq_hi, vsq_lo = _split_bf16(v * v)
        ss = _dot(vsq_hi, head_ones) + _dot(vsq_lo, head_ones)
        vn = v * lax.rsqrt(ss * (1.0 / SGU_HEAD_DIM) + EPS) * sgug_ref[...]
        if emit_vn:
            vn_o_ref[prow, :] = vn
        yield
        if T >= SGU_CHUNK:
            parts = []
            for c in range(P // SGU_CHUNK):
                vc = vn[c * SGU_CHUNK:(c + 1) * SGU_CHUNK]
                stack = jnp.concatenate(
                    [jnp.where(lane_head == hh, vc, 0.0) for hh in range(SGU_HEADS)], axis=0)
                parts.append(_dot(wcat, stack.astype(BF16)) + sgub_ref[...])
            s_gate = jnp.concatenate(parts, axis=0)
        else:
            vn3 = vn.reshape(NS, T, SGU_WIDTH)
            srow = lax.broadcasted_iota(jnp.int32, (T, SGU_WIDTH), 0)
            s3 = jnp.zeros((NS, T, SGU_WIDTH), F32) + sgub_ref[...][None]
            for jj in range(T):
                coef = jnp.where(srow >= jj, sguw_ref[jj], 0.0)
                s3 = s3 + vn3[:, jj:jj + 1, :] * coef[None]
            s_gate = s3.reshape(P, SGU_WIDTH)
        b_out = u * s_gate
        yield

        xg = _dot(alow, wa2_ref[...]) + ba_ref[...]
        lg = (jnp.minimum(xg, 0.0) - jnp.log(1.0 + jnp.exp(-jnp.abs(xg)))) * (1.0 / GLA_GATE_NORM)
        lg_hi, lg_lo = _split_bf16(lg)
        bcum = _dot(tri_bd, lg_hi) + _dot(tri_bd, lg_lo)
        b3 = bcum.reshape(units, C, GLA_KW)
        btot = jnp.broadcast_to(b3[:, C - 1:C, :], (units, C, GLA_KW)).reshape(P, GLA_KW)
        ke_all = z_s[prow, OFF_K:OFF_K + GLA_KW] * jnp.exp(-bcum)
        qe_s[prow, :] = z_s[prow, OFF_Q:OFF_Q + GLA_KW] * (GLA_DK ** -0.5) * jnp.exp(bcum)
        ke_s[prow, :] = ke_all
        kd_s[prow, :] = ke_all * jnp.exp(btot)
        yield
        etot = jnp.exp(b3[:, C - 1, :])
        et_hi = etot.astype(BF16).astype(F32)
        et_lo = etot - et_hi
        zpad = jnp.zeros((4, GLA_KW), F32)
        for pp in range(units // 2):
            a_rows = jnp.concatenate([et_hi[2 * pp:2 * pp + 1], et_lo[2 * pp:2 * pp + 1],
                                      et_hi[2 * pp + 1:2 * pp + 2], et_lo[2 * pp + 1:2 * pp + 2], zpad], axis=0)
            e0 = (part * units + 2 * pp) * GLA_DV
            ecol_s[:, e0:e0 + 2 * GLA_DV] = _dot_tn(a_rows.astype(BF16), sel2)
        yield

        if NS == 1:
            state = [[gla_o_ref[0, hh] for hh in range(GLA_HEADS)]]
            ug = [gla_unit_stages(r0p + c * C, (part * units + c) * GLA_DV,
                                  lambda: state[0], lambda new: state.__setitem__(0, new)) for c in range(units)]
            step(ug[0])
            step(ug[0])
            for c in range(units):
                if c + 1 < units:
                    step(ug[c + 1])
                step(ug[c])
                if c + 1 < units:
                    step(ug[c + 1])
                yield
            for hh in range(GLA_HEADS):
                gla_o_ref[0, hh] = state[0][hh]
        else:
            assert units == NS and NS % SEQ_UNROLL == 0

            def seq_group(g, carry):
                def unit(sq):
                    def write(new):
                        for hh in range(GLA_HEADS):
                            gla_o_ref[sq, hh] = new[hh]
                    return gla_unit_stages(pl.multiple_of(sq * C, C), pl.multiple_of(sq * GLA_DV, GLA_DV),
                                           lambda: [gla_o_ref[sq, hh] for hh in range(GLA_HEADS)], write)
                ug = [unit(g * SEQ_UNROLL + i) for i in range(SEQ_UNROLL)]
                for _ in range(3):
                    for u_gen in ug:
                        step(u_gen)
                return carry

            lax.fori_loop(0, NS // SEQ_UNROLL, seq_group, 0)

        gate = z_s[prow, OFF_G:OFF_G + GLA_VW]
        o_all = o_s[prow, :]
        c_parts = []
        for hh in range(GLA_HEADS):
            sl = slice(hh * GLA_DV, (hh + 1) * GLA_DV)
            gh = gate[:, sl]
            c_parts.append(_rms(o_all[:, sl], glag_ref[...]) * (gh * jax.nn.sigmoid(gh)))

        mix = jnp.concatenate([a_out, b_out] + c_parts, axis=-1).astype(BF16)
        yield
        for n in range(D_MODEL // MXU_COLS):
            cs = slice(n * MXU_COLS, (n + 1) * MXU_COLS)
            xo_ref[prow, cs] = x[:, cs] + _dot(mix, wout_ref[:, cs])
            yield


    n_proj = 1 + IN_MAIN // MXU_COLS
    gens = [part_stages(p) for p in range(R // P)]
    done = [False] * len(gens)

    def advance(p, n=1):
        for _ in range(n):
            if not done[p]:
                try:
                    next(gens[p])
                except StopIteration:
                    done[p] = True

    n_mix = 6 + units if NS == 1 else 7
    advance(0, n_proj)
    for p in range(len(gens)):
        for _ in range(n_mix):
            advance(p)
            if p + 1 < len(gens):
                advance(p + 1)
            if p > 0:
                advance(p - 1)
    for p in range(len(gens)):
        while not done[p]:
            advance(p)

    pool_o_ref[...] = pz_s[:, T + 1:T + POOL_PAD, :]
    if T >= POOL_PAD:
        pz_s[:, 0:POOL_PAD, :] = pz_s[:, T:T + POOL_PAD, :]


def _ffn_kernel(xp_ref, xs_ref, g_ref, wg_ref, wu_ref, wd_ref, fg_ref, op_ref, os_ref, *, n_prompt_steps, final):
    def part_stages(x_ref, o_ref, rows):
        x = x_ref[rows, :]
        hf = _rms(x, g_ref[...]).astype(BF16)
        yield
        acc = x
        for lo, hi in FFN_CHUNKS:
            gt = _dot(hf, wg_ref[:, lo:hi])
            yield
            up = _dot(hf, wu_ref[:, lo:hi])
            yield
            act = (gt * jax.nn.sigmoid(gt) * up).astype(BF16)
            acc = acc + _dot(act, wd_ref[lo:hi, :])
            yield
        if final:
            acc = _rms(acc, fg_ref[...])
        o_ref[rows, :] = acc

    def ffn(x_ref, o_ref):
        n_rows = x_ref.shape[0]
        gens = [part_stages(x_ref, o_ref, slice(r, r + PART_ROWS)) for r in range(0, n_rows, PART_ROWS)]
        live = list(gens)
        lag = 0
        while live:
            for g in list(live[:lag + 1]):
                try:
                    next(g)
                except StopIteration:
                    live.remove(g)
            lag += 1

    i = pl.program_id(0)
    pl.when(i < n_prompt_steps)(lambda: ffn(xp_ref, op_ref))
    pl.when(i >= n_prompt_steps)(lambda: ffn(xs_ref, os_ref))


def _wspec(shape, layer):
    nd = len(shape)
    return pl.BlockSpec((None,) + tuple(shape), lambda *g, _l=layer, _n=nd: (_l,) + (0,) * _n,
                        pipeline_mode=pl.Buffered(1))


def _mix_call(layer, x2, pool0, gla0, wts, prev, *, nseq, L, NS, T, pos0, state_layer, emit_vn, seed_shapes=()):
    R = NS * T
    n_chunk = L // T
    grid = (nseq // NS, n_chunk)
    (ng, win, wina, poolw, pscale, sgug, sguw, sgub, wa2, ba, glag, wout) = wts
    sl = state_layer
    in_specs = [
        pl.BlockSpec((R, D_MODEL), lambda i, j: (i * n_chunk + j, 0)),
        pl.BlockSpec((None, NS, POOL_BUF, POOL_WIDTH), lambda i, j: (sl, i, 0, 0)),
        pl.BlockSpec((None, NS, GLA_HEADS, GLA_DK, GLA_DV), lambda i, j: (sl, i, 0, 0, 0)),
        _wspec((1, D_MODEL), layer),
        _wspec((D_MODEL, IN_MAIN), layer),
        _wspec((D_MODEL, GLA_RANK_PAD), layer),
        _wspec((POOL_WIDTH, POOL_WIDTH), layer),
        _wspec((1, POOL_WIDTH), layer),
        _wspec((1, SGU_WIDTH), layer),
        _wspec(sguw.shape[1:], layer),
        _wspec(sgub.shape[1:], layer),
        _wspec((GLA_RANK_PAD, GLA_KW), layer),
        _wspec((1, GLA_KW), layer),
        _wspec((1, GLA_DV), layer),
        _wspec((D_MODEL, D_MODEL), layer),
    ]
    operands = [x2, pool0, gla0, ng, win, wina, poolw, pscale, sgug, sguw, sgub, wa2, ba, glag, wout]
    out_specs = [
        pl.BlockSpec((R, D_MODEL), lambda i, j: (i * n_chunk + j, 0)),
        pl.BlockSpec((None, NS, POOL_BUF, POOL_WIDTH), lambda i, j: (layer, i, 0, 0)),
        pl.BlockSpec((None, NS, GLA_HEADS, GLA_DK, GLA_DV), lambda i, j: (layer, i, 0, 0, 0)),
    ]
    out_shape = [
        jax.ShapeDtypeStruct((nseq * L, D_MODEL), F32),
        jax.ShapeDtypeStruct((DEPTH, nseq, POOL_BUF, POOL_WIDTH), F32),
        jax.ShapeDtypeStruct((DEPTH, nseq, GLA_HEADS, GLA_DK, GLA_DV), F32),
    ]
    if emit_vn:
        out_specs.append(pl.BlockSpec((None, R, SGU_WIDTH), lambda i, j: (layer, i * n_chunk + j, 0)))
        out_shape.append(jax.ShapeDtypeStruct((DEPTH, nseq * L, SGU_WIDTH), F32))
    assert len(prev) == len(out_shape) - 1
    n_steps = grid[0] * grid[1]
    for shp in seed_shapes:
        blk = (shp[0], shp[1] // n_steps) + tuple(shp[2:])
        out_specs.append(pl.BlockSpec(blk, lambda i, j, _n=len(shp): (0, i * n_chunk + j) + (0,) * (_n - 2)))
        out_shape.append(jax.ShapeDtypeStruct(shp, F32))
    aliases = {}
    for k, arr in enumerate(prev):
        aliases[len(operands)] = 1 + k
        operands.append(arr)
        in_specs.append(pl.BlockSpec(memory_space=pl.ANY))
    scratch = [
        pltpu.VMEM((R, IN_MAIN), F32),
        pltpu.VMEM((NS, POOL_PAD + T, POOL_WIDTH), F32),
        pltpu.VMEM((R, GLA_KW), F32),
        pltpu.VMEM((R, GLA_KW), F32),
        pltpu.VMEM((R, GLA_KW), F32),
        pltpu.VMEM((GLA_KW, (R // min(GLA_CHUNK, T)) * GLA_DV), F32),
        pltpu.VMEM((R, GLA_VW), F32),
    ]
    outs = pl.pallas_call(
        functools.partial(_mix_kernel, NS=NS, T=T, pos0=pos0, emit_vn=emit_vn, n_prev=len(prev)),
        grid=grid, in_specs=in_specs, out_specs=out_specs, out_shape=out_shape,
        scratch_shapes=scratch, input_output_aliases=aliases,
        compiler_params=pltpu.CompilerParams(
            dimension_semantics=("arbitrary", "arbitrary"), vmem_limit_bytes=VMEM_LIMIT),
        name=f"mix_T{T}",
    )(*operands)
    n_state = len(prev)
    return outs[0], tuple(outs[1:1 + n_state]), tuple(outs[1 + n_state:])


def _ffn_call(layer, xp, xs, wts, final_g, *, TM, final):
    n_p, n_s = xp.shape[0] // TM, xs.shape[0] // TM
    g, wg, wu, wd = wts
    p_idx = lambda i: (jnp.minimum(i, n_p - 1), 0)
    s_idx = lambda i: (jnp.maximum(i - n_p, 0), 0)
    return pl.pallas_call(
        functools.partial(_ffn_kernel, n_prompt_steps=n_p, final=final),
        grid=(n_p + n_s,),
        in_specs=[
            pl.BlockSpec((TM, D_MODEL), p_idx),
            pl.BlockSpec((TM, D_MODEL), s_idx),
            _wspec((1, D_MODEL), layer),
            _wspec((D_MODEL, D_FF), layer),
            _wspec((D_MODEL, D_FF), layer),
            _wspec((D_FF, D_MODEL), layer),
            pl.BlockSpec((1, D_MODEL), lambda i: (0, 0)),
        ],
        out_specs=[pl.BlockSpec((TM, D_MODEL), p_idx), pl.BlockSpec((TM, D_MODEL), s_idx)],
        out_shape=[jax.ShapeDtypeStruct(xp.shape, F32), jax.ShapeDtypeStruct(xs.shape, F32)],
        compiler_params=pltpu.CompilerParams(
            dimension_semantics=("arbitrary",), vmem_limit_bytes=VMEM_LIMIT),
        name="ffn",
    )(xp, xs, g, wg, wu, wd, final_g)


def kernel(x_prompt, x_sample, state_pool, state_gla, attn_norm_g, w_in, pool_w, pool_scale, sgu_norm_g, sgu_ws, sgu_b, gla_wa2, gla_ba, gla_norm_g, w_out, ffn_norm_g, w_gate, w_up, w_down, final_norm_g):
    bp, seq, _ = x_prompt.shape
    bs, dseq, _ = x_sample.shape
    assert seq % PROMPT_TILE == 0 and dseq == 8 and bs % SAMPLE_SEQS == 0

    ng = attn_norm_g[:, None, :]
    win = w_in.astype(BF16)
    wina = jnp.pad(w_in[:, :, IN_MAIN:], ((0, 0), (0, 0), (0, GLA_RANK_PAD - GLA_RANK))).astype(BF16)
    eye_g = jnp.eye(len(POOL_WINDOWS), dtype=F32)
    poolw = jnp.einsum('lgcd,gh->lgchd', pool_w, eye_g).reshape(DEPTH, POOL_WIDTH, POOL_WIDTH).astype(BF16)
    pscale = pool_scale[:, None, :]
    sgug = sgu_norm_g.reshape(DEPTH, 1, SGU_WIDTH)
    sguw_cat = sgu_ws.transpose(0, 2, 1, 3).reshape(DEPTH, SGU_CHUNK, SGU_HEADS * SGU_CHUNK)
    sgub_tile = jnp.repeat(sgu_b.transpose(0, 2, 1), SGU_HEAD_DIM, axis=-1)
    sguw_dec = jnp.repeat(sgu_ws[:, :, :dseq, :dseq].transpose(0, 3, 2, 1), SGU_HEAD_DIM, axis=-1)
    sgub_dec = sgub_tile[:, :dseq]
    wa2 = jnp.pad(gla_wa2, ((0, 0), (0, GLA_RANK_PAD - GLA_RANK), (0, 0))).astype(BF16)
    ba = gla_ba[:, None, :]
    glag = gla_norm_g[:, None, :]
    wout = w_out.astype(BF16)
    fng = ffn_norm_g[:, None, :]
    wg = w_gate.astype(BF16)
    wu = w_up.astype(BF16)
    wd = w_down.astype(BF16)
    fin = final_norm_g[None, :]

    xp = x_prompt.reshape(bp * seq, D_MODEL)
    xs = x_sample.reshape(bs * dseq, D_MODEL)
    pool0_p = jnp.zeros((1, bp, POOL_BUF, POOL_WIDTH), F32)
    gla0_p = jnp.zeros((1, bp, GLA_HEADS, GLA_DK, GLA_DV), F32)

    st_p = (jnp.zeros((DEPTH, bp, POOL_BUF, POOL_WIDTH), F32), jnp.zeros((DEPTH, bp, GLA_HEADS, GLA_DK, GLA_DV), F32))
    seeds_s = ((DEPTH, bs, POOL_BUF, POOL_WIDTH), (DEPTH, bs, GLA_HEADS, GLA_DK, GLA_DV), (DEPTH, bs * dseq, SGU_WIDTH))
    st_s = None
    for l in range(DEPTH):
        common = (ng, win, wina, poolw, pscale, sgug)
        tail = (wa2, ba, glag, wout)
        xp, st_p, seeds = _mix_call(l, xp, pool0_p, gla0_p, common + (sguw_cat, sgub_tile) + tail, st_p,
                                    nseq=bp, L=seq, NS=1, T=PROMPT_TILE, pos0=0, state_layer=0, emit_vn=False,
                                    seed_shapes=seeds_s if l == 0 else ())
        st_s = seeds if l == 0 else st_s
        xs, st_s, _ = _mix_call(l, xs, state_pool, state_gla, common + (sguw_dec, sgub_dec) + tail, st_s,
                                nseq=bs, L=dseq, NS=SAMPLE_SEQS, T=dseq, pos0=PAST_LEN, state_layer=l, emit_vn=True)
        xp, xs = _ffn_call(l, xp, xs, (fng, wg, wu, wd), fin, TM=FFN_ROWS, final=l == DEPTH - 1)

    return (xp.reshape(bp, seq, D_MODEL), xs.reshape(bs, dseq, D_MODEL),
            st_p[0], st_p[1], st_s[0], st_s[1], st_s[2].reshape(DEPTH, bs, dseq, SGU_WIDTH))
```

```python
import functools

import jax
import jax.numpy as jnp
from jax import lax
from jax.experimental import pallas as pl
from jax.experimental.pallas import tpu as pltpu

F32 = jnp.float32
BF16 = jnp.bfloat16

D_MODEL = 1024
DEPTH = 4
POOL_WIDTH = 256
POOL_WINDOWS = (2, 4, 8, 16)
POOL_GROUP_DIM = 64
POOL_BUF = 15
POOL_PAD = 16
SGU_WIDTH = 256
SGU_HEADS = 4
SGU_HEAD_DIM = 64
SGU_CHUNK = 128
GLA_HEADS = 4
GLA_DK = 64
GLA_DV = 128
GLA_KW = GLA_HEADS * GLA_DK
GLA_VW = GLA_HEADS * GLA_DV
GLA_RANK = 16
GLA_RANK_PAD = 128
GLA_GATE_NORM = 16.0
GLA_CHUNK = 64
MXU_COLS = 256
PART_ROWS = 256
SEQ_UNROLL = 8
D_FF = 2816
EPS = 1e-6

OFF_P, OFF_U, OFF_V, OFF_Q, OFF_K, OFF_VG, OFF_G, OFF_A = 0, 256, 512, 768, 1024, 1280, 1792, 2304
IN_MAIN = 2304
FFN_CHUNKS = ((0, 2816),)

PAST_LEN = 16384
PROMPT_TILE = 1024
SAMPLE_SEQS = 32
FFN_ROWS = 512
VMEM_LIMIT = 56 * 1024 * 1024


def _rms(x, g):
    return x * lax.rsqrt(jnp.mean(x * x, axis=-1, keepdims=True) + EPS) * g


def _dot(a, b):
    return jnp.dot(a, b, preferred_element_type=F32)


def _dot_nt(a, b):
    return lax.dot_general(a, b, (((1,), (1,)), ((), ())), preferred_element_type=F32)


def _dot_tn(a, b):
    return lax.dot_general(a, b, (((0,), (0,)), ((), ())), preferred_element_type=F32)


def _split_bf16(x):
    hi = x.astype(BF16)
    lo = (x - hi.astype(F32)).astype(BF16)
    return hi, lo


def _mix_kernel(*refs, NS, T, pos0, emit_vn, n_prev):
    (x_ref, pool0_ref, gla0_ref, ng_ref, win_ref, wina_ref, poolw_ref, pscale_ref,
     sgug_ref, sguw_ref, sgub_ref, wa2_ref, ba_ref, glag_ref, wout_ref) = refs[:15]
    refs = refs[15 + n_prev:]
    xo_ref, pool_o_ref, gla_o_ref = refs[:3]
    vn_o_ref = refs[3] if emit_vn else None
    seed_refs = refs[3 + emit_vn:-7]
    z_s, pz_s, qe_s, ke_s, kd_s, ecol_s, o_s = refs[-7:]
    j = pl.program_id(1)
    R = NS * T
    P = PART_ROWS
    TP = P // NS
    C = min(GLA_CHUNK, T)
    units = P // C
    assert R % P == 0 and (NS == 1 or R == P)

    @pl.when(j == 0)
    def _():
        pz_s[:, 0:1, :] = jnp.zeros((NS, 1, POOL_WIDTH), F32)
        pz_s[:, 1:POOL_PAD, :] = pool0_ref[...]
        gla_o_ref[...] = gla0_ref[...]

    for seed_ref in seed_refs:
        seed_ref[...] = jnp.zeros(seed_ref.shape, F32)

    grp = lax.broadcasted_iota(jnp.int32, (1, 1, POOL_WIDTH), 2) // POOL_GROUP_DIM
    win = jnp.where(grp == 0, 2, jnp.where(grp == 1, 4, jnp.where(grp == 2, 8, 16)))
    r_i = lax.broadcasted_iota(jnp.int32, (SGU_WIDTH, SGU_WIDTH), 0) // SGU_HEAD_DIM
    c_i = lax.broadcasted_iota(jnp.int32, (SGU_WIDTH, SGU_WIDTH), 1) // SGU_HEAD_DIM
    head_ones = (r_i == c_i).astype(BF16)
    lane_head = lax.broadcasted_iota(jnp.int32, (1, SGU_WIDTH), 1) // SGU_HEAD_DIM
    if T >= SGU_CHUNK:
        row = lax.broadcasted_iota(jnp.int32, (SGU_CHUNK, SGU_HEADS * SGU_CHUNK), 0)
        col = lax.broadcasted_iota(jnp.int32, (SGU_CHUNK, SGU_HEADS * SGU_CHUNK), 1)
        wcat = jnp.where((col & (SGU_CHUNK - 1)) <= row, sguw_ref[...], 0.0).astype(BF16)
    rr = lax.broadcasted_iota(jnp.int32, (P, P), 0)
    cc = lax.broadcasted_iota(jnp.int32, (P, P), 1)
    tri_bd = ((rr // C == cc // C) & (rr >= cc)).astype(BF16)
    sel2 = (lax.broadcasted_iota(jnp.int32, (8, 2 * GLA_DV), 0) // 2
            == lax.broadcasted_iota(jnp.int32, (8, 2 * GLA_DV), 1) // GLA_DV).astype(BF16)
    khead = lax.broadcasted_iota(jnp.int32, (1, GLA_KW), 1) // GLA_DK
    vhead = lax.broadcasted_iota(jnp.int32, (1, GLA_VW), 1) // GLA_DV
    arow = lax.broadcasted_iota(jnp.int32, (C, GLA_HEADS * C), 0)
    acol = lax.broadcasted_iota(jnp.int32, (C, GLA_HEADS * C), 1) & (C - 1)
    causal = acol <= arow
    zblk = jnp.zeros((GLA_DK, GLA_DV), F32)

    def gla_unit_stages(r0, e0, read_state, write_state):
        rows = pl.ds(r0, C)
        qe = qe_s[rows, :].astype(BF16)
        ke = ke_s[rows, :]
        kd = kd_s[rows, :]
        vv = z_s[rows, OFF_VG:OFF_VG + GLA_VW]
        ke_bd = jnp.concatenate(
            [jnp.where(khead == hh, ke, 0.0) for hh in range(GLA_HEADS)], axis=0).astype(BF16)
        att_raw = _dot_nt(qe, ke_bd)
        v_bd = jnp.concatenate(
            [jnp.where(vhead == hh, vv, 0.0) for hh in range(GLA_HEADS)], axis=0).astype(BF16)
        kd_st = jnp.concatenate(
            [kd[:, hh * GLA_DK:(hh + 1) * GLA_DK] for hh in range(GLA_HEADS)], axis=0).astype(BF16)
        upd = _dot_tn(kd_st, v_bd)
        yield
        att = jnp.where(causal, att_raw, 0.0).astype(BF16)
        o_intra = _dot(att, v_bd)
        yield
        s_blocks = read_state()
        s_bd = jnp.concatenate(
            [jnp.concatenate([s_blocks[hh] if hc == hh else zblk for hc in range(GLA_HEADS)], axis=1)
             for hh in range(GLA_HEADS)], axis=0)
        o_s[rows, :] = o_intra + _dot(qe, s_bd.astype(BF16))
        ecol = ecol_s[:, pl.ds(e0, GLA_DV)]
        write_state([ecol[hh * GLA_DK:(hh + 1) * GLA_DK] * s_blocks[hh] + upd[:, hh * GLA_DV:(hh + 1) * GLA_DV]
                     for hh in range(GLA_HEADS)])

    def step(gen):
        try:
            next(gen)
        except StopIteration:
            pass

    def part_stages(part):
        r0p = part * P
        prow = slice(r0p, r0p + P)
        t0 = part * TP
        x = x_ref[prow, :]
        h = _rms(x, ng_ref[...]).astype(BF16)
        yield
        for n in range(IN_MAIN // MXU_COLS):
            cs = slice(n * MXU_COLS, (n + 1) * MXU_COLS)
            z_s[prow, cs] = _dot(h, win_ref[:, cs])
            yield
        alow = _dot(h, wina_ref[...]).astype(BF16)

        p3 = z_s[prow, OFF_P:OFF_P + POOL_WIDTH].reshape(NS, TP, POOL_WIDTH)
        pz_s[:, POOL_PAD + t0:POOL_PAD + t0 + TP, :] = p3
        e = pz_s[:, t0:t0 + POOL_PAD + TP, :]
        w2 = e + pltpu.roll(e, 1, 1)
        w4 = w2 + pltpu.roll(w2, 2, 1)
        w8 = w4 + pltpu.roll(w4, 4, 1)
        w16 = w8 + pltpu.roll(w8, 8, 1)
        wsum = jnp.where(grp == 0, w2, jnp.where(grp == 1, w4, jnp.where(grp == 2, w8, w16)))
        wsum = wsum[:, POOL_PAD:POOL_PAD + TP, :]
        pos = pos0 + j * T + t0 + lax.broadcasted_iota(jnp.int32, (1, TP, 1), 1)
        cnt = jnp.minimum(pos + 1, win).astype(F32)
        d = (wsum / cnt - p3).reshape(P, POOL_WIDTH)
        a_out = _dot(d.astype(BF16), poolw_ref[...]) * pscale_ref[...]
        yield

        v = z_s[prow, OFF_V:OFF_V + SGU_WIDTH]
        u = z_s[prow, OFF_U:OFF_U + SGU_WIDTH]
        vsq_hi, vsq_lo = _split_bf16(v * v)
        ss = _dot(vsq_hi, head_ones) + _dot(vsq_lo, head_ones)
        vn = v * lax.rsqrt(ss * (1.0 / SGU_HEAD_DIM) + EPS) * sgug_ref[...]
        if emit_vn:
            vn_o_ref[prow, :] = vn
        yield
        if T >= SGU_CHUNK:
            parts = []
            for c in range(P // SGU_CHUNK):
                vc = vn[c * SGU_CHUNK:(c + 1) * SGU_CHUNK]
                stack = jnp.concatenate(
                    [jnp.where(lane_head == hh, vc, 0.0) for hh in range(SGU_HEADS)], axis=0)
                parts.append(_dot(wcat, stack.astype(BF16)) + sgub_ref[...])
            s_gate = jnp.concatenate(parts, axis=0)
        else:
            vn3 = vn.reshape(NS, T, SGU_WIDTH)
            srow = lax.broadcasted_iota(jnp.int32, (T, SGU_WIDTH), 0)
            s3 = jnp.zeros((NS, T, SGU_WIDTH), F32) + sgub_ref[...][None]
            for jj in range(T):
                coef = jnp.where(srow >= jj, sguw_ref[jj], 0.0)
                s3 = s3 + vn3[:, jj:jj + 1, :] * coef[None]
            s_gate = s3.reshape(P, SGU_WIDTH)
        b_out = u * s_gate
        yield

        xg = _dot(alow, wa2_ref[...]) + ba_ref[...]
        lg = (jnp.minimum(xg, 0.0) - jnp.log(1.0 + jnp.exp(-jnp.abs(xg)))) * (1.0 / GLA_GATE_NORM)
        lg_hi, lg_lo = _split_bf16(lg)
        bcum = _dot(tri_bd, lg_hi) + _dot(tri_bd, lg_lo)
        b3 = bcum.reshape(units, C, GLA_KW)
        btot = jnp.broadcast_to(b3[:, C - 1:C, :], (units, C, GLA_KW)).reshape(P, GLA_KW)
        ke_all = z_s[prow, OFF_K:OFF_K + GLA_KW] * jnp.exp(-bcum)
        qe_s[prow, :] = z_s[prow, OFF_Q:OFF_Q + GLA_KW] * (GLA_DK ** -0.5) * jnp.exp(bcum)
        ke_s[prow, :] = ke_all
        kd_s[prow, :] = ke_all * jnp.exp(btot)
        yield
        etot = jnp.exp(b3[:, C - 1, :])
        et_hi = etot.astype(BF16).astype(F32)
        et_lo = etot - et_hi
        zpad = jnp.zeros((4, GLA_KW), F32)
        for pp in range(units // 2):
            a_rows = jnp.concatenate([et_hi[2 * pp:2 * pp + 1], et_lo[2 * pp:2 * pp + 1],
                                      et_hi[2 * pp + 1:2 * pp + 2], et_lo[2 * pp + 1:2 * pp + 2], zpad], axis=0)
            e0 = (part * units + 2 * pp) * GLA_DV
            ecol_s[:, e0:e0 + 2 * GLA_DV] = _dot_tn(a_rows.astype(BF16), sel2)
        yield

        if NS == 1:
            state = [[gla_o_ref[0, hh] for hh in range(GLA_HEADS)]]
            ug = [gla_unit_stages(r0p + c * C, (part * units + c) * GLA_DV,
                                  lambda: state[0], lambda new: state.__setitem__(0, new)) for c in range(units)]
            step(ug[0])
            step(ug[0])
            for c in range(units):
                if c + 1 < units:
                    step(ug[c + 1])
                step(ug[c])
                if c + 1 < units:
                    step(ug[c + 1])
                yield
            for hh in range(GLA_HEADS):
                gla_o_ref[0, hh] = state[0][hh]
        else:
            assert units == NS and NS % SEQ_UNROLL == 0

            def seq_group(g, carry):
                def unit(sq):
                    def write(new):
                        for hh in range(GLA_HEADS):
                            gla_o_ref[sq, hh] = new[hh]
                    return gla_unit_stages(pl.multiple_of(sq * C, C), pl.multiple_of(sq * GLA_DV, GLA_DV),
                                           lambda: [gla_o_ref[sq, hh] for hh in range(GLA_HEADS)], write)
                ug = [unit(g * SEQ_UNROLL + i) for i in range(SEQ_UNROLL)]
                for _ in range(3):
                    for u_gen in ug:
                        step(u_gen)
                return carry

            lax.fori_loop(0, NS // SEQ_UNROLL, seq_group, 0)

        gate = z_s[prow, OFF_G:OFF_G + GLA_VW]
        o_all = o_s[prow, :]
        c_parts = []
        for hh in range(GLA_HEADS):
            sl = slice(hh * GLA_DV, (hh + 1) * GLA_DV)
            gh = gate[:, sl]
            c_parts.append(_rms(o_all[:, sl], glag_ref[...]) * (gh * jax.nn.sigmoid(gh)))

        mix = jnp.concatenate([a_out, b_out] + c_parts, axis=-1).astype(BF16)
        yield
        for n in range(D_MODEL // MXU_COLS):
            cs = slice(n * MXU_COLS, (n + 1) * MXU_COLS)
            xo_ref[prow, cs] = x[:, cs] + _dot(mix, wout_ref[:, cs])
            yield


    n_proj = 1 + IN_MAIN // MXU_COLS
    gens = [part_stages(p) for p in range(R // P)]
    done = [False] * len(gens)

    def advance(p, n=1):
        for _ in range(n):
            if not done[p]:
                try:
                    next(gens[p])
                except StopIteration:
                    done[p] = True

    n_mix = 6 + units if NS == 1 else 7
    advance(0, n_proj)
    for p in range(len(gens)):
        for _ in range(n_mix):
            advance(p)
            if p + 1 < len(gens):
                advance(p + 1)
            if p > 0:
                advance(p - 1)
    for p in range(len(gens)):
        while not done[p]:
            advance(p)

    pool_o_ref[...] = pz_s[:, T + 1:T + POOL_PAD, :]
    if T >= POOL_PAD:
        pz_s[:, 0:POOL_PAD, :] = pz_s[:, T:T + POOL_PAD, :]


def _ffn_kernel(xp_ref, xs_ref, g_ref, wg_ref, wu_ref, wd_ref, fg_ref, op_ref, os_ref, *, n_prompt_steps, final):
    def part_stages(x_ref, o_ref, rows):
        x = x_ref[rows, :]
        hf = _rms(x, g_ref[...]).astype(BF16)
        yield
        acc = x
        for lo, hi in FFN_CHUNKS:
            gt = _dot(hf, wg_ref[:, lo:hi])
            yield
            up = _dot(hf, wu_ref[:, lo:hi])
            yield
            act = (gt * jax.nn.sigmoid(gt) * up).astype(BF16)
            acc = acc + _dot(act, wd_ref[lo:hi, :])
            yield
        if final:
            acc = _rms(acc, fg_ref[...])
        o_ref[rows, :] = acc

    def ffn(x_ref, o_ref):
        n_rows = x_ref.shape[0]
        gens = [part_stages(x_ref, o_ref, slice(r, r + PART_ROWS)) for r in range(0, n_rows, PART_ROWS)]
        live = list(gens)
        lag = 0
        while live:
            for g in list(live[:lag + 1]):
                try:
                    next(g)
                except StopIteration:
                    live.remove(g)
            lag += 1

    i = pl.program_id(0)
    pl.when(i < n_prompt_steps)(lambda: ffn(xp_ref, op_ref))
    pl.when(i >= n_prompt_steps)(lambda: ffn(xs_ref, os_ref))


def _wspec(shape, layer):
    nd = len(shape)
    return pl.BlockSpec((None,) + tuple(shape), lambda *g, _l=layer, _n=nd: (_l,) + (0,) * _n,
                        pipeline_mode=pl.Buffered(1))


def _mix_call(layer, x2, pool0, gla0, wts, prev, *, nseq, L, NS, T, pos0, state_layer, emit_vn, seed_shapes=()):
    R = NS * T
    n_chunk = L // T
    grid = (nseq // NS, n_chunk)
    (ng, win, wina, poolw, pscale, sgug, sguw, sgub, wa2, ba, glag, wout) = wts
    sl = state_layer
    in_specs = [
        pl.BlockSpec((R, D_MODEL), lambda i, j: (i * n_chunk + j, 0)),
        pl.BlockSpec((None, NS, POOL_BUF, POOL_WIDTH), lambda i, j: (sl, i, 0, 0)),
        pl.BlockSpec((None, NS, GLA_HEADS, GLA_DK, GLA_DV), lambda i, j: (sl, i, 0, 0, 0)),
        _wspec((1, D_MODEL), layer),
        _wspec((D_MODEL, IN_MAIN), layer),
        _wspec((D_MODEL, GLA_RANK_PAD), layer),
        _wspec((POOL_WIDTH, POOL_WIDTH), layer),
        _wspec((1, POOL_WIDTH), layer),
        _wspec((1, SGU_WIDTH), layer),
        _wspec(sguw.shape[1:], layer),
        _wspec(sgub.shape[1:], layer),
        _wspec((GLA_RANK_PAD, GLA_KW), layer),
        _wspec((1, GLA_KW), layer),
        _wspec((1, GLA_DV), layer),
        _wspec((D_MODEL, D_MODEL), layer),
    ]
    operands = [x2, pool0, gla0, ng, win, wina, poolw, pscale, sgug, sguw, sgub, wa2, ba, glag, wout]
    out_specs = [
        pl.BlockSpec((R, D_MODEL), lambda i, j: (i * n_chunk + j, 0)),
        pl.BlockSpec((None, NS, POOL_BUF, POOL_WIDTH), lambda i, j: (layer, i, 0, 0)),
        pl.BlockSpec((None, NS, GLA_HEADS, GLA_DK, GLA_DV), lambda i, j: (layer, i, 0, 0, 0)),
    ]
    out_shape = [
        jax.ShapeDtypeStruct((nseq * L, D_MODEL), F32),
        jax.ShapeDtypeStruct((DEPTH, nseq, POOL_BUF, POOL_WIDTH), F32),
        jax.ShapeDtypeStruct((DEPTH, nseq, GLA_HEADS, GLA_DK, GLA_DV), F32),
    ]
    if emit_vn:
        out_specs.append(pl.BlockSpec((None, R, SGU_WIDTH), lambda i, j: (layer, i * n_chunk + j, 0)))
        out_shape.append(jax.ShapeDtypeStruct((DEPTH, nseq * L, SGU_WIDTH), F32))
    assert len(prev) == len(out_shape) - 1
    n_steps = grid[0] * grid[1]
    for shp in seed_shapes:
        blk = (shp[0], shp[1] // n_steps) + tuple(shp[2:])
        out_specs.append(pl.BlockSpec(blk, lambda i, j, _n=len(shp): (0, i * n_chunk + j) + (0,) * (_n - 2)))
        out_shape.append(jax.ShapeDtypeStruct(shp, F32))
    aliases = {}
    for k, arr in enumerate(prev):
        aliases[len(operands)] = 1 + k
        operands.append(arr)
        in_specs.append(pl.BlockSpec(memory_space=pl.ANY))
    scratch = [
        pltpu.VMEM((R, IN_MAIN), F32),
        pltpu.VMEM((NS, POOL_PAD + T, POOL_WIDTH), F32),
        pltpu.VMEM((R, GLA_KW), F32),
        pltpu.VMEM((R, GLA_KW), F32),
        pltpu.VMEM((R, GLA_KW), F32),
        pltpu.VMEM((GLA_KW, (R // min(GLA_CHUNK, T)) * GLA_DV), F32),
        pltpu.VMEM((R, GLA_VW), F32),
    ]
    outs = pl.pallas_call(
        functools.partial(_mix_kernel, NS=NS, T=T, pos0=pos0, emit_vn=emit_vn, n_prev=len(prev)),
        grid=grid, in_specs=in_specs, out_specs=out_specs, out_shape=out_shape,
        scratch_shapes=scratch, input_output_aliases=aliases,
        compiler_params=pltpu.CompilerParams(
            dimension_semantics=("arbitrary", "arbitrary"), vmem_limit_bytes=VMEM_LIMIT),
        name=f"mix_T{T}",
    )(*operands)
    n_state = len(prev)
    return outs[0], tuple(outs[1:1 + n_state]), tuple(outs[1 + n_state:])


def _ffn_call(layer, xp, xs, wts, final_g, *, TM, final):
    n_p, n_s = xp.shape[0] // TM, xs.shape[0] // TM
    g, wg, wu, wd = wts
    p_idx = lambda i: (jnp.minimum(i, n_p - 1), 0)
    s_idx = lambda i: (jnp.maximum(i - n_p, 0), 0)
    return pl.pallas_call(
        functools.partial(_ffn_kernel, n_prompt_steps=n_p, final=final),
        grid=(n_p + n_s,),
        in_specs=[
            pl.BlockSpec((TM, D_MODEL), p_idx),
            pl.BlockSpec((TM, D_MODEL), s_idx),
            _wspec((1, D_MODEL), layer),
            _wspec((D_MODEL, D_FF), layer),
            _wspec((D_MODEL, D_FF), layer),
            _wspec((D_FF, D_MODEL), layer),
            pl.BlockSpec((1, D_MODEL), lambda i: (0, 0)),
        ],
        out_specs=[pl.BlockSpec((TM, D_MODEL), p_idx), pl.BlockSpec((TM, D_MODEL), s_idx)],
        out_shape=[jax.ShapeDtypeStruct(xp.shape, F32), jax.ShapeDtypeStruct(xs.shape, F32)],
        compiler_params=pltpu.CompilerParams(
            dimension_semantics=("arbitrary",), vmem_limit_bytes=VMEM_LIMIT),
        name="ffn",
    )(xp, xs, g, wg, wu, wd, final_g)


def kernel(x_prompt, x_sample, state_pool, state_gla, attn_norm_g, w_in, pool_w, pool_scale, sgu_norm_g, sgu_ws, sgu_b, gla_wa2, gla_ba, gla_norm_g, w_out, ffn_norm_g, w_gate, w_up, w_down, final_norm_g):
    bp, seq, _ = x_prompt.shape
    bs, dseq, _ = x_sample.shape
    assert seq % PROMPT_TILE == 0 and dseq == 8 and bs % SAMPLE_SEQS == 0

    ng = attn_norm_g[:, None, :]
    win = w_in.astype(BF16)
    wina = jnp.pad(w_in[:, :, IN_MAIN:], ((0, 0), (0, 0), (0, GLA_RANK_PAD - GLA_RANK))).astype(BF16)
    eye_g = jnp.eye(len(POOL_WINDOWS), dtype=F32)
    poolw = jnp.einsum('lgcd,gh->lgchd', pool_w, eye_g).reshape(DEPTH, POOL_WIDTH, POOL_WIDTH).astype(BF16)
    pscale = pool_scale[:, None, :]
    sgug = sgu_norm_g.reshape(DEPTH, 1, SGU_WIDTH)
    sguw_cat = sgu_ws.transpose(0, 2, 1, 3).reshape(DEPTH, SGU_CHUNK, SGU_HEADS * SGU_CHUNK)
    sgub_tile = jnp.repeat(sgu_b.transpose(0, 2, 1), SGU_HEAD_DIM, axis=-1)
    sguw_dec = jnp.repeat(sgu_ws[:, :, :dseq, :dseq].transpose(0, 3, 2, 1), SGU_HEAD_DIM, axis=-1)
    sgub_dec = sgub_tile[:, :dseq]
    wa2 = jnp.pad(gla_wa2, ((0, 0), (0, GLA_RANK_PAD - GLA_RANK), (0, 0))).astype(BF16)
    ba = gla_ba[:, None, :]
    glag = gla_norm_g[:, None, :]
    wout = w_out.astype(BF16)
    fng = ffn_norm_g[:, None, :]
    wg = w_gate.astype(BF16)
    wu = w_up.astype(BF16)
    wd = w_down.astype(BF16)
    fin = final_norm_g[None, :]

    xp = x_prompt.reshape(bp * seq, D_MODEL)
    xs = x_sample.reshape(bs * dseq, D_MODEL)
    pool0_p = jnp.zeros((1, bp, POOL_BUF, POOL_WIDTH), F32)
    gla0_p = jnp.zeros((1, bp, GLA_HEADS, GLA_DK, GLA_DV), F32)

    st_p = (jnp.zeros((DEPTH, bp, POOL_BUF, POOL_WIDTH), F32), jnp.zeros((DEPTH, bp, GLA_HEADS, GLA_DK, GLA_DV), F32))
    seeds_s = ((DEPTH, bs, POOL_BUF, POOL_WIDTH), (DEPTH, bs, GLA_HEADS, GLA_DK, GLA_DV), (DEPTH, bs * dseq, SGU_WIDTH))
    st_s = None
    for l in range(DEPTH):
        common = (ng, win, wina, poolw, pscale, sgug)
        tail = (wa2, ba, glag, wout)
        xp, st_p, seeds = _mix_call(l, xp, pool0_p, gla0_p, common + (sguw_cat, sgub_tile) + tail, st_p,
                                    nseq=bp, L=seq, NS=1, T=PROMPT_TILE, pos0=0, state_layer=0, emit_vn=False,
                                    seed_shapes=seeds_s if l == 0 else ())
        st_s = seeds if l == 0 else st_s
        xs, st_s, _ = _mix_call(l, xs, state_pool, state_gla, common + (sguw_dec, sgub_dec) + tail, st_s,
                                nseq=bs, L=dseq, NS=SAMPLE_SEQS, T=dseq, pos0=PAST_LEN, state_layer=l, emit_vn=True)
        xp, xs = _ffn_call(l, xp, xs, (fng, wg, wu, wd), fin, TM=FFN_ROWS, final=l == DEPTH - 1)

    return (xp.reshape(bp, seq, D_MODEL), xs.reshape(bs, dseq, D_MODEL),
            st_p[0], st_p[1], st_s[0], st_s[1], st_s[2].reshape(DEPTH, bs, dseq, SGU_WIDTH))
```

```python
import functools

import jax
import jax.numpy as jnp
from jax import lax
from jax.experimental import pallas as pl
from jax.experimental.pallas import tpu as pltpu

F32 = jnp.float32
BF16 = jnp.bfloat16

D_MODEL = 1024
DEPTH = 4
POOL_WIDTH = 256
POOL_WINDOWS = (2, 4, 8, 16)
POOL_GROUP_DIM = 64
POOL_BUF = 15
POOL_PAD = 16
SGU_WIDTH = 256
SGU_HEADS = 4
SGU_HEAD_DIM = 64
SGU_CHUNK = 128
GLA_HEADS = 4
GLA_DK = 64
GLA_DV = 128
GLA_KW = GLA_HEADS * GLA_DK
GLA_VW = GLA_HEADS * GLA_DV
GLA_RANK = 16
GLA_RANK_PAD = 128
GLA_GATE_NORM = 16.0
GLA_CHUNK = 64
MXU_COLS = 256
PART_ROWS = 256
SEQ_UNROLL = 8
D_FF = 2816
EPS = 1e-6

OFF_P, OFF_U, OFF_V, OFF_Q, OFF_K, OFF_VG, OFF_G, OFF_A = 0, 256, 512, 768, 1024, 1280, 1792, 2304
IN_MAIN = 2304
FFN_CHUNKS = ((0, 1536), (1536, 2816))

PAST_LEN = 16384
PROMPT_TILE = 1024
SAMPLE_SEQS = 32
FFN_ROWS = 512
VMEM_LIMIT = 56 * 1024 * 1024


def _rms(x, g):
    return x * lax.rsqrt(jnp.mean(x * x, axis=-1, keepdims=True) + EPS) * g


def _dot(a, b):
    return jnp.dot(a, b, preferred_element_type=F32)


def _dot_nt(a, b):
    return lax.dot_general(a, b, (((1,), (1,)), ((), ())), preferred_element_type=F32)


def _dot_tn(a, b):
    return lax.dot_general(a, b, (((0,), (0,)), ((), ())), preferred_element_type=F32)


def _split_bf16(x):
    hi = x.astype(BF16)
    lo = (x - hi.astype(F32)).astype(BF16)
    return hi, lo


def _mix_kernel(*refs, NS, T, pos0, emit_vn, n_prev, n_cast):
    (x_ref, pool0_ref, gla0_ref, ng_ref, win_ref, wina_ref, poolw_ref, pscale_ref,
     sgug_ref, sguw_ref, sgub_ref, wa2_ref, ba_ref, glag_ref, wout_ref) = refs[:15]
    cast_in = refs[15:15 + n_cast]
    refs = refs[15 + n_cast + n_prev:]
    xo_ref, pool_o_ref, gla_o_ref = refs[:3]
    vn_o_ref = refs[3] if emit_vn else None
    seed_refs = refs[3 + emit_vn:len(refs) - 7 - n_cast]
    cast_out = refs[len(refs) - 7 - n_cast:len(refs) - 7]
    z_s, pz_s, qe_s, ke_s, kd_s, ecol_s, o_s = refs[-7:]
    j = pl.program_id(1)
    R = NS * T
    P = PART_ROWS
    TP = P // NS
    C = min(GLA_CHUNK, T)
    units = P // C
    assert R % P == 0 and (NS == 1 or R == P)

    @pl.when(j == 0)
    def _():
        pz_s[:, 0:1, :] = jnp.zeros((NS, 1, POOL_WIDTH), F32)
        pz_s[:, 1:POOL_PAD, :] = pool0_ref[...]
        gla_o_ref[...] = gla0_ref[...]

    for seed_ref in seed_refs:
        seed_ref[...] = jnp.zeros(seed_ref.shape, F32)

    grp = lax.broadcasted_iota(jnp.int32, (1, 1, POOL_WIDTH), 2) // POOL_GROUP_DIM
    win = jnp.where(grp == 0, 2, jnp.where(grp == 1, 4, jnp.where(grp == 2, 8, 16)))
    r_i = lax.broadcasted_iota(jnp.int32, (SGU_WIDTH, SGU_WIDTH), 0) // SGU_HEAD_DIM
    c_i = lax.broadcasted_iota(jnp.int32, (SGU_WIDTH, SGU_WIDTH), 1) // SGU_HEAD_DIM
    head_ones = (r_i == c_i).astype(BF16)
    lane_head = lax.broadcasted_iota(jnp.int32, (1, SGU_WIDTH), 1) // SGU_HEAD_DIM
    if T >= SGU_CHUNK:
        row = lax.broadcasted_iota(jnp.int32, (SGU_CHUNK, SGU_HEADS * SGU_CHUNK), 0)
        col = lax.broadcasted_iota(jnp.int32, (SGU_CHUNK, SGU_HEADS * SGU_CHUNK), 1)
        wcat = jnp.where((col & (SGU_CHUNK - 1)) <= row, sguw_ref[...], 0.0).astype(BF16)
    rr = lax.broadcasted_iota(jnp.int32, (P, P), 0)
    cc = lax.broadcasted_iota(jnp.int32, (P, P), 1)
    tri_bd = ((rr // C == cc // C) & (rr >= cc)).astype(BF16)
    sel2 = (lax.broadcasted_iota(jnp.int32, (8, 2 * GLA_DV), 0) // 2
            == lax.broadcasted_iota(jnp.int32, (8, 2 * GLA_DV), 1) // GLA_DV).astype(BF16)
    khead = lax.broadcasted_iota(jnp.int32, (1, GLA_KW), 1) // GLA_DK
    vhead = lax.broadcasted_iota(jnp.int32, (1, GLA_VW), 1) // GLA_DV
    arow = lax.broadcasted_iota(jnp.int32, (C, GLA_HEADS * C), 0)
    acol = lax.broadcasted_iota(jnp.int32, (C, GLA_HEADS * C), 1) & (C - 1)
    causal = acol <= arow
    zblk = jnp.zeros((GLA_DK, GLA_DV), F32)

    def gla_unit_stages(r0, e0, read_state, write_state):
        rows = pl.ds(r0, C)
        qe = qe_s[rows, :].astype(BF16)
        ke = ke_s[rows, :]
        kd = kd_s[rows, :]
        vv = z_s[rows, OFF_VG:OFF_VG + GLA_VW]
        ke_bd = jnp.concatenate(
            [jnp.where(khead == hh, ke, 0.0) for hh in range(GLA_HEADS)], axis=0).astype(BF16)
        att_raw = _dot_nt(qe, ke_bd)
        v_bd = jnp.concatenate(
            [jnp.where(vhead == hh, vv, 0.0) for hh in range(GLA_HEADS)], axis=0).astype(BF16)
        kd_st = jnp.concatenate(
            [kd[:, hh * GLA_DK:(hh + 1) * GLA_DK] for hh in range(GLA_HEADS)], axis=0).astype(BF16)
        upd = _dot_tn(kd_st, v_bd)
        yield
        att = jnp.where(causal, att_raw, 0.0).astype(BF16)
        o_intra = _dot(att, v_bd)
        yield
        s_blocks = read_state()
        s_bd = jnp.concatenate(
            [jnp.concatenate([s_blocks[hh] if hc == hh else zblk for hc in range(GLA_HEADS)], axis=1)
             for hh in range(GLA_HEADS)], axis=0)
        o_s[rows, :] = o_intra + _dot(qe, s_bd.astype(BF16))
        ecol = ecol_s[:, pl.ds(e0, GLA_DV)]
        write_state([ecol[hh * GLA_DK:(hh + 1) * GLA_DK] * s_blocks[hh] + upd[:, hh * GLA_DV:(hh + 1) * GLA_DV]
                     for hh in range(GLA_HEADS)])

    def step(gen):
        try:
            next(gen)
        except StopIteration:
            pass

    def part_stages(part):
        r0p = part * P
        prow = slice(r0p, r0p + P)
        t0 = part * TP
        x = x_ref[prow, :]
        h = _rms(x, ng_ref[...]).astype(BF16)
        yield
        for n in range(IN_MAIN // MXU_COLS):
            cs = slice(n * MXU_COLS, (n + 1) * MXU_COLS)
            z_s[prow, cs] = _dot(h, win_ref[:, cs])
            yield
        alow = _dot(h, wina_ref[...]).astype(BF16)

        p3 = z_s[prow, OFF_P:OFF_P + POOL_WIDTH].reshape(NS, TP, POOL_WIDTH)
        pz_s[:, POOL_PAD + t0:POOL_PAD + t0 + TP, :] = p3
        e = pz_s[:, t0:t0 + POOL_PAD + TP, :]
        w2 = e + pltpu.roll(e, 1, 1)
        w4 = w2 + pltpu.roll(w2, 2, 1)
        w8 = w4 + pltpu.roll(w4, 4, 1)
        w16 = w8 + pltpu.roll(w8, 8, 1)
        wsum = jnp.where(grp == 0, w2, jnp.where(grp == 1, w4, jnp.where(grp == 2, w8, w16)))
        wsum = wsum[:, POOL_PAD:POOL_PAD + TP, :]
        pos = pos0 + j * T + t0 + lax.broadcasted_iota(jnp.int32, (1, TP, 1), 1)
        cnt = jnp.minimum(pos + 1, win).astype(F32)
        d = (wsum / cnt - p3).reshape(P, POOL_WIDTH)
        a_out = _dot(d.astype(BF16), poolw_ref[...]) * pscale_ref[...]
        yield

        v = z_s[prow, OFF_V:OFF_V + SGU_WIDTH]
        u = z_s[prow, OFF_U:OFF_U + SGU_WIDTH]
        vsq_hi, vsq_lo = _split_bf16(v * v)
        ss = _dot(vsq_hi, head_ones) + _dot(vsq_lo, head_ones)
        vn = v * lax.rsqrt(ss * (1.0 / SGU_HEAD_DIM) + EPS) * sgug_ref[...]
        if emit_vn:
            vn_o_ref[prow, :] = vn
        yield
        if T >= SGU_CHUNK:
            parts = []
            for c in range(P // SGU_CHUNK):
                vc = vn[c * SGU_CHUNK:(c + 1) * SGU_CHUNK]
                stack = jnp.concatenate(
                    [jnp.where(lane_head == hh, vc, 0.0) for hh in range(SGU_HEADS)], axis=0)
                parts.append(_dot(wcat, stack.astype(BF16)) + sgub_ref[...])
            s_gate = jnp.concatenate(parts, axis=0)
        else:
            vn3 = vn.reshape(NS, T, SGU_WIDTH)
            srow = lax.broadcasted_iota(jnp.int32, (T, SGU_WIDTH), 0)
            s3 = jnp.zeros((NS, T, SGU_WIDTH), F32) + sgub_ref[...][None]
            for jj in range(T):
                coef = jnp.where(srow >= jj, sguw_ref[jj], 0.0)
                s3 = s3 + vn3[:, jj:jj + 1, :] * coef[None]
            s_gate = s3.reshape(P, SGU_WIDTH)
        b_out = u * s_gate
        yield

        xg = _dot(alow, wa2_ref[...]) + ba_ref[...]
        lg = (jnp.minimum(xg, 0.0) - jnp.log(1.0 + jnp.exp(-jnp.abs(xg)))) * (1.0 / GLA_GATE_NORM)
        lg_hi, lg_lo = _split_bf16(lg)
        bcum = _dot(tri_bd, lg_hi) + _dot(tri_bd, lg_lo)
        b3 = bcum.reshape(units, C, GLA_KW)
        btot = jnp.broadcast_to(b3[:, C - 1:C, :], (units, C, GLA_KW)).reshape(P, GLA_KW)
        ke_all = z_s[prow, OFF_K:OFF_K + GLA_KW] * jnp.exp(-bcum)
        qe_s[prow, :] = z_s[prow, OFF_Q:OFF_Q + GLA_KW] * (GLA_DK ** -0.5) * jnp.exp(bcum)
        ke_s[prow, :] = ke_all
        kd_s[prow, :] = ke_all * jnp.exp(btot)
        yield
        etot = jnp.exp(b3[:, C - 1, :])
        et_hi = etot.astype(BF16).astype(F32)
        et_lo = etot - et_hi
        zpad = jnp.zeros((4, GLA_KW), F32)
        for pp in range(units // 2):
            a_rows = jnp.concatenate([et_hi[2 * pp:2 * pp + 1], et_lo[2 * pp:2 * pp + 1],
                                      et_hi[2 * pp + 1:2 * pp + 2], et_lo[2 * pp + 1:2 * pp + 2], zpad], axis=0)
            e0 = (part * units + 2 * pp) * GLA_DV
            ecol_s[:, e0:e0 + 2 * GLA_DV] = _dot_tn(a_rows.astype(BF16), sel2)
        yield

        if NS == 1:
            state = [[gla_o_ref[0, hh] for hh in range(GLA_HEADS)]]
            ug = [gla_unit_stages(r0p + c * C, (part * units + c) * GLA_DV,
                                  lambda: state[0], lambda new: state.__setitem__(0, new)) for c in range(units)]
            step(ug[0])
            step(ug[0])
            for c in range(units):
                if c + 1 < units:
                    step(ug[c + 1])
                step(ug[c])
                if c + 1 < units:
                    step(ug[c + 1])
                yield
            for hh in range(GLA_HEADS):
                gla_o_ref[0, hh] = state[0][hh]
        else:
            assert units == NS and NS % SEQ_UNROLL == 0

            def seq_group(g, carry):
                def unit(sq):
                    def write(new):
                        for hh in range(GLA_HEADS):
                            gla_o_ref[sq, hh] = new[hh]
                    return gla_unit_stages(pl.multiple_of(sq * C, C), pl.multiple_of(sq * GLA_DV, GLA_DV),
                                           lambda: [gla_o_ref[sq, hh] for hh in range(GLA_HEADS)], write)
                ug = [unit(g * SEQ_UNROLL + i) for i in range(SEQ_UNROLL)]
                for _ in range(3):
                    for u_gen in ug:
                        step(u_gen)
                return carry

            lax.fori_loop(0, NS // SEQ_UNROLL, seq_group, 0)

        gate = z_s[prow, OFF_G:OFF_G + GLA_VW]
        o_all = o_s[prow, :]
        c_parts = []
        for hh in range(GLA_HEADS):
            sl = slice(hh * GLA_DV, (hh + 1) * GLA_DV)
            gh = gate[:, sl]
            c_parts.append(_rms(o_all[:, sl], glag_ref[...]) * (gh * jax.nn.sigmoid(gh)))

        mix = jnp.concatenate([a_out, b_out] + c_parts, axis=-1).astype(BF16)
        yield
        for n in range(D_MODEL // MXU_COLS):
            cs = slice(n * MXU_COLS, (n + 1) * MXU_COLS)
            xo_ref[prow, cs] = x[:, cs] + _dot(mix, wout_ref[:, cs])
            yield


    n_proj = 1 + IN_MAIN // MXU_COLS
    gens = [part_stages(p) for p in range(R // P)]
    done = [False] * len(gens)

    def advance(p, n=1):
        for _ in range(n):
            if not done[p]:
                try:
                    next(gens[p])
                except StopIteration:
                    done[p] = True

    n_mix = 6 + units if NS == 1 else 7
    n_head = 2
    advance(0, n_head)
    for src, dst in zip(cast_in, cast_out):
        dst[...] = src[...].astype(BF16)
        advance(0, 2)
        n_head += 2
    assert n_head <= n_proj
    advance(0, n_proj - n_head)
    for p in range(len(gens)):
        for _ in range(n_mix):
            advance(p)
            if p + 1 < len(gens):
                advance(p + 1)
            if p > 0:
                advance(p - 1)
    for p in range(len(gens)):
        while not done[p]:
            advance(p)

    pool_o_ref[...] = pz_s[:, T + 1:T + POOL_PAD, :]
    if T >= POOL_PAD:
        pz_s[:, 0:POOL_PAD, :] = pz_s[:, T:T + POOL_PAD, :]


def _ffn_kernel(xp_ref, xs_ref, g_ref, wg_ref, wu_ref, wd_ref, fg_ref, op_ref, os_ref, *, n_prompt_steps, final):
    def part_stages(x_ref, o_ref, rows):
        x = x_ref[rows, :]
        hf = _rms(x, g_ref[...]).astype(BF16)
        yield
        acc = x
        for lo, hi in FFN_CHUNKS:
            gt = _dot(hf, wg_ref[:, lo:hi])
            yield
            up = _dot(hf, wu_ref[:, lo:hi])
            yield
            act = (gt * jax.nn.sigmoid(gt) * up).astype(BF16)
            acc = acc + _dot(act, wd_ref[lo:hi, :])
            yield
        if final:
            acc = _rms(acc, fg_ref[...])
        o_ref[rows, :] = acc

    def ffn(x_ref, o_ref):
        n_rows = x_ref.shape[0]
        gens = [part_stages(x_ref, o_ref, slice(r, r + PART_ROWS)) for r in range(0, n_rows, PART_ROWS)]
        live = list(gens)
        lag = 0
        while live:
            for g in list(live[:lag + 1]):
                try:
                    next(g)
                except StopIteration:
                    live.remove(g)
            lag += 1

    i = pl.program_id(0)
    pl.when(i < n_prompt_steps)(lambda: ffn(xp_ref, op_ref))
    pl.when(i >= n_prompt_steps)(lambda: ffn(xs_ref, os_ref))


def _wspec(shape, layer):
    nd = len(shape)
    return pl.BlockSpec((None,) + tuple(shape), lambda *g, _l=layer, _n=nd: (_l,) + (0,) * _n,
                        pipeline_mode=pl.Buffered(1))


def _mix_call(layer, x2, pool0, gla0, wts, prev, *, nseq, L, NS, T, pos0, state_layer, emit_vn, seed_shapes=(), cast_srcs=()):
    R = NS * T
    n_chunk = L // T
    grid = (nseq // NS, n_chunk)
    (ng, win, wina, poolw, pscale, sgug, sguw, sgub, wa2, ba, glag, wout) = wts
    sl = state_layer
    in_specs = [
        pl.BlockSpec((R, D_MODEL), lambda i, j: (i * n_chunk + j, 0)),
        pl.BlockSpec((None, NS, POOL_BUF, POOL_WIDTH), lambda i, j: (sl, i, 0, 0)),
        pl.BlockSpec((None, NS, GLA_HEADS, GLA_DK, GLA_DV), lambda i, j: (sl, i, 0, 0, 0)),
        _wspec((1, D_MODEL), layer),
        _wspec((D_MODEL, IN_MAIN), layer),
        _wspec((D_MODEL, GLA_RANK_PAD), layer),
        _wspec((POOL_WIDTH, POOL_WIDTH), layer),
        _wspec((1, POOL_WIDTH), layer),
        _wspec((1, SGU_WIDTH), layer),
        _wspec(sguw.shape[1:], layer),
        _wspec(sgub.shape[1:], layer),
        _wspec((GLA_RANK_PAD, GLA_KW), layer),
        _wspec((1, GLA_KW), layer),
        _wspec((1, GLA_DV), layer),
        _wspec((D_MODEL, D_MODEL), layer),
    ]
    operands = [x2, pool0, gla0, ng, win, wina, poolw, pscale, sgug, sguw, sgub, wa2, ba, glag, wout]
    out_specs = [
        pl.BlockSpec((R, D_MODEL), lambda i, j: (i * n_chunk + j, 0)),
        pl.BlockSpec((None, NS, POOL_BUF, POOL_WIDTH), lambda i, j: (layer, i, 0, 0)),
        pl.BlockSpec((None, NS, GLA_HEADS, GLA_DK, GLA_DV), lambda i, j: (layer, i, 0, 0, 0)),
    ]
    out_shape = [
        jax.ShapeDtypeStruct((nseq * L, D_MODEL), F32),
        jax.ShapeDtypeStruct((DEPTH, nseq, POOL_BUF, POOL_WIDTH), F32),
        jax.ShapeDtypeStruct((DEPTH, nseq, GLA_HEADS, GLA_DK, GLA_DV), F32),
    ]
    if emit_vn:
        out_specs.append(pl.BlockSpec((None, R, SGU_WIDTH), lambda i, j: (layer, i * n_chunk + j, 0)))
        out_shape.append(jax.ShapeDtypeStruct((DEPTH, nseq * L, SGU_WIDTH), F32))
    assert len(prev) == len(out_shape) - 1
    n_steps = grid[0] * grid[1]
    for shp in seed_shapes:
        blk = (shp[0], shp[1] // n_steps) + tuple(shp[2:])
        out_specs.append(pl.BlockSpec(blk, lambda i, j, _n=len(shp): (0, i * n_chunk + j) + (0,) * (_n - 2)))
        out_shape.append(jax.ShapeDtypeStruct(shp, F32))
    for w in cast_srcs:
        rows, cols = w.shape[1] // n_steps, w.shape[2]
        operands.append(w)
        in_specs.append(pl.BlockSpec((None, rows, cols), lambda i, j: (layer, i * n_chunk + j, 0)))
        out_specs.append(pl.BlockSpec((rows, cols), lambda i, j: (i * n_chunk + j, 0)))
        out_shape.append(jax.ShapeDtypeStruct(w.shape[1:], BF16))
    aliases = {}
    for k, arr in enumerate(prev):
        aliases[len(operands)] = 1 + k
        operands.append(arr)
        in_specs.append(pl.BlockSpec(memory_space=pl.ANY))
    scratch = [
        pltpu.VMEM((R, IN_MAIN), F32),
        pltpu.VMEM((NS, POOL_PAD + T, POOL_WIDTH), F32),
        pltpu.VMEM((R, GLA_KW), F32),
        pltpu.VMEM((R, GLA_KW), F32),
        pltpu.VMEM((R, GLA_KW), F32),
        pltpu.VMEM((GLA_KW, (R // min(GLA_CHUNK, T)) * GLA_DV), F32),
        pltpu.VMEM((R, GLA_VW), F32),
    ]
    outs = pl.pallas_call(
        functools.partial(_mix_kernel, NS=NS, T=T, pos0=pos0, emit_vn=emit_vn, n_prev=len(prev),
                          n_cast=len(cast_srcs)),
        grid=grid, in_specs=in_specs, out_specs=out_specs, out_shape=out_shape,
        scratch_shapes=scratch, input_output_aliases=aliases,
        compiler_params=pltpu.CompilerParams(
            dimension_semantics=("arbitrary", "arbitrary"), vmem_limit_bytes=VMEM_LIMIT),
        name=f"mix_T{T}",
    )(*operands)
    n_state, n_seed = len(prev), len(seed_shapes)
    return (outs[0], tuple(outs[1:1 + n_state]), tuple(outs[1 + n_state:1 + n_state + n_seed]),
            tuple(outs[1 + n_state + n_seed:]))


def _ffn_call(layer, xp, xs, wts, final_g, *, TM, final):
    n_p, n_s = xp.shape[0] // TM, xs.shape[0] // TM
    g, wg, wu, wd = wts
    p_idx = lambda i: (jnp.minimum(i, n_p - 1), 0)
    s_idx = lambda i: (jnp.maximum(i - n_p, 0), 0)
    return pl.pallas_call(
        functools.partial(_ffn_kernel, n_prompt_steps=n_p, final=final),
        grid=(n_p + n_s,),
        in_specs=[
            pl.BlockSpec((TM, D_MODEL), p_idx),
            pl.BlockSpec((TM, D_MODEL), s_idx),
            _wspec((1, D_MODEL), layer),
            pl.BlockSpec((D_MODEL, D_FF), lambda i: (0, 0), pipeline_mode=pl.Buffered(1)),
            pl.BlockSpec((D_MODEL, D_FF), lambda i: (0, 0), pipeline_mode=pl.Buffered(1)),
            pl.BlockSpec((D_FF, D_MODEL), lambda i: (0, 0), pipeline_mode=pl.Buffered(1)),
            pl.BlockSpec((1, D_MODEL), lambda i: (0, 0)),
        ],
        out_specs=[pl.BlockSpec((TM, D_MODEL), p_idx), pl.BlockSpec((TM, D_MODEL), s_idx)],
        out_shape=[jax.ShapeDtypeStruct(xp.shape, F32), jax.ShapeDtypeStruct(xs.shape, F32)],
        compiler_params=pltpu.CompilerParams(
            dimension_semantics=("arbitrary",), vmem_limit_bytes=VMEM_LIMIT),
        name="ffn",
    )(xp, xs, g, wg, wu, wd, final_g)


def kernel(x_prompt, x_sample, state_pool, state_gla, attn_norm_g, w_in, pool_w, pool_scale, sgu_norm_g, sgu_ws, sgu_b, gla_wa2, gla_ba, gla_norm_g, w_out, ffn_norm_g, w_gate, w_up, w_down, final_norm_g):
    bp, seq, _ = x_prompt.shape
    bs, dseq, _ = x_sample.shape
    assert seq % PROMPT_TILE == 0 and dseq == 8 and bs % SAMPLE_SEQS == 0

    ng = attn_norm_g[:, None, :]
    win = w_in.astype(BF16)
    wina = jnp.pad(w_in[:, :, IN_MAIN:], ((0, 0), (0, 0), (0, GLA_RANK_PAD - GLA_RANK))).astype(BF16)
    eye_g = jnp.eye(len(POOL_WINDOWS), dtype=F32)
    poolw = jnp.einsum('lgcd,gh->lgchd', pool_w, eye_g).reshape(DEPTH, POOL_WIDTH, POOL_WIDTH).astype(BF16)
    pscale = pool_scale[:, None, :]
    sgug = sgu_norm_g.reshape(DEPTH, 1, SGU_WIDTH)
    sguw_cat = sgu_ws.transpose(0, 2, 1, 3).reshape(DEPTH, SGU_CHUNK, SGU_HEADS * SGU_CHUNK)
    sgub_tile = jnp.repeat(sgu_b.transpose(0, 2, 1), SGU_HEAD_DIM, axis=-1)
    sguw_dec = jnp.repeat(sgu_ws[:, :, :dseq, :dseq].transpose(0, 3, 2, 1), SGU_HEAD_DIM, axis=-1)
    sgub_dec = sgub_tile[:, :dseq]
    wa2 = jnp.pad(gla_wa2, ((0, 0), (0, GLA_RANK_PAD - GLA_RANK), (0, 0))).astype(BF16)
    ba = gla_ba[:, None, :]
    glag = gla_norm_g[:, None, :]
    wout = w_out.astype(BF16)
    fng = ffn_norm_g[:, None, :]
    fin = final_norm_g[None, :]

    xp = x_prompt.reshape(bp * seq, D_MODEL)
    xs = x_sample.reshape(bs * dseq, D_MODEL)
    pool0_p = jnp.zeros((1, bp, POOL_BUF, POOL_WIDTH), F32)
    gla0_p = jnp.zeros((1, bp, GLA_HEADS, GLA_DK, GLA_DV), F32)

    st_p = (jnp.zeros((DEPTH, bp, POOL_BUF, POOL_WIDTH), F32), jnp.zeros((DEPTH, bp, GLA_HEADS, GLA_DK, GLA_DV), F32))
    seeds_s = ((DEPTH, bs, POOL_BUF, POOL_WIDTH), (DEPTH, bs, GLA_HEADS, GLA_DK, GLA_DV), (DEPTH, bs * dseq, SGU_WIDTH))
    st_s = None
    for l in range(DEPTH):
        common = (ng, win, wina, poolw, pscale, sgug)
        tail = (wa2, ba, glag, wout)
        xp, st_p, seeds, (wg, wu, wd) = _mix_call(l, xp, pool0_p, gla0_p, common + (sguw_cat, sgub_tile) + tail, st_p,
                                    nseq=bp, L=seq, NS=1, T=PROMPT_TILE, pos0=0, state_layer=0, emit_vn=False,
                                    seed_shapes=seeds_s if l == 0 else (),
                                    cast_srcs=(w_gate, w_up, w_down))
        st_s = seeds if l == 0 else st_s
        xs, st_s, _, _ = _mix_call(l, xs, state_pool, state_gla, common + (sguw_dec, sgub_dec) + tail, st_s,
                                nseq=bs, L=dseq, NS=SAMPLE_SEQS, T=dseq, pos0=PAST_LEN, state_layer=l, emit_vn=True)
        xp, xs = _ffn_call(l, xp, xs, (fng, wg, wu, wd), fin, TM=FFN_ROWS, final=l == DEPTH - 1)

    return (xp.reshape(bp, seq, D_MODEL), xs.reshape(bs, dseq, D_MODEL),
            st_p[0], st_p[1], st_s[0], st_s[1], st_s[2].reshape(DEPTH, bs, dseq, SGU_WIDTH))
```

```python
import functools

import jax
import jax.numpy as jnp
from jax import lax
from jax.experimental import pallas as pl
from jax.experimental.pallas import tpu as pltpu

F32 = jnp.float32
BF16 = jnp.bfloat16

D_MODEL = 1024
DEPTH = 4
POOL_WIDTH = 256
POOL_WINDOWS = (2, 4, 8, 16)
POOL_GROUP_DIM = 64
POOL_BUF = 15
POOL_PAD = 16
SGU_WIDTH = 256
SGU_HEADS = 4
SGU_HEAD_DIM = 64
SGU_CHUNK = 128
GLA_HEADS = 4
GLA_DK = 64
GLA_DV = 128
GLA_KW = GLA_HEADS * GLA_DK
GLA_VW = GLA_HEADS * GLA_DV
GLA_RANK = 16
GLA_RANK_PAD = 128
GLA_GATE_NORM = 16.0
GLA_CHUNK = 64
MXU_COLS = 256
PART_ROWS = 256
SEQ_UNROLL = 8
D_FF = 2816
EPS = 1e-6

OFF_P, OFF_U, OFF_V, OFF_Q, OFF_K, OFF_VG, OFF_G, OFF_A = 0, 256, 512, 768, 1024, 1280, 1792, 2304
IN_MAIN = 2304
FFN_CHUNKS = ((0, 1536), (1536, 2816))

PAST_LEN = 16384
PROMPT_TILE = 1024
SAMPLE_SEQS = 32
FFN_ROWS = 512
VMEM_LIMIT = 56 * 1024 * 1024


def _rms(x, g):
    return x * lax.rsqrt(jnp.mean(x * x, axis=-1, keepdims=True) + EPS) * g


def _dot(a, b):
    return jnp.dot(a, b, preferred_element_type=F32)


def _dot_nt(a, b):
    return lax.dot_general(a, b, (((1,), (1,)), ((), ())), preferred_element_type=F32)


def _dot_tn(a, b):
    return lax.dot_general(a, b, (((0,), (0,)), ((), ())), preferred_element_type=F32)


def _split_bf16(x):
    hi = x.astype(BF16)
    lo = (x - hi.astype(F32)).astype(BF16)
    return hi, lo


def _mix_kernel(*refs, NS, T, pos0, emit_vn, n_prev, n_cast, cast_win):
    (x_ref, pool0_ref, gla0_ref, ng_ref, win_ref, wina_ref, poolw_ref, pscale_ref,
     sgug_ref, sguw_ref, sgub_ref, wa2_ref, ba_ref, glag_ref, wout_ref) = refs[:15]
    cast_in = refs[15:15 + n_cast]
    wint_ref = refs[15 + n_cast] if cast_win else None
    refs = refs[15 + n_cast + cast_win + n_prev:]
    xo_ref, pool_o_ref, gla_o_ref = refs[:3]
    vn_o_ref = refs[3] if emit_vn else None
    n_tail = 7 + cast_win
    seed_refs = refs[3 + emit_vn:len(refs) - n_tail - n_cast]
    cast_out = refs[len(refs) - n_tail - n_cast:len(refs) - n_tail]
    win_next_ref = refs[len(refs) - n_tail] if cast_win else None
    z_s, pz_s, qe_s, ke_s, kd_s, ecol_s, o_s = refs[-7:]
    j = pl.program_id(1)
    R = NS * T
    P = PART_ROWS
    TP = P // NS
    C = min(GLA_CHUNK, T)
    units = P // C
    assert R % P == 0 and (NS == 1 or R == P)

    @pl.when(j == 0)
    def _():
        pz_s[:, 0:1, :] = jnp.zeros((NS, 1, POOL_WIDTH), F32)
        pz_s[:, 1:POOL_PAD, :] = pool0_ref[...]
        gla_o_ref[...] = gla0_ref[...]

    for seed_ref in seed_refs:
        seed_ref[...] = jnp.zeros(seed_ref.shape, F32)
    if cast_win:
        win_next_ref[...] = wint_ref[...].T.astype(BF16)

    grp = lax.broadcasted_iota(jnp.int32, (1, 1, POOL_WIDTH), 2) // POOL_GROUP_DIM
    win = jnp.where(grp == 0, 2, jnp.where(grp == 1, 4, jnp.where(grp == 2, 8, 16)))
    r_i = lax.broadcasted_iota(jnp.int32, (SGU_WIDTH, SGU_WIDTH), 0) // SGU_HEAD_DIM
    c_i = lax.broadcasted_iota(jnp.int32, (SGU_WIDTH, SGU_WIDTH), 1) // SGU_HEAD_DIM
    head_ones = (r_i == c_i).astype(BF16)
    lane_head = lax.broadcasted_iota(jnp.int32, (1, SGU_WIDTH), 1) // SGU_HEAD_DIM
    if T >= SGU_CHUNK:
        row = lax.broadcasted_iota(jnp.int32, (SGU_CHUNK, SGU_HEADS * SGU_CHUNK), 0)
        col = lax.broadcasted_iota(jnp.int32, (SGU_CHUNK, SGU_HEADS * SGU_CHUNK), 1)
        wcat = jnp.where((col & (SGU_CHUNK - 1)) <= row, sguw_ref[...], 0.0).astype(BF16)
    rr = lax.broadcasted_iota(jnp.int32, (P, P), 0)
    cc = lax.broadcasted_iota(jnp.int32, (P, P), 1)
    tri_bd = ((rr // C == cc // C) & (rr >= cc)).astype(BF16)
    sel2 = (lax.broadcasted_iota(jnp.int32, (8, 2 * GLA_DV), 0) // 2
            == lax.broadcasted_iota(jnp.int32, (8, 2 * GLA_DV), 1) // GLA_DV).astype(BF16)
    khead = lax.broadcasted_iota(jnp.int32, (1, GLA_KW), 1) // GLA_DK
    vhead = lax.broadcasted_iota(jnp.int32, (1, GLA_VW), 1) // GLA_DV
    arow = lax.broadcasted_iota(jnp.int32, (C, GLA_HEADS * C), 0)
    acol = lax.broadcasted_iota(jnp.int32, (C, GLA_HEADS * C), 1) & (C - 1)
    causal = acol <= arow
    zblk = jnp.zeros((GLA_DK, GLA_DV), F32)

    def gla_unit_stages(r0, e0, read_state, write_state):
        rows = pl.ds(r0, C)
        qe = qe_s[rows, :].astype(BF16)
        ke = ke_s[rows, :]
        kd = kd_s[rows, :]
        vv = z_s[rows, OFF_VG:OFF_VG + GLA_VW]
        ke_bd = jnp.concatenate(
            [jnp.where(khead == hh, ke, 0.0) for hh in range(GLA_HEADS)], axis=0).astype(BF16)
        att_raw = _dot_nt(qe, ke_bd)
        v_bd = jnp.concatenate(
            [jnp.where(vhead == hh, vv, 0.0) for hh in range(GLA_HEADS)], axis=0).astype(BF16)
        kd_st = jnp.concatenate(
            [kd[:, hh * GLA_DK:(hh + 1) * GLA_DK] for hh in range(GLA_HEADS)], axis=0).astype(BF16)
        upd = _dot_tn(kd_st, v_bd)
        yield
        att = jnp.where(causal, att_raw, 0.0).astype(BF16)
        o_intra = _dot(att, v_bd)
        yield
        s_blocks = read_state()
        s_bd = jnp.concatenate(
            [jnp.concatenate([s_blocks[hh] if hc == hh else zblk for hc in range(GLA_HEADS)], axis=1)
             for hh in range(GLA_HEADS)], axis=0)
        o_s[rows, :] = o_intra + _dot(qe, s_bd.astype(BF16))
        ecol = ecol_s[:, pl.ds(e0, GLA_DV)]
        write_state([ecol[hh * GLA_DK:(hh + 1) * GLA_DK] * s_blocks[hh] + upd[:, hh * GLA_DV:(hh + 1) * GLA_DV]
                     for hh in range(GLA_HEADS)])

    def step(gen):
        try:
            next(gen)
        except StopIteration:
            pass

    def part_stages(part):
        r0p = part * P
        prow = slice(r0p, r0p + P)
        t0 = part * TP
        x = x_ref[prow, :]
        h = _rms(x, ng_ref[...]).astype(BF16)
        yield
        for n in range(IN_MAIN // MXU_COLS):
            cs = slice(n * MXU_COLS, (n + 1) * MXU_COLS)
            z_s[prow, cs] = _dot(h, win_ref[:, cs])
            yield
        alow = _dot_nt(h, wina_ref[...]).astype(BF16)

        p3 = z_s[prow, OFF_P:OFF_P + POOL_WIDTH].reshape(NS, TP, POOL_WIDTH)
        pz_s[:, POOL_PAD + t0:POOL_PAD + t0 + TP, :] = p3
        e = pz_s[:, t0:t0 + POOL_PAD + TP, :]
        w2 = e + pltpu.roll(e, 1, 1)
        w4 = w2 + pltpu.roll(w2, 2, 1)
        w8 = w4 + pltpu.roll(w4, 4, 1)
        w16 = w8 + pltpu.roll(w8, 8, 1)
        wsum = jnp.where(grp == 0, w2, jnp.where(grp == 1, w4, jnp.where(grp == 2, w8, w16)))
        wsum = wsum[:, POOL_PAD:POOL_PAD + TP, :]
        pos = pos0 + j * T + t0 + lax.broadcasted_iota(jnp.int32, (1, TP, 1), 1)
        cnt = jnp.minimum(pos + 1, win).astype(F32)
        d = (wsum / cnt - p3).reshape(P, POOL_WIDTH)
        a_out = _dot(d.astype(BF16), poolw_ref[...]) * pscale_ref[...]
        yield

        v = z_s[prow, OFF_V:OFF_V + SGU_WIDTH]
        u = z_s[prow, OFF_U:OFF_U + SGU_WIDTH]
        vsq_hi, vsq_lo = _split_bf16(v * v)
        ss = _dot(vsq_hi, head_ones) + _dot(vsq_lo, head_ones)
        vn = v * lax.rsqrt(ss * (1.0 / SGU_HEAD_DIM) + EPS) * sgug_ref[...]
        if emit_vn:
            vn_o_ref[prow, :] = vn
        yield
        if T >= SGU_CHUNK:
            parts = []
            for c in range(P // SGU_CHUNK):
                vc = vn[c * SGU_CHUNK:(c + 1) * SGU_CHUNK]
                stack = jnp.concatenate(
                    [jnp.where(lane_head == hh, vc, 0.0) for hh in range(SGU_HEADS)], axis=0)
                parts.append(_dot(wcat, stack.astype(BF16)) + sgub_ref[...])
            s_gate = jnp.concatenate(parts, axis=0)
        else:
            vn3 = vn.reshape(NS, T, SGU_WIDTH)
            srow = lax.broadcasted_iota(jnp.int32, (T, SGU_WIDTH), 0)
            s3 = jnp.zeros((NS, T, SGU_WIDTH), F32) + sgub_ref[...][None]
            for jj in range(T):
                coef = jnp.where(srow >= jj, sguw_ref[jj], 0.0)
                s3 = s3 + vn3[:, jj:jj + 1, :] * coef[None]
            s_gate = s3.reshape(P, SGU_WIDTH)
        b_out = u * s_gate
        yield

        xg = _dot(alow, wa2_ref[...]) + ba_ref[...]
        lg = (jnp.minimum(xg, 0.0) - jnp.log(1.0 + jnp.exp(-jnp.abs(xg)))) * (1.0 / GLA_GATE_NORM)
        lg_hi, lg_lo = _split_bf16(lg)
        bcum = _dot(tri_bd, lg_hi) + _dot(tri_bd, lg_lo)
        b3 = bcum.reshape(units, C, GLA_KW)
        btot = jnp.broadcast_to(b3[:, C - 1:C, :], (units, C, GLA_KW)).reshape(P, GLA_KW)
        ke_all = z_s[prow, OFF_K:OFF_K + GLA_KW] * jnp.exp(-bcum)
        qe_s[prow, :] = z_s[prow, OFF_Q:OFF_Q + GLA_KW] * (GLA_DK ** -0.5) * jnp.exp(bcum)
        ke_s[prow, :] = ke_all
        kd_s[prow, :] = ke_all * jnp.exp(btot)
        yield
        etot = jnp.exp(b3[:, C - 1, :])
        et_hi = etot.astype(BF16).astype(F32)
        et_lo = etot - et_hi
        zpad = jnp.zeros((4, GLA_KW), F32)
        for pp in range(units // 2):
            a_rows = jnp.concatenate([et_hi[2 * pp:2 * pp + 1], et_lo[2 * pp:2 * pp + 1],
                                      et_hi[2 * pp + 1:2 * pp + 2], et_lo[2 * pp + 1:2 * pp + 2], zpad], axis=0)
            e0 = (part * units + 2 * pp) * GLA_DV
            ecol_s[:, e0:e0 + 2 * GLA_DV] = _dot_tn(a_rows.astype(BF16), sel2)
        yield

        if NS == 1:
            state = [[gla_o_ref[0, hh] for hh in range(GLA_HEADS)]]
            ug = [gla_unit_stages(r0p + c * C, (part * units + c) * GLA_DV,
                                  lambda: state[0], lambda new: state.__setitem__(0, new)) for c in range(units)]
            step(ug[0])
            step(ug[0])
            for c in range(units):
                if c + 1 < units:
                    step(ug[c + 1])
                step(ug[c])
                if c + 1 < units:
                    step(ug[c + 1])
                yield
            for hh in range(GLA_HEADS):
                gla_o_ref[0, hh] = state[0][hh]
        else:
            assert units == NS and NS % SEQ_UNROLL == 0

            def seq_group(g, carry):
                def unit(sq):
                    def write(new):
                        for hh in range(GLA_HEADS):
                            gla_o_ref[sq, hh] = new[hh]
                    return gla_unit_stages(pl.multiple_of(sq * C, C), pl.multiple_of(sq * GLA_DV, GLA_DV),
                                           lambda: [gla_o_ref[sq, hh] for hh in range(GLA_HEADS)], write)
                ug = [unit(g * SEQ_UNROLL + i) for i in range(SEQ_UNROLL)]
                for _ in range(3):
                    for u_gen in ug:
                        step(u_gen)
                return carry

            lax.fori_loop(0, NS // SEQ_UNROLL, seq_group, 0)

        gate = z_s[prow, OFF_G:OFF_G + GLA_VW]
        o_all = o_s[prow, :]
        c_parts = []
        for hh in range(GLA_HEADS):
            sl = slice(hh * GLA_DV, (hh + 1) * GLA_DV)
            gh = gate[:, sl]
            c_parts.append(_rms(o_all[:, sl], glag_ref[...]) * (gh * jax.nn.sigmoid(gh)))

        mix = jnp.concatenate([a_out, b_out] + c_parts, axis=-1).astype(BF16)
        yield
        for n in range(D_MODEL // MXU_COLS):
            cs = slice(n * MXU_COLS, (n + 1) * MXU_COLS)
            xo_ref[prow, cs] = x[:, cs] + _dot(mix, wout_ref[:, cs])
            yield


    n_proj = 1 + IN_MAIN // MXU_COLS
    gens = [part_stages(p) for p in range(R // P)]
    done = [False] * len(gens)

    def advance(p, n=1):
        for _ in range(n):
            if not done[p]:
                try:
                    next(gens[p])
                except StopIteration:
                    done[p] = True

    n_mix = 6 + units if NS == 1 else 7
    n_head = 2
    advance(0, n_head)
    for src, dst in zip(cast_in, cast_out):
        dst[...] = src[...].astype(BF16)
        advance(0, 2)
        n_head += 2
    assert n_head <= n_proj
    advance(0, n_proj - n_head)
    for p in range(len(gens)):
        for _ in range(n_mix):
            advance(p)
            if p + 1 < len(gens):
                advance(p + 1)
            if p > 0:
                advance(p - 1)
    for p in range(len(gens)):
        while not done[p]:
            advance(p)

    pool_o_ref[...] = pz_s[:, T + 1:T + POOL_PAD, :]
    if T >= POOL_PAD:
        pz_s[:, 0:POOL_PAD, :] = pz_s[:, T:T + POOL_PAD, :]


def _ffn_kernel(xp_ref, xs_ref, g_ref, wg_ref, wu_ref, wd_ref, fg_ref, op_ref, os_ref, *, n_prompt_steps, final):
    def part_stages(x_ref, o_ref, rows):
        x = x_ref[rows, :]
        hf = _rms(x, g_ref[...]).astype(BF16)
        yield
        acc = x
        for lo, hi in FFN_CHUNKS:
            gt = _dot(hf, wg_ref[:, lo:hi])
            yield
            up = _dot(hf, wu_ref[:, lo:hi])
            yield
            act = (gt * jax.nn.sigmoid(gt) * up).astype(BF16)
            acc = acc + _dot(act, wd_ref[lo:hi, :])
            yield
        if final:
            acc = _rms(acc, fg_ref[...])
        o_ref[rows, :] = acc

    def ffn(x_ref, o_ref):
        n_rows = x_ref.shape[0]
        gens = [part_stages(x_ref, o_ref, slice(r, r + PART_ROWS)) for r in range(0, n_rows, PART_ROWS)]
        live = list(gens)
        lag = 0
        while live:
            for g in list(live[:lag + 1]):
                try:
                    next(g)
                except StopIteration:
                    live.remove(g)
            lag += 1

    i = pl.program_id(0)
    pl.when(i < n_prompt_steps)(lambda: ffn(xp_ref, op_ref))
    pl.when(i >= n_prompt_steps)(lambda: ffn(xs_ref, os_ref))


def _wspec(shape, layer):
    nd = len(shape)
    return pl.BlockSpec((None,) + tuple(shape), lambda *g, _l=layer, _n=nd: (_l,) + (0,) * _n,
                        pipeline_mode=pl.Buffered(1))


def _mix_call(layer, x2, pool0, gla0, wts, prev, *, nseq, L, NS, T, pos0, state_layer, emit_vn, seed_shapes=(), cast_srcs=(), win_t_next=None):
    R = NS * T
    n_chunk = L // T
    grid = (nseq // NS, n_chunk)
    (ng, win, wina, poolw, pscale, sgug, sguw, sgub, wa2, ba, glag, wout) = wts
    sl = state_layer
    in_specs = [
        pl.BlockSpec((R, D_MODEL), lambda i, j: (i * n_chunk + j, 0)),
        pl.BlockSpec((None, NS, POOL_BUF, POOL_WIDTH), lambda i, j: (sl, i, 0, 0)),
        pl.BlockSpec((None, NS, GLA_HEADS, GLA_DK, GLA_DV), lambda i, j: (sl, i, 0, 0, 0)),
        _wspec((1, D_MODEL), layer),
        pl.BlockSpec((D_MODEL, IN_MAIN), lambda i, j: (0, 0), pipeline_mode=pl.Buffered(1)),
        _wspec((GLA_RANK_PAD, D_MODEL), layer),
        _wspec((POOL_WIDTH, POOL_WIDTH), layer),
        _wspec((1, POOL_WIDTH), layer),
        _wspec((1, SGU_WIDTH), layer),
        _wspec(sguw.shape[1:], layer),
        _wspec(sgub.shape[1:], layer),
        _wspec((GLA_RANK_PAD, GLA_KW), layer),
        _wspec((1, GLA_KW), layer),
        _wspec((1, GLA_DV), layer),
        pl.BlockSpec((D_MODEL, D_MODEL), lambda i, j: (0, 0), pipeline_mode=pl.Buffered(1)),
    ]
    operands = [x2, pool0, gla0, ng, win, wina, poolw, pscale, sgug, sguw, sgub, wa2, ba, glag, wout]
    out_specs = [
        pl.BlockSpec((R, D_MODEL), lambda i, j: (i * n_chunk + j, 0)),
        pl.BlockSpec((None, NS, POOL_BUF, POOL_WIDTH), lambda i, j: (layer, i, 0, 0)),
        pl.BlockSpec((None, NS, GLA_HEADS, GLA_DK, GLA_DV), lambda i, j: (layer, i, 0, 0, 0)),
    ]
    out_shape = [
        jax.ShapeDtypeStruct((nseq * L, D_MODEL), F32),
        jax.ShapeDtypeStruct((DEPTH, nseq, POOL_BUF, POOL_WIDTH), F32),
        jax.ShapeDtypeStruct((DEPTH, nseq, GLA_HEADS, GLA_DK, GLA_DV), F32),
    ]
    if emit_vn:
        out_specs.append(pl.BlockSpec((None, R, SGU_WIDTH), lambda i, j: (layer, i * n_chunk + j, 0)))
        out_shape.append(jax.ShapeDtypeStruct((DEPTH, nseq * L, SGU_WIDTH), F32))
    assert len(prev) == len(out_shape) - 1
    n_steps = grid[0] * grid[1]
    for shp in seed_shapes:
        blk = (shp[0], shp[1] // n_steps) + tuple(shp[2:])
        out_specs.append(pl.BlockSpec(blk, lambda i, j, _n=len(shp): (0, i * n_chunk + j) + (0,) * (_n - 2)))
        out_shape.append(jax.ShapeDtypeStruct(shp, F32))
    for w, w_layer in cast_srcs:
        rows, cols = w.shape[1] // n_steps, w.shape[2]
        operands.append(w)
        in_specs.append(pl.BlockSpec((None, rows, cols), lambda i, j, _l=w_layer: (_l, i * n_chunk + j, 0)))
        out_specs.append(pl.BlockSpec((rows, cols), lambda i, j: (i * n_chunk + j, 0)))
        out_shape.append(jax.ShapeDtypeStruct(w.shape[1:], BF16))
    if win_t_next is not None:
        n_blk = IN_MAIN // MXU_COLS
        assert n_steps >= n_blk
        blk_idx = lambda i, j: jnp.minimum(i * n_chunk + j, n_blk - 1)
        operands.append(win_t_next)
        in_specs.append(pl.BlockSpec((None, MXU_COLS, D_MODEL), lambda i, j: (layer + 1, blk_idx(i, j), 0)))
        out_specs.append(pl.BlockSpec((D_MODEL, MXU_COLS), lambda i, j: (0, blk_idx(i, j))))
        out_shape.append(jax.ShapeDtypeStruct((D_MODEL, IN_MAIN), BF16))
    aliases = {}
    for k, arr in enumerate(prev):
        aliases[len(operands)] = 1 + k
        operands.append(arr)
        in_specs.append(pl.BlockSpec(memory_space=pl.ANY))
    scratch = [
        pltpu.VMEM((R, IN_MAIN), F32),
        pltpu.VMEM((NS, POOL_PAD + T, POOL_WIDTH), F32),
        pltpu.VMEM((R, GLA_KW), F32),
        pltpu.VMEM((R, GLA_KW), F32),
        pltpu.VMEM((R, GLA_KW), F32),
        pltpu.VMEM((GLA_KW, (R // min(GLA_CHUNK, T)) * GLA_DV), F32),
        pltpu.VMEM((R, GLA_VW), F32),
    ]
    outs = pl.pallas_call(
        functools.partial(_mix_kernel, NS=NS, T=T, pos0=pos0, emit_vn=emit_vn, n_prev=len(prev),
                          n_cast=len(cast_srcs), cast_win=win_t_next is not None),
        grid=grid, in_specs=in_specs, out_specs=out_specs, out_shape=out_shape,
        scratch_shapes=scratch, input_output_aliases=aliases,
        compiler_params=pltpu.CompilerParams(
            dimension_semantics=("arbitrary", "arbitrary"), vmem_limit_bytes=VMEM_LIMIT),
        name=f"mix_T{T}",
    )(*operands)
    n_state, n_seed = len(prev), len(seed_shapes)
    return (outs[0], tuple(outs[1:1 + n_state]), tuple(outs[1 + n_state:1 + n_state + n_seed]),
            tuple(outs[1 + n_state + n_seed:]))


def _ffn_call(layer, xp, xs, wts, final_g, *, TM, final):
    n_p, n_s = xp.shape[0] // TM, xs.shape[0] // TM
    g, wg, wu, wd = wts
    p_idx = lambda i: (jnp.minimum(i, n_p - 1), 0)
    s_idx = lambda i: (jnp.maximum(i - n_p, 0), 0)
    return pl.pallas_call(
        functools.partial(_ffn_kernel, n_prompt_steps=n_p, final=final),
        grid=(n_p + n_s,),
        in_specs=[
            pl.BlockSpec((TM, D_MODEL), p_idx),
            pl.BlockSpec((TM, D_MODEL), s_idx),
            _wspec((1, D_MODEL), layer),
            pl.BlockSpec((D_MODEL, D_FF), lambda i: (0, 0), pipeline_mode=pl.Buffered(1)),
            pl.BlockSpec((D_MODEL, D_FF), lambda i: (0, 0), pipeline_mode=pl.Buffered(1)),
            pl.BlockSpec((D_FF, D_MODEL), lambda i: (0, 0), pipeline_mode=pl.Buffered(1)),
            pl.BlockSpec((1, D_MODEL), lambda i: (0, 0)),
        ],
        out_specs=[pl.BlockSpec((TM, D_MODEL), p_idx), pl.BlockSpec((TM, D_MODEL), s_idx)],
        out_shape=[jax.ShapeDtypeStruct(xp.shape, F32), jax.ShapeDtypeStruct(xs.shape, F32)],
        compiler_params=pltpu.CompilerParams(
            dimension_semantics=("arbitrary",), vmem_limit_bytes=VMEM_LIMIT),
        name="ffn",
    )(xp, xs, g, wg, wu, wd, final_g)


def _cast_win0_kernel(wt_ref, o_ref):
    o_ref[...] = wt_ref[...].T.astype(BF16)


def _cast_win0(win_t):
    return pl.pallas_call(
        _cast_win0_kernel,
        grid=(IN_MAIN // MXU_COLS,),
        in_specs=[pl.BlockSpec((None, MXU_COLS, D_MODEL), lambda s: (0, s, 0))],
        out_specs=pl.BlockSpec((D_MODEL, MXU_COLS), lambda s: (0, s)),
        out_shape=jax.ShapeDtypeStruct((D_MODEL, IN_MAIN), BF16),
        compiler_params=pltpu.CompilerParams(dimension_semantics=("arbitrary",)),
        name="cast_win0",
    )(win_t)


def kernel(x_prompt, x_sample, state_pool, state_gla, attn_norm_g, w_in, pool_w, pool_scale, sgu_norm_g, sgu_ws, sgu_b, gla_wa2, gla_ba, gla_norm_g, w_out, ffn_norm_g, w_gate, w_up, w_down, final_norm_g):
    bp, seq, _ = x_prompt.shape
    bs, dseq, _ = x_sample.shape
    assert seq % PROMPT_TILE == 0 and dseq == 8 and bs % SAMPLE_SEQS == 0

    ng = attn_norm_g[:, None, :]
    win_t = w_in.transpose(0, 2, 1)
    win = _cast_win0(win_t)
    wina = jnp.pad(win_t[:, IN_MAIN:], ((0, 0), (0, GLA_RANK_PAD - GLA_RANK), (0, 0))).astype(BF16)
    eye_g = jnp.eye(len(POOL_WINDOWS), dtype=F32)
    poolw = jnp.einsum('lgcd,gh->lgchd', pool_w, eye_g).reshape(DEPTH, POOL_WIDTH, POOL_WIDTH).astype(BF16)
    pscale = pool_scale[:, None, :]
    sgug = sgu_norm_g.reshape(DEPTH, 1, SGU_WIDTH)
    sguw_cat = sgu_ws.transpose(0, 2, 1, 3).reshape(DEPTH, SGU_CHUNK, SGU_HEADS * SGU_CHUNK)
    sgub_tile = jnp.repeat(sgu_b.transpose(0, 2, 1), SGU_HEAD_DIM, axis=-1)
    sguw_dec = jnp.repeat(sgu_ws[:, :, :dseq, :dseq].transpose(0, 3, 2, 1), SGU_HEAD_DIM, axis=-1)
    sgub_dec = sgub_tile[:, :dseq]
    wa2 = jnp.pad(gla_wa2, ((0, 0), (0, GLA_RANK_PAD - GLA_RANK), (0, 0))).astype(BF16)
    ba = gla_ba[:, None, :]
    glag = gla_norm_g[:, None, :]
    wout = w_out[0].astype(BF16)
    fng = ffn_norm_g[:, None, :]
    fin = final_norm_g[None, :]

    xp = x_prompt.reshape(bp * seq, D_MODEL)
    xs = x_sample.reshape(bs * dseq, D_MODEL)
    pool0_p = jnp.zeros((1, bp, POOL_BUF, POOL_WIDTH), F32)
    gla0_p = jnp.zeros((1, bp, GLA_HEADS, GLA_DK, GLA_DV), F32)

    st_p = (jnp.zeros((DEPTH, bp, POOL_BUF, POOL_WIDTH), F32), jnp.zeros((DEPTH, bp, GLA_HEADS, GLA_DK, GLA_DV), F32))
    seeds_s = ((DEPTH, bs, POOL_BUF, POOL_WIDTH), (DEPTH, bs, GLA_HEADS, GLA_DK, GLA_DV), (DEPTH, bs * dseq, SGU_WIDTH))
    st_s = None
    for l in range(DEPTH):
        common = (ng, win, wina, poolw, pscale, sgug)
        tail = (wa2, ba, glag, wout)
        nxt = l + 1 < DEPTH
        xp, st_p, seeds, cast = _mix_call(l, xp, pool0_p, gla0_p, common + (sguw_cat, sgub_tile) + tail, st_p,
                                    nseq=bp, L=seq, NS=1, T=PROMPT_TILE, pos0=0, state_layer=0, emit_vn=False,
                                    seed_shapes=seeds_s if l == 0 else (),
                                    cast_srcs=((w_gate, l), (w_up, l), (w_down, l)) + (((w_out, l + 1),) if nxt else ()),
                                    win_t_next=win_t if nxt else None)
        st_s = seeds if l == 0 else st_s
        xs, st_s, _, _ = _mix_call(l, xs, state_pool, state_gla, common + (sguw_dec, sgub_dec) + tail, st_s,
                                nseq=bs, L=dseq, NS=SAMPLE_SEQS, T=dseq, pos0=PAST_LEN, state_layer=l, emit_vn=True)
        xp, xs = _ffn_call(l, xp, xs, (fng,) + cast[:3], fin, TM=FFN_ROWS, final=l == DEPTH - 1)
        if nxt:
            wout, win = cast[3], cast[4]

    return (xp.reshape(bp, seq, D_MODEL), xs.reshape(bs, dseq, D_MODEL),
            st_p[0], st_p[1], st_s[0], st_s[1], st_s[2].reshape(DEPTH, bs, dseq, SGU_WIDTH))
```

```python
import functools

import jax
import jax.numpy as jnp
from jax import lax
from jax.experimental import pallas as pl
from jax.experimental.pallas import tpu as pltpu

F32 = jnp.float32
BF16 = jnp.bfloat16

D_MODEL = 1024
DEPTH = 4
POOL_WIDTH = 256
POOL_WINDOWS = (2, 4, 8, 16)
POOL_GROUP_DIM = 64
POOL_BUF = 15
POOL_PAD = 16
SGU_WIDTH = 256
SGU_HEADS = 4
SGU_HEAD_DIM = 64
SGU_CHUNK = 128
GLA_HEADS = 4
GLA_DK = 64
GLA_DV = 128
GLA_KW = GLA_HEADS * GLA_DK
GLA_VW = GLA_HEADS * GLA_DV
GLA_RANK = 16
GLA_RANK_PAD = 128
GLA_GATE_NORM = 16.0
GLA_CHUNK = 64
MXU_COLS = 256
PART_ROWS = 256
SEQ_UNROLL = 16
D_FF = 2816
EPS = 1e-6

OFF_P, OFF_U, OFF_V, OFF_Q, OFF_K, OFF_VG, OFF_G, OFF_A = 0, 256, 512, 768, 1024, 1280, 1792, 2304
IN_MAIN = 2304
CAST0_COLS = 768
FFN_CHUNKS = ((0, 1536), (1536, 2816))

PAST_LEN = 16384
PROMPT_TILE = 1024
SAMPLE_SEQS = 32
FFN_ROWS = 512
VMEM_LIMIT = 56 * 1024 * 1024


def _rms(x, g):
    return x * lax.rsqrt(jnp.mean(x * x, axis=-1, keepdims=True) + EPS) * g


def _dot(a, b):
    return jnp.dot(a, b, preferred_element_type=F32)


def _dot_nt(a, b):
    return lax.dot_general(a, b, (((1,), (1,)), ((), ())), preferred_element_type=F32)


def _dot_tn(a, b):
    return lax.dot_general(a, b, (((0,), (0,)), ((), ())), preferred_element_type=F32)


def _split_bf16(x):
    hi = x.astype(BF16)
    lo = (x - hi.astype(F32)).astype(BF16)
    return hi, lo


def _mix_kernel(*refs, NS, T, pos0, emit_vn, n_prev, n_cast, cast_win):
    (x_ref, pool0_ref, gla0_ref, ng_ref, win_ref, wina_ref, poolw_ref, pscale_ref,
     sgug_ref, sguw_ref, sgub_ref, wa2_ref, ba_ref, glag_ref, wout_ref) = refs[:15]
    cast_in = refs[15:15 + n_cast]
    wint_ref = refs[15 + n_cast] if cast_win else None
    refs = refs[15 + n_cast + cast_win + n_prev:]
    xo_ref, pool_o_ref, gla_o_ref = refs[:3]
    vn_o_ref = refs[3] if emit_vn else None
    n_tail = 7 + cast_win
    seed_refs = refs[3 + emit_vn:len(refs) - n_tail - n_cast]
    cast_out = refs[len(refs) - n_tail - n_cast:len(refs) - n_tail]
    win_next_ref = refs[len(refs) - n_tail] if cast_win else None
    z_s, pz_s, qe_s, ke_s, kd_s, ecol_s, o_s = refs[-7:]
    j = pl.program_id(1)
    R = NS * T
    P = PART_ROWS
    TP = P // NS
    C = min(GLA_CHUNK, T)
    units = P // C
    assert R % P == 0 and (NS == 1 or R == P)

    @pl.when(j == 0)
    def _():
        pz_s[:, 0:1, :] = jnp.zeros((NS, 1, POOL_WIDTH), F32)
        pz_s[:, 1:POOL_PAD, :] = pool0_ref[...]
        gla_o_ref[...] = gla0_ref[...]

    for seed_ref in seed_refs:
        seed_ref[...] = jnp.zeros(seed_ref.shape, F32)
    if cast_win:
        win_next_ref[...] = wint_ref[...].T.astype(BF16)

    grp = lax.broadcasted_iota(jnp.int32, (1, 1, POOL_WIDTH), 2) // POOL_GROUP_DIM
    win = jnp.where(grp == 0, 2, jnp.where(grp == 1, 4, jnp.where(grp == 2, 8, 16)))
    r_i = lax.broadcasted_iota(jnp.int32, (SGU_WIDTH, SGU_WIDTH), 0) // SGU_HEAD_DIM
    c_i = lax.broadcasted_iota(jnp.int32, (SGU_WIDTH, SGU_WIDTH), 1) // SGU_HEAD_DIM
    head_ones = (r_i == c_i).astype(BF16)
    lane_head = lax.broadcasted_iota(jnp.int32, (1, SGU_WIDTH), 1) // SGU_HEAD_DIM
    if T >= SGU_CHUNK:
        row = lax.broadcasted_iota(jnp.int32, (SGU_CHUNK, SGU_HEADS * SGU_CHUNK), 0)
        col = lax.broadcasted_iota(jnp.int32, (SGU_CHUNK, SGU_HEADS * SGU_CHUNK), 1)
        wcat = jnp.where((col & (SGU_CHUNK - 1)) <= row, sguw_ref[...], 0.0).astype(BF16)
    rr = lax.broadcasted_iota(jnp.int32, (P, P), 0)
    cc = lax.broadcasted_iota(jnp.int32, (P, P), 1)
    tri_bd = ((rr // C == cc // C) & (rr >= cc)).astype(BF16)
    sel2 = (lax.broadcasted_iota(jnp.int32, (8, 2 * GLA_DV), 0) // 2
            == lax.broadcasted_iota(jnp.int32, (8, 2 * GLA_DV), 1) // GLA_DV).astype(BF16)
    khead = lax.broadcasted_iota(jnp.int32, (1, GLA_KW), 1) // GLA_DK
    vhead = lax.broadcasted_iota(jnp.int32, (1, GLA_VW), 1) // GLA_DV
    arow = lax.broadcasted_iota(jnp.int32, (C, GLA_HEADS * C), 0)
    acol = lax.broadcasted_iota(jnp.int32, (C, GLA_HEADS * C), 1) & (C - 1)
    causal = acol <= arow
    zblk = jnp.zeros((GLA_DK, GLA_DV), F32)

    def gla_unit_stages(r0, e0, read_state, write_state):
        rows = pl.ds(r0, C)
        qe = qe_s[rows, :].astype(BF16)
        ke = ke_s[rows, :]
        kd = kd_s[rows, :]
        vv = z_s[rows, OFF_VG:OFF_VG + GLA_VW]
        ke_bd = jnp.concatenate(
            [jnp.where(khead == hh, ke, 0.0) for hh in range(GLA_HEADS)], axis=0).astype(BF16)
        att_raw = _dot_nt(qe, ke_bd)
        v_bd = jnp.concatenate(
            [jnp.where(vhead == hh, vv, 0.0) for hh in range(GLA_HEADS)], axis=0).astype(BF16)
        kd_st = jnp.concatenate(
            [kd[:, hh * GLA_DK:(hh + 1) * GLA_DK] for hh in range(GLA_HEADS)], axis=0).astype(BF16)
        upd = _dot_tn(kd_st, v_bd)
        yield
        att = jnp.where(causal, att_raw, 0.0).astype(BF16)
        o_intra = _dot(att, v_bd)
        yield
        s_blocks = read_state()
        s_bd = jnp.concatenate(
            [jnp.concatenate([s_blocks[hh] if hc == hh else zblk for hc in range(GLA_HEADS)], axis=1)
             for hh in range(GLA_HEADS)], axis=0)
        o_s[rows, :] = o_intra + _dot(qe, s_bd.astype(BF16))
        ecol = ecol_s[:, pl.ds(e0, GLA_DV)]
        write_state([ecol[hh * GLA_DK:(hh + 1) * GLA_DK] * s_blocks[hh] + upd[:, hh * GLA_DV:(hh + 1) * GLA_DV]
                     for hh in range(GLA_HEADS)])

    def step(gen):
        try:
            next(gen)
        except StopIteration:
            pass

    def part_stages(part):
        r0p = part * P
        prow = slice(r0p, r0p + P)
        t0 = part * TP
        x = x_ref[prow, :]
        h = _rms(x, ng_ref[...]).astype(BF16)
        yield
        for n in range(IN_MAIN // MXU_COLS):
            cs = slice(n * MXU_COLS, (n + 1) * MXU_COLS)
            z_s[prow, cs] = _dot(h, win_ref[:, cs])
            yield
        alow = _dot_nt(h, wina_ref[...]).astype(BF16)

        p3 = z_s[prow, OFF_P:OFF_P + POOL_WIDTH].reshape(NS, TP, POOL_WIDTH)
        pz_s[:, POOL_PAD + t0:POOL_PAD + t0 + TP, :] = p3
        e = pz_s[:, t0:t0 + POOL_PAD + TP, :]
        w2 = e + pltpu.roll(e, 1, 1)
        w4 = w2 + pltpu.roll(w2, 2, 1)
        w8 = w4 + pltpu.roll(w4, 4, 1)
        w16 = w8 + pltpu.roll(w8, 8, 1)
        wsum = jnp.where(grp == 0, w2, jnp.where(grp == 1, w4, jnp.where(grp == 2, w8, w16)))
        wsum = wsum[:, POOL_PAD:POOL_PAD + TP, :]
        pos = pos0 + j * T + t0 + lax.broadcasted_iota(jnp.int32, (1, TP, 1), 1)
        cnt = jnp.minimum(pos + 1, win).astype(F32)
        d = (wsum / cnt - p3).reshape(P, POOL_WIDTH)
        a_out = _dot(d.astype(BF16), poolw_ref[...]) * pscale_ref[...]
        yield

        v = z_s[prow, OFF_V:OFF_V + SGU_WIDTH]
        u = z_s[prow, OFF_U:OFF_U + SGU_WIDTH]
        vsq_hi, vsq_lo = _split_bf16(v * v)
        ss = _dot(vsq_hi, head_ones) + _dot(vsq_lo, head_ones)
        vn = v * lax.rsqrt(ss * (1.0 / SGU_HEAD_DIM) + EPS) * sgug_ref[...]
        if emit_vn:
            vn_o_ref[prow, :] = vn
        yield
        if T >= SGU_CHUNK:
            parts = []
            for c in range(P // SGU_CHUNK):
                vc = vn[c * SGU_CHUNK:(c + 1) * SGU_CHUNK]
                stack = jnp.concatenate(
                    [jnp.where(lane_head == hh, vc, 0.0) for hh in range(SGU_HEADS)], axis=0)
                parts.append(_dot(wcat, stack.astype(BF16)) + sgub_ref[...])
            s_gate = jnp.concatenate(parts, axis=0)
        else:
            vn3 = vn.reshape(NS, T, SGU_WIDTH)
            srow = lax.broadcasted_iota(jnp.int32, (T, SGU_WIDTH), 0)
            s3 = jnp.zeros((NS, T, SGU_WIDTH), F32) + sgub_ref[...][None]
            for jj in range(T):
                coef = jnp.where(srow >= jj, sguw_ref[jj], 0.0)
                s3 = s3 + vn3[:, jj:jj + 1, :] * coef[None]
            s_gate = s3.reshape(P, SGU_WIDTH)
        b_out = u * s_gate
        yield

        xg = _dot(alow, wa2_ref[...]) + ba_ref[...]
        lg = (jnp.minimum(xg, 0.0) - jnp.log(1.0 + jnp.exp(-jnp.abs(xg)))) * (1.0 / GLA_GATE_NORM)
        lg_hi, lg_lo = _split_bf16(lg)
        bcum = _dot(tri_bd, lg_hi) + _dot(tri_bd, lg_lo)
        b3 = bcum.reshape(units, C, GLA_KW)
        btot = jnp.broadcast_to(b3[:, C - 1:C, :], (units, C, GLA_KW)).reshape(P, GLA_KW)
        ke_all = z_s[prow, OFF_K:OFF_K + GLA_KW] * jnp.exp(-bcum)
        qe_s[prow, :] = z_s[prow, OFF_Q:OFF_Q + GLA_KW] * (GLA_DK ** -0.5) * jnp.exp(bcum)
        ke_s[prow, :] = ke_all
        kd_s[prow, :] = ke_all * jnp.exp(btot)
        yield
        etot = jnp.exp(b3[:, C - 1, :])
        et_hi = etot.astype(BF16).astype(F32)
        et_lo = etot - et_hi
        zpad = jnp.zeros((4, GLA_KW), F32)
        for pp in range(units // 2):
            a_rows = jnp.concatenate([et_hi[2 * pp:2 * pp + 1], et_lo[2 * pp:2 * pp + 1],
                                      et_hi[2 * pp + 1:2 * pp + 2], et_lo[2 * pp + 1:2 * pp + 2], zpad], axis=0)
            e0 = (part * units + 2 * pp) * GLA_DV
            ecol_s[:, e0:e0 + 2 * GLA_DV] = _dot_tn(a_rows.astype(BF16), sel2)
        yield

        if NS == 1:
            state = [[gla_o_ref[0, hh] for hh in range(GLA_HEADS)]]
            ug = [gla_unit_stages(r0p + c * C, (part * units + c) * GLA_DV,
                                  lambda: state[0], lambda new: state.__setitem__(0, new)) for c in range(units)]
            step(ug[0])
            step(ug[0])
            for c in range(units):
                if c + 1 < units:
                    step(ug[c + 1])
                step(ug[c])
                if c + 1 < units:
                    step(ug[c + 1])
                yield
            for hh in range(GLA_HEADS):
                gla_o_ref[0, hh] = state[0][hh]
        else:
            assert units == NS and NS % SEQ_UNROLL == 0

            def seq_group(g, carry):
                def unit(sq):
                    def write(new):
                        for hh in range(GLA_HEADS):
                            gla_o_ref[sq, hh] = new[hh]
                    return gla_unit_stages(pl.multiple_of(sq * C, C), pl.multiple_of(sq * GLA_DV, GLA_DV),
                                           lambda: [gla_o_ref[sq, hh] for hh in range(GLA_HEADS)], write)
                ug = [unit(g * SEQ_UNROLL + i) for i in range(SEQ_UNROLL)]
                for _ in range(3):
                    for u_gen in ug:
                        step(u_gen)
                return carry

            lax.fori_loop(0, NS // SEQ_UNROLL, seq_group, 0)

        gate = z_s[prow, OFF_G:OFF_G + GLA_VW]
        o_all = o_s[prow, :]
        c_parts = []
        for hh in range(GLA_HEADS):
            sl = slice(hh * GLA_DV, (hh + 1) * GLA_DV)
            gh = gate[:, sl]
            c_parts.append(_rms(o_all[:, sl], glag_ref[...]) * (gh * jax.nn.sigmoid(gh)))

        mix = jnp.concatenate([a_out, b_out] + c_parts, axis=-1).astype(BF16)
        yield
        for n in range(D_MODEL // MXU_COLS):
            cs = slice(n * MXU_COLS, (n + 1) * MXU_COLS)
            xo_ref[prow, cs] = x[:, cs] + _dot(mix, wout_ref[:, cs])
            yield


    n_proj = 1 + IN_MAIN // MXU_COLS
    gens = [part_stages(p) for p in range(R // P)]
    done = [False] * len(gens)

    def advance(p, n=1):
        for _ in range(n):
            if not done[p]:
                try:
                    next(gens[p])
                except StopIteration:
                    done[p] = True

    n_mix = 6 + units if NS == 1 else 7
    n_head = 2
    advance(0, n_head)
    for src, dst in zip(cast_in, cast_out):
        dst[...] = src[...].astype(BF16)
        advance(0, 2)
        n_head += 2
    assert n_head <= n_proj
    advance(0, n_proj - n_head)
    for p in range(len(gens)):
        for _ in range(n_mix):
            advance(p)
            if p + 1 < len(gens):
                advance(p + 1)
            if p > 0:
                advance(p - 1)
    for p in range(len(gens)):
        while not done[p]:
            advance(p)

    pool_o_ref[...] = pz_s[:, T + 1:T + POOL_PAD, :]
    if T >= POOL_PAD:
        pz_s[:, 0:POOL_PAD, :] = pz_s[:, T:T + POOL_PAD, :]


def _ffn_kernel(xp_ref, xs_ref, g_ref, wg_ref, wu_ref, wd_ref, fg_ref, op_ref, os_ref, *, n_prompt_steps, final):
    def part_stages(x_ref, o_ref, rows):
        x = x_ref[rows, :]
        hf = _rms(x, g_ref[...]).astype(BF16)
        yield
        acc = x
        for lo, hi in FFN_CHUNKS:
            gt = _dot(hf, wg_ref[:, lo:hi])
            yield
            up = _dot(hf, wu_ref[:, lo:hi])
            yield
            act = (gt * jax.nn.sigmoid(gt) * up).astype(BF16)
            acc = acc + _dot(act, wd_ref[lo:hi, :])
            yield
        if final:
            acc = _rms(acc, fg_ref[...])
        o_ref[rows, :] = acc

    def ffn(x_ref, o_ref):
        n_rows = x_ref.shape[0]
        gens = [part_stages(x_ref, o_ref, slice(r, r + PART_ROWS)) for r in range(0, n_rows, PART_ROWS)]
        live = list(gens)
        lag = 0
        while live:
            for g in list(live[:lag + 1]):
                try:
                    next(g)
                except StopIteration:
                    live.remove(g)
            lag += 1

    i = pl.program_id(0)
    pl.when(i < n_prompt_steps)(lambda: ffn(xp_ref, op_ref))
    pl.when(i >= n_prompt_steps)(lambda: ffn(xs_ref, os_ref))


def _wspec(shape, layer):
    nd = len(shape)
    return pl.BlockSpec((None,) + tuple(shape), lambda *g, _l=layer, _n=nd: (_l,) + (0,) * _n,
                        pipeline_mode=pl.Buffered(1))


def _mix_call(layer, x2, pool0, gla0, wts, prev, *, nseq, L, NS, T, pos0, state_layer, emit_vn, seed_shapes=(), cast_srcs=(), win_t_next=None):
    R = NS * T
    n_chunk = L // T
    grid = (nseq // NS, n_chunk)
    (ng, win, wina, poolw, pscale, sgug, sguw, sgub, wa2, ba, glag, wout) = wts
    sl = state_layer
    in_specs = [
        pl.BlockSpec((R, D_MODEL), lambda i, j: (i * n_chunk + j, 0)),
        pl.BlockSpec((None, NS, POOL_BUF, POOL_WIDTH), lambda i, j: (sl, i, 0, 0)),
        pl.BlockSpec((None, NS, GLA_HEADS, GLA_DK, GLA_DV), lambda i, j: (sl, i, 0, 0, 0)),
        _wspec((1, D_MODEL), layer),
        pl.BlockSpec((D_MODEL, IN_MAIN), lambda i, j: (0, 0), pipeline_mode=pl.Buffered(1)),
        _wspec((GLA_RANK_PAD, D_MODEL), layer),
        _wspec((POOL_WIDTH, POOL_WIDTH), layer),
        _wspec((1, POOL_WIDTH), layer),
        _wspec((1, SGU_WIDTH), layer),
        _wspec(sguw.shape[1:], layer),
        _wspec(sgub.shape[1:], layer),
        _wspec((GLA_RANK_PAD, GLA_KW), layer),
        _wspec((1, GLA_KW), layer),
        _wspec((1, GLA_DV), layer),
        pl.BlockSpec((D_MODEL, D_MODEL), lambda i, j: (0, 0), pipeline_mode=pl.Buffered(1)),
    ]
    operands = [x2, pool0, gla0, ng, win, wina, poolw, pscale, sgug, sguw, sgub, wa2, ba, glag, wout]
    out_specs = [
        pl.BlockSpec((R, D_MODEL), lambda i, j: (i * n_chunk + j, 0)),
        pl.BlockSpec((None, NS, POOL_BUF, POOL_WIDTH), lambda i, j: (layer, i, 0, 0)),
        pl.BlockSpec((None, NS, GLA_HEADS, GLA_DK, GLA_DV), lambda i, j: (layer, i, 0, 0, 0)),
    ]
    out_shape = [
        jax.ShapeDtypeStruct((nseq * L, D_MODEL), F32),
        jax.ShapeDtypeStruct((DEPTH, nseq, POOL_BUF, POOL_WIDTH), F32),
        jax.ShapeDtypeStruct((DEPTH, nseq, GLA_HEADS, GLA_DK, GLA_DV), F32),
    ]
    if emit_vn:
        out_specs.append(pl.BlockSpec((None, R, SGU_WIDTH), lambda i, j: (layer, i * n_chunk + j, 0)))
        out_shape.append(jax.ShapeDtypeStruct((DEPTH, nseq * L, SGU_WIDTH), F32))
    assert len(prev) == len(out_shape) - 1
    n_steps = grid[0] * grid[1]
    for shp in seed_shapes:
        blk = (shp[0], shp[1] // n_steps) + tuple(shp[2:])
        out_specs.append(pl.BlockSpec(blk, lambda i, j, _n=len(shp): (0, i * n_chunk + j) + (0,) * (_n - 2)))
        out_shape.append(jax.ShapeDtypeStruct(shp, F32))
    for w, w_layer in cast_srcs:
        rows, cols = w.shape[1] // n_steps, w.shape[2]
        operands.append(w)
        in_specs.append(pl.BlockSpec((None, rows, cols), lambda i, j, _l=w_layer: (_l, i * n_chunk + j, 0)))
        out_specs.append(pl.BlockSpec((rows, cols), lambda i, j: (i * n_chunk + j, 0)))
        out_shape.append(jax.ShapeDtypeStruct(w.shape[1:], BF16))
    if win_t_next is not None:
        n_blk = IN_MAIN // MXU_COLS
        assert n_steps >= n_blk
        blk_idx = lambda i, j: jnp.minimum(i * n_chunk + j, n_blk - 1)
        operands.append(win_t_next)
        in_specs.append(pl.BlockSpec((None, MXU_COLS, D_MODEL), lambda i, j: (layer + 1, blk_idx(i, j), 0)))
        out_specs.append(pl.BlockSpec((D_MODEL, MXU_COLS), lambda i, j: (0, blk_idx(i, j))))
        out_shape.append(jax.ShapeDtypeStruct((D_MODEL, IN_MAIN), BF16))
    aliases = {}
    for k, arr in enumerate(prev):
        aliases[len(operands)] = 1 + k
        operands.append(arr)
        in_specs.append(pl.BlockSpec(memory_space=pl.ANY))
    scratch = [
        pltpu.VMEM((R, IN_MAIN), F32),
        pltpu.VMEM((NS, POOL_PAD + T, POOL_WIDTH), F32),
        pltpu.VMEM((R, GLA_KW), F32),
        pltpu.VMEM((R, GLA_KW), F32),
        pltpu.VMEM((R, GLA_KW), F32),
        pltpu.VMEM((GLA_KW, (R // min(GLA_CHUNK, T)) * GLA_DV), F32),
        pltpu.VMEM((R, GLA_VW), F32),
    ]
    outs = pl.pallas_call(
        functools.partial(_mix_kernel, NS=NS, T=T, pos0=pos0, emit_vn=emit_vn, n_prev=len(prev),
                          n_cast=len(cast_srcs), cast_win=win_t_next is not None),
        grid=grid, in_specs=in_specs, out_specs=out_specs, out_shape=out_shape,
        scratch_shapes=scratch, input_output_aliases=aliases,
        compiler_params=pltpu.CompilerParams(
            dimension_semantics=("arbitrary", "arbitrary"), vmem_limit_bytes=VMEM_LIMIT),
        name=f"mix_T{T}",
    )(*operands)
    n_state, n_seed = len(prev), len(seed_shapes)
    return (outs[0], tuple(outs[1:1 + n_state]), tuple(outs[1 + n_state:1 + n_state + n_seed]),
            tuple(outs[1 + n_state + n_seed:]))


def _ffn_call(layer, xp, xs, wts, final_g, *, TM, final):
    n_p, n_s = xp.shape[0] // TM, xs.shape[0] // TM
    g, wg, wu, wd = wts
    p_idx = lambda i: (jnp.minimum(i, n_p - 1), 0)
    s_idx = lambda i: (jnp.maximum(i - n_p, 0), 0)
    return pl.pallas_call(
        functools.partial(_ffn_kernel, n_prompt_steps=n_p, final=final),
        grid=(n_p + n_s,),
        in_specs=[
            pl.BlockSpec((TM, D_MODEL), p_idx),
            pl.BlockSpec((TM, D_MODEL), s_idx),
            _wspec((1, D_MODEL), layer),
            pl.BlockSpec((D_MODEL, D_FF), lambda i: (0, 0), pipeline_mode=pl.Buffered(1)),
            pl.BlockSpec((D_MODEL, D_FF), lambda i: (0, 0), pipeline_mode=pl.Buffered(1)),
            pl.BlockSpec((D_FF, D_MODEL), lambda i: (0, 0), pipeline_mode=pl.Buffered(1)),
            pl.BlockSpec((1, D_MODEL), lambda i: (0, 0)),
        ],
        out_specs=[pl.BlockSpec((TM, D_MODEL), p_idx), pl.BlockSpec((TM, D_MODEL), s_idx)],
        out_shape=[jax.ShapeDtypeStruct(xp.shape, F32), jax.ShapeDtypeStruct(xs.shape, F32)],
        compiler_params=pltpu.CompilerParams(
            dimension_semantics=("arbitrary",), vmem_limit_bytes=VMEM_LIMIT),
        name="ffn",
    )(xp, xs, g, wg, wu, wd, final_g)


def _cast_win0_kernel(wt_ref, o_ref):
    o_ref[...] = wt_ref[...].T.astype(BF16)


def _cast_win0(win_t):
    return pl.pallas_call(
        _cast_win0_kernel,
        grid=(IN_MAIN // CAST0_COLS,),
        in_specs=[pl.BlockSpec((None, CAST0_COLS, D_MODEL), lambda s: (0, s, 0))],
        out_specs=pl.BlockSpec((D_MODEL, CAST0_COLS), lambda s: (0, s)),
        out_shape=jax.ShapeDtypeStruct((D_MODEL, IN_MAIN), BF16),
        compiler_params=pltpu.CompilerParams(dimension_semantics=("arbitrary",)),
        name="cast_win0",
    )(win_t)


def kernel(x_prompt, x_sample, state_pool, state_gla, attn_norm_g, w_in, pool_w, pool_scale, sgu_norm_g, sgu_ws, sgu_b, gla_wa2, gla_ba, gla_norm_g, w_out, ffn_norm_g, w_gate, w_up, w_down, final_norm_g):
    bp, seq, _ = x_prompt.shape
    bs, dseq, _ = x_sample.shape
    assert seq % PROMPT_TILE == 0 and dseq == 8 and bs % SAMPLE_SEQS == 0

    ng = attn_norm_g[:, None, :]
    win_t = w_in.transpose(0, 2, 1)
    win = _cast_win0(win_t)
    wina = jnp.pad(win_t[:, IN_MAIN:], ((0, 0), (0, GLA_RANK_PAD - GLA_RANK), (0, 0))).astype(BF16)
    eye_g = jnp.eye(len(POOL_WINDOWS), dtype=F32)
    poolw = jnp.einsum('lgcd,gh->lgchd', pool_w, eye_g).reshape(DEPTH, POOL_WIDTH, POOL_WIDTH).astype(BF16)
    pscale = pool_scale[:, None, :]
    sgug = sgu_norm_g.reshape(DEPTH, 1, SGU_WIDTH)
    sguw_cat = sgu_ws.transpose(0, 2, 1, 3).reshape(DEPTH, SGU_CHUNK, SGU_HEADS * SGU_CHUNK)
    sgub_tile = jnp.repeat(sgu_b.transpose(0, 2, 1), SGU_HEAD_DIM, axis=-1)
    sguw_dec = jnp.repeat(sgu_ws[:, :, :dseq, :dseq].transpose(0, 3, 2, 1), SGU_HEAD_DIM, axis=-1)
    sgub_dec = sgub_tile[:, :dseq]
    wa2 = jnp.pad(gla_wa2, ((0, 0), (0, GLA_RANK_PAD - GLA_RANK), (0, 0))).astype(BF16)
    ba = gla_ba[:, None, :]
    glag = gla_norm_g[:, None, :]
    wout = w_out[0].astype(BF16)
    fng = ffn_norm_g[:, None, :]
    fin = final_norm_g[None, :]

    xp = x_prompt.reshape(bp * seq, D_MODEL)
    xs = x_sample.reshape(bs * dseq, D_MODEL)
    pool0_p = jnp.zeros((1, bp, POOL_BUF, POOL_WIDTH), F32)
    gla0_p = jnp.zeros((1, bp, GLA_HEADS, GLA_DK, GLA_DV), F32)

    st_p = (jnp.zeros((DEPTH, bp, POOL_BUF, POOL_WIDTH), F32), jnp.zeros((DEPTH, bp, GLA_HEADS, GLA_DK, GLA_DV), F32))
    seeds_s = ((DEPTH, bs, POOL_BUF, POOL_WIDTH), (DEPTH, bs, GLA_HEADS, GLA_DK, GLA_DV), (DEPTH, bs * dseq, SGU_WIDTH))
    st_s = None
    for l in range(DEPTH):
        common = (ng, win, wina, poolw, pscale, sgug)
        tail = (wa2, ba, glag, wout)
        nxt = l + 1 < DEPTH
        xp, st_p, seeds, cast = _mix_call(l, xp, pool0_p, gla0_p, common + (sguw_cat, sgub_tile) + tail, st_p,
                                    nseq=bp, L=seq, NS=1, T=PROMPT_TILE, pos0=0, state_layer=0, emit_vn=False,
                                    seed_shapes=seeds_s if l == 0 else (),
                                    cast_srcs=((w_gate, l), (w_up, l), (w_down, l)) + (((w_out, l + 1),) if nxt else ()),
                                    win_t_next=win_t if nxt else None)
        st_s = seeds if l == 0 else st_s
        xs, st_s, _, _ = _mix_call(l, xs, state_pool, state_gla, common + (sguw_dec, sgub_dec) + tail, st_s,
                                nseq=bs, L=dseq, NS=SAMPLE_SEQS, T=dseq, pos0=PAST_LEN, state_layer=l, emit_vn=True)
        xp, xs = _ffn_call(l, xp, xs, (fng,) + cast[:3], fin, TM=FFN_ROWS, final=l == DEPTH - 1)
        if nxt:
            wout, win = cast[3], cast[4]

    return (xp.reshape(bp, seq, D_MODEL), xs.reshape(bs, dseq, D_MODEL),
            st_p[0], st_p[1], st_s[0], st_s[1], st_s[2].reshape(DEPTH, bs, dseq, SGU_WIDTH))
```

```python
import functools

import jax
import jax.numpy as jnp
from jax import lax
from jax.experimental import pallas as pl
from jax.experimental.pallas import tpu as pltpu

F32 = jnp.float32
BF16 = jnp.bfloat16

D_MODEL = 1024
DEPTH = 4
POOL_WIDTH = 256
POOL_WINDOWS = (2, 4, 8, 16)
POOL_GROUP_DIM = 64
POOL_BUF = 15
POOL_PAD = 16
SGU_WIDTH = 256
SGU_HEADS = 4
SGU_HEAD_DIM = 64
SGU_CHUNK = 128
GLA_HEADS = 4
GLA_DK = 64
GLA_DV = 128
GLA_KW = GLA_HEADS * GLA_DK
GLA_VW = GLA_HEADS * GLA_DV
GLA_RANK = 16
GLA_RANK_PAD = 128
GLA_GATE_NORM = 16.0
GLA_CHUNK = 64
MXU_COLS = 256
PART_ROWS = 256
SEQ_UNROLL = 32
D_FF = 2816
EPS = 1e-6

OFF_P, OFF_U, OFF_V, OFF_Q, OFF_K, OFF_VG, OFF_G, OFF_A = 0, 256, 512, 768, 1024, 1280, 1792, 2304
IN_MAIN = 2304
CAST0_COLS = 768
FFN_CHUNKS = ((0, 1536), (1536, 2816))

PAST_LEN = 16384
PROMPT_TILE = 1024
SAMPLE_SEQS = 32
FFN_ROWS = 512
VMEM_LIMIT = 56 * 1024 * 1024


def _rms(x, g):
    return x * lax.rsqrt(jnp.mean(x * x, axis=-1, keepdims=True) + EPS) * g


def _dot(a, b):
    return jnp.dot(a, b, preferred_element_type=F32)


def _dot_nt(a, b):
    return lax.dot_general(a, b, (((1,), (1,)), ((), ())), preferred_element_type=F32)


def _dot_tn(a, b):
    return lax.dot_general(a, b, (((0,), (0,)), ((), ())), preferred_element_type=F32)


def _split_bf16(x):
    hi = x.astype(BF16)
    lo = (x - hi.astype(F32)).astype(BF16)
    return hi, lo


def _mix_kernel(*refs, NS, T, pos0, emit_vn, n_prev, n_cast, cast_win):
    (x_ref, pool0_ref, gla0_ref, ng_ref, win_ref, wina_ref, poolw_ref, pscale_ref,
     sgug_ref, sguw_ref, sgub_ref, wa2_ref, ba_ref, glag_ref, wout_ref) = refs[:15]
    cast_in = refs[15:15 + n_cast]
    wint_ref = refs[15 + n_cast] if cast_win else None
    refs = refs[15 + n_cast + cast_win + n_prev:]
    xo_ref, pool_o_ref, gla_o_ref = refs[:3]
    vn_o_ref = refs[3] if emit_vn else None
    n_tail = 7 + cast_win
    seed_refs = refs[3 + emit_vn:len(refs) - n_tail - n_cast]
    cast_out = refs[len(refs) - n_tail - n_cast:len(refs) - n_tail]
    win_next_ref = refs[len(refs) - n_tail] if cast_win else None
    z_s, pz_s, qe_s, ke_s, kd_s, ecol_s, o_s = refs[-7:]
    j = pl.program_id(1)
    R = NS * T
    P = PART_ROWS
    TP = P // NS
    C = min(GLA_CHUNK, T)
    units = P // C
    assert R % P == 0 and (NS == 1 or R == P)

    @pl.when(j == 0)
    def _():
        pz_s[:, 0:1, :] = jnp.zeros((NS, 1, POOL_WIDTH), F32)
        pz_s[:, 1:POOL_PAD, :] = pool0_ref[...]
        gla_o_ref[...] = gla0_ref[...]

    for seed_ref in seed_refs:
        seed_ref[...] = jnp.zeros(seed_ref.shape, F32)
    if cast_win:
        win_next_ref[...] = wint_ref[...].T.astype(BF16)

    grp = lax.broadcasted_iota(jnp.int32, (1, 1, POOL_WIDTH), 2) // POOL_GROUP_DIM
    win = jnp.where(grp == 0, 2, jnp.where(grp == 1, 4, jnp.where(grp == 2, 8, 16)))
    r_i = lax.broadcasted_iota(jnp.int32, (SGU_WIDTH, SGU_WIDTH), 0) // SGU_HEAD_DIM
    c_i = lax.broadcasted_iota(jnp.int32, (SGU_WIDTH, SGU_WIDTH), 1) // SGU_HEAD_DIM
    head_ones = (r_i == c_i).astype(BF16)
    lane_head = lax.broadcasted_iota(jnp.int32, (1, SGU_WIDTH), 1) // SGU_HEAD_DIM
    if T >= SGU_CHUNK:
        row = lax.broadcasted_iota(jnp.int32, (SGU_CHUNK, SGU_HEADS * SGU_CHUNK), 0)
        col = lax.broadcasted_iota(jnp.int32, (SGU_CHUNK, SGU_HEADS * SGU_CHUNK), 1)
        wcat = jnp.where((col & (SGU_CHUNK - 1)) <= row, sguw_ref[...], 0.0).astype(BF16)
    rr = lax.broadcasted_iota(jnp.int32, (P, P), 0)
    cc = lax.broadcasted_iota(jnp.int32, (P, P), 1)
    tri_bd = ((rr // C == cc // C) & (rr >= cc)).astype(BF16)
    sel2 = (lax.broadcasted_iota(jnp.int32, (8, 2 * GLA_DV), 0) // 2
            == lax.broadcasted_iota(jnp.int32, (8, 2 * GLA_DV), 1) // GLA_DV).astype(BF16)
    khead = lax.broadcasted_iota(jnp.int32, (1, GLA_KW), 1) // GLA_DK
    vhead = lax.broadcasted_iota(jnp.int32, (1, GLA_VW), 1) // GLA_DV
    arow = lax.broadcasted_iota(jnp.int32, (C, GLA_HEADS * C), 0)
    acol = lax.broadcasted_iota(jnp.int32, (C, GLA_HEADS * C), 1) & (C - 1)
    causal = acol <= arow
    zblk = jnp.zeros((GLA_DK, GLA_DV), F32)

    def gla_unit_stages(r0, e0, read_state, write_state):
        rows = pl.ds(r0, C)
        qe = qe_s[rows, :].astype(BF16)
        ke = ke_s[rows, :]
        kd = kd_s[rows, :]
        vv = z_s[rows, OFF_VG:OFF_VG + GLA_VW]
        ke_bd = jnp.concatenate(
            [jnp.where(khead == hh, ke, 0.0) for hh in range(GLA_HEADS)], axis=0).astype(BF16)
        att_raw = _dot_nt(qe, ke_bd)
        v_bd = jnp.concatenate(
            [jnp.where(vhead == hh, vv, 0.0) for hh in range(GLA_HEADS)], axis=0).astype(BF16)
        kd_st = jnp.concatenate(
            [kd[:, hh * GLA_DK:(hh + 1) * GLA_DK] for hh in range(GLA_HEADS)], axis=0).astype(BF16)
        upd = _dot_tn(kd_st, v_bd)
        yield
        att = jnp.where(causal, att_raw, 0.0).astype(BF16)
        o_intra = _dot(att, v_bd)
        yield
        s_blocks = read_state()
        s_bd = jnp.concatenate(
            [jnp.concatenate([s_blocks[hh] if hc == hh else zblk for hc in range(GLA_HEADS)], axis=1)
             for hh in range(GLA_HEADS)], axis=0)
        o_s[rows, :] = o_intra + _dot(qe, s_bd.astype(BF16))
        ecol = ecol_s[:, pl.ds(e0, GLA_DV)]
        write_state([ecol[hh * GLA_DK:(hh + 1) * GLA_DK] * s_blocks[hh] + upd[:, hh * GLA_DV:(hh + 1) * GLA_DV]
                     for hh in range(GLA_HEADS)])

    def step(gen):
        try:
            next(gen)
        except StopIteration:
            pass

    def part_stages(part):
        r0p = part * P
        prow = slice(r0p, r0p + P)
        t0 = part * TP
        x = x_ref[prow, :]
        h = _rms(x, ng_ref[...]).astype(BF16)
        yield
        for n in range(IN_MAIN // MXU_COLS):
            cs = slice(n * MXU_COLS, (n + 1) * MXU_COLS)
            z_s[prow, cs] = _dot(h, win_ref[:, cs])
            yield
        alow = _dot_nt(h, wina_ref[...]).astype(BF16)

        p3 = z_s[prow, OFF_P:OFF_P + POOL_WIDTH].reshape(NS, TP, POOL_WIDTH)
        pz_s[:, POOL_PAD + t0:POOL_PAD + t0 + TP, :] = p3
        e = pz_s[:, t0:t0 + POOL_PAD + TP, :]
        w2 = e + pltpu.roll(e, 1, 1)
        w4 = w2 + pltpu.roll(w2, 2, 1)
        w8 = w4 + pltpu.roll(w4, 4, 1)
        w16 = w8 + pltpu.roll(w8, 8, 1)
        wsum = jnp.where(grp == 0, w2, jnp.where(grp == 1, w4, jnp.where(grp == 2, w8, w16)))
        wsum = wsum[:, POOL_PAD:POOL_PAD + TP, :]
        pos = pos0 + j * T + t0 + lax.broadcasted_iota(jnp.int32, (1, TP, 1), 1)
        cnt = jnp.minimum(pos + 1, win).astype(F32)
        d = (wsum / cnt - p3).reshape(P, POOL_WIDTH)
        a_out = _dot(d.astype(BF16), poolw_ref[...]) * pscale_ref[...]
        yield

        v = z_s[prow, OFF_V:OFF_V + SGU_WIDTH]
        u = z_s[prow, OFF_U:OFF_U + SGU_WIDTH]
        vsq_hi, vsq_lo = _split_bf16(v * v)
        ss = _dot(vsq_hi, head_ones) + _dot(vsq_lo, head_ones)
        vn = v * lax.rsqrt(ss * (1.0 / SGU_HEAD_DIM) + EPS) * sgug_ref[...]
        if emit_vn:
            vn_o_ref[prow, :] = vn
        yield
        if T >= SGU_CHUNK:
            parts = []
            for c in range(P // SGU_CHUNK):
                vc = vn[c * SGU_CHUNK:(c + 1) * SGU_CHUNK]
                stack = jnp.concatenate(
                    [jnp.where(lane_head == hh, vc, 0.0) for hh in range(SGU_HEADS)], axis=0)
                parts.append(_dot(wcat, stack.astype(BF16)) + sgub_ref[...])
            s_gate = jnp.concatenate(parts, axis=0)
        else:
            vn3 = vn.reshape(NS, T, SGU_WIDTH)
            srow = lax.broadcasted_iota(jnp.int32, (T, SGU_WIDTH), 0)
            s3 = jnp.zeros((NS, T, SGU_WIDTH), F32) + sgub_ref[...][None]
            for jj in range(T):
                coef = jnp.where(srow >= jj, sguw_ref[jj], 0.0)
                s3 = s3 + vn3[:, jj:jj + 1, :] * coef[None]
            s_gate = s3.reshape(P, SGU_WIDTH)
        b_out = u * s_gate
        yield

        xg = _dot(alow, wa2_ref[...]) + ba_ref[...]
        lg = (jnp.minimum(xg, 0.0) - jnp.log(1.0 + jnp.exp(-jnp.abs(xg)))) * (1.0 / GLA_GATE_NORM)
        lg_hi, lg_lo = _split_bf16(lg)
        bcum = _dot(tri_bd, lg_hi) + _dot(tri_bd, lg_lo)
        b3 = bcum.reshape(units, C, GLA_KW)
        btot = jnp.broadcast_to(b3[:, C - 1:C, :], (units, C, GLA_KW)).reshape(P, GLA_KW)
        ke_all = z_s[prow, OFF_K:OFF_K + GLA_KW] * jnp.exp(-bcum)
        qe_s[prow, :] = z_s[prow, OFF_Q:OFF_Q + GLA_KW] * (GLA_DK ** -0.5) * jnp.exp(bcum)
        ke_s[prow, :] = ke_all
        kd_s[prow, :] = ke_all * jnp.exp(btot)
        yield
        etot = jnp.exp(b3[:, C - 1, :])
        et_hi = etot.astype(BF16).astype(F32)
        et_lo = etot - et_hi
        zpad = jnp.zeros((4, GLA_KW), F32)
        for pp in range(units // 2):
            a_rows = jnp.concatenate([et_hi[2 * pp:2 * pp + 1], et_lo[2 * pp:2 * pp + 1],
                                      et_hi[2 * pp + 1:2 * pp + 2], et_lo[2 * pp + 1:2 * pp + 2], zpad], axis=0)
            e0 = (part * units + 2 * pp) * GLA_DV
            ecol_s[:, e0:e0 + 2 * GLA_DV] = _dot_tn(a_rows.astype(BF16), sel2)
        yield

        if NS == 1:
            state = [[gla_o_ref[0, hh] for hh in range(GLA_HEADS)]]
            ug = [gla_unit_stages(r0p + c * C, (part * units + c) * GLA_DV,
                                  lambda: state[0], lambda new: state.__setitem__(0, new)) for c in range(units)]
            step(ug[0])
            step(ug[0])
            for c in range(units):
                if c + 1 < units:
                    step(ug[c + 1])
                step(ug[c])
                if c + 1 < units:
                    step(ug[c + 1])
                yield
            for hh in range(GLA_HEADS):
                gla_o_ref[0, hh] = state[0][hh]
        else:
            assert units == NS and NS % SEQ_UNROLL == 0

            def seq_group(g, carry):
                def unit(sq):
                    def write(new):
                        for hh in range(GLA_HEADS):
                            gla_o_ref[sq, hh] = new[hh]
                    return gla_unit_stages(pl.multiple_of(sq * C, C), pl.multiple_of(sq * GLA_DV, GLA_DV),
                                           lambda: [gla_o_ref[sq, hh] for hh in range(GLA_HEADS)], write)
                ug = [unit(g * SEQ_UNROLL + i) for i in range(SEQ_UNROLL)]
                for _ in range(3):
                    for u_gen in ug:
                        step(u_gen)
                return carry

            lax.fori_loop(0, NS // SEQ_UNROLL, seq_group, 0)

        gate = z_s[prow, OFF_G:OFF_G + GLA_VW]
        o_all = o_s[prow, :]
        c_parts = []
        for hh in range(GLA_HEADS):
            sl = slice(hh * GLA_DV, (hh + 1) * GLA_DV)
            gh = gate[:, sl]
            c_parts.append(_rms(o_all[:, sl], glag_ref[...]) * (gh * jax.nn.sigmoid(gh)))

        mix = jnp.concatenate([a_out, b_out] + c_parts, axis=-1).astype(BF16)
        yield
        for n in range(D_MODEL // MXU_COLS):
            cs = slice(n * MXU_COLS, (n + 1) * MXU_COLS)
            xo_ref[prow, cs] = x[:, cs] + _dot(mix, wout_ref[:, cs])
            yield


    n_proj = 1 + IN_MAIN // MXU_COLS
    gens = [part_stages(p) for p in range(R // P)]
    done = [False] * len(gens)

    def advance(p, n=1):
        for _ in range(n):
            if not done[p]:
                try:
                    next(gens[p])
                except StopIteration:
                    done[p] = True

    n_mix = 6 + units if NS == 1 else 7
    n_head = 2
    advance(0, n_head)
    for src, dst in zip(cast_in, cast_out):
        dst[...] = src[...].astype(BF16)
        advance(0, 2)
        n_head += 2
    assert n_head <= n_proj
    advance(0, n_proj - n_head)
    for p in range(len(gens)):
        for _ in range(n_mix):
            advance(p)
            if p + 1 < len(gens):
                advance(p + 1)
            if p > 0:
                advance(p - 1)
    for p in range(len(gens)):
        while not done[p]:
            advance(p)

    pool_o_ref[...] = pz_s[:, T + 1:T + POOL_PAD, :]
    if T >= POOL_PAD:
        pz_s[:, 0:POOL_PAD, :] = pz_s[:, T:T + POOL_PAD, :]


def _ffn_kernel(xp_ref, xs_ref, g_ref, wg_ref, wu_ref, wd_ref, fg_ref, op_ref, os_ref, *, n_prompt_steps, final):
    def part_stages(x_ref, o_ref, rows):
        x = x_ref[rows, :]
        hf = _rms(x, g_ref[...]).astype(BF16)
        yield
        acc = x
        for lo, hi in FFN_CHUNKS:
            gt = _dot(hf, wg_ref[:, lo:hi])
            yield
            up = _dot(hf, wu_ref[:, lo:hi])
            yield
            act = (gt * jax.nn.sigmoid(gt) * up).astype(BF16)
            acc = acc + _dot(act, wd_ref[lo:hi, :])
            yield
        if final:
            acc = _rms(acc, fg_ref[...])
        o_ref[rows, :] = acc

    def ffn(x_ref, o_ref):
        n_rows = x_ref.shape[0]
        gens = [part_stages(x_ref, o_ref, slice(r, r + PART_ROWS)) for r in range(0, n_rows, PART_ROWS)]
        live = list(gens)
        lag = 0
        while live:
            for g in list(live[:lag + 1]):
                try:
                    next(g)
                except StopIteration:
                    live.remove(g)
            lag += 1

    i = pl.program_id(0)
    pl.when(i < n_prompt_steps)(lambda: ffn(xp_ref, op_ref))
    pl.when(i >= n_prompt_steps)(lambda: ffn(xs_ref, os_ref))


def _wspec(shape, layer):
    nd = len(shape)
    return pl.BlockSpec((None,) + tuple(shape), lambda *g, _l=layer, _n=nd: (_l,) + (0,) * _n,
                        pipeline_mode=pl.Buffered(1))


def _mix_call(layer, x2, pool0, gla0, wts, prev, *, nseq, L, NS, T, pos0, state_layer, emit_vn, seed_shapes=(), cast_srcs=(), win_t_next=None):
    R = NS * T
    n_chunk = L // T
    grid = (nseq // NS, n_chunk)
    (ng, win, wina, poolw, pscale, sgug, sguw, sgub, wa2, ba, glag, wout) = wts
    sl = state_layer
    in_specs = [
        pl.BlockSpec((R, D_MODEL), lambda i, j: (i * n_chunk + j, 0)),
        pl.BlockSpec((None, NS, POOL_BUF, POOL_WIDTH), lambda i, j: (sl, i, 0, 0)),
        pl.BlockSpec((None, NS, GLA_HEADS, GLA_DK, GLA_DV), lambda i, j: (sl, i, 0, 0, 0)),
        _wspec((1, D_MODEL), layer),
        pl.BlockSpec((D_MODEL, IN_MAIN), lambda i, j: (0, 0), pipeline_mode=pl.Buffered(1)),
        _wspec((GLA_RANK_PAD, D_MODEL), layer),
        _wspec((POOL_WIDTH, POOL_WIDTH), layer),
        _wspec((1, POOL_WIDTH), layer),
        _wspec((1, SGU_WIDTH), layer),
        _wspec(sguw.shape[1:], layer),
        _wspec(sgub.shape[1:], layer),
        _wspec((GLA_RANK_PAD, GLA_KW), layer),
        _wspec((1, GLA_KW), layer),
        _wspec((1, GLA_DV), layer),
        pl.BlockSpec((D_MODEL, D_MODEL), lambda i, j: (0, 0), pipeline_mode=pl.Buffered(1)),
    ]
    operands = [x2, pool0, gla0, ng, win, wina, poolw, pscale, sgug, sguw, sgub, wa2, ba, glag, wout]
    out_specs = [
        pl.BlockSpec((R, D_MODEL), lambda i, j: (i * n_chunk + j, 0)),
        pl.BlockSpec((None, NS, POOL_BUF, POOL_WIDTH), lambda i, j: (layer, i, 0, 0)),
        pl.BlockSpec((None, NS, GLA_HEADS, GLA_DK, GLA_DV), lambda i, j: (layer, i, 0, 0, 0)),
    ]
    out_shape = [
        jax.ShapeDtypeStruct((nseq * L, D_MODEL), F32),
        jax.ShapeDtypeStruct((DEPTH, nseq, POOL_BUF, POOL_WIDTH), F32),
        jax.ShapeDtypeStruct((DEPTH, nseq, GLA_HEADS, GLA_DK, GLA_DV), F32),
    ]
    if emit_vn:
        out_specs.append(pl.BlockSpec((None, R, SGU_WIDTH), lambda i, j: (layer, i * n_chunk + j, 0)))
        out_shape.append(jax.ShapeDtypeStruct((DEPTH, nseq * L, SGU_WIDTH), F32))
    assert len(prev) == len(out_shape) - 1
    n_steps = grid[0] * grid[1]
    for shp in seed_shapes:
        blk = (shp[0], shp[1] // n_steps) + tuple(shp[2:])
        out_specs.append(pl.BlockSpec(blk, lambda i, j, _n=len(shp): (0, i * n_chunk + j) + (0,) * (_n - 2)))
        out_shape.append(jax.ShapeDtypeStruct(shp, F32))
    for w, w_layer in cast_srcs:
        rows, cols = w.shape[1] // n_steps, w.shape[2]
        operands.append(w)
        in_specs.append(pl.BlockSpec((None, rows, cols), lambda i, j, _l=w_layer: (_l, i * n_chunk + j, 0)))
        out_specs.append(pl.BlockSpec((rows, cols), lambda i, j: (i * n_chunk + j, 0)))
        out_shape.append(jax.ShapeDtypeStruct(w.shape[1:], BF16))
    if win_t_next is not None:
        n_blk = IN_MAIN // MXU_COLS
        assert n_steps >= n_blk
        blk_idx = lambda i, j: jnp.minimum(i * n_chunk + j, n_blk - 1)
        operands.append(win_t_next)
        in_specs.append(pl.BlockSpec((None, MXU_COLS, D_MODEL), lambda i, j: (layer + 1, blk_idx(i, j), 0)))
        out_specs.append(pl.BlockSpec((D_MODEL, MXU_COLS), lambda i, j: (0, blk_idx(i, j))))
        out_shape.append(jax.ShapeDtypeStruct((D_MODEL, IN_MAIN), BF16))
    aliases = {}
    for k, arr in enumerate(prev):
        aliases[len(operands)] = 1 + k
        operands.append(arr)
        in_specs.append(pl.BlockSpec(memory_space=pl.ANY))
    scratch = [
        pltpu.VMEM((R, IN_MAIN), F32),
        pltpu.VMEM((NS, POOL_PAD + T, POOL_WIDTH), F32),
        pltpu.VMEM((R, GLA_KW), F32),
        pltpu.VMEM((R, GLA_KW), F32),
        pltpu.VMEM((R, GLA_KW), F32),
        pltpu.VMEM((GLA_KW, (R // min(GLA_CHUNK, T)) * GLA_DV), F32),
        pltpu.VMEM((R, GLA_VW), F32),
    ]
    outs = pl.pallas_call(
        functools.partial(_mix_kernel, NS=NS, T=T, pos0=pos0, emit_vn=emit_vn, n_prev=len(prev),
                          n_cast=len(cast_srcs), cast_win=win_t_next is not None),
        grid=grid, in_specs=in_specs, out_specs=out_specs, out_shape=out_shape,
        scratch_shapes=scratch, input_output_aliases=aliases,
        compiler_params=pltpu.CompilerParams(
            dimension_semantics=("arbitrary", "arbitrary"), vmem_limit_bytes=VMEM_LIMIT),
        name=f"mix_T{T}",
    )(*operands)
    n_state, n_seed = len(prev), len(seed_shapes)
    return (outs[0], tuple(outs[1:1 + n_state]), tuple(outs[1 + n_state:1 + n_state + n_seed]),
            tuple(outs[1 + n_state + n_seed:]))


def _ffn_call(layer, xp, xs, wts, final_g, *, TM, final):
    n_p, n_s = xp.shape[0] // TM, xs.shape[0] // TM
    g, wg, wu, wd = wts
    p_idx = lambda i: (jnp.minimum(i, n_p - 1), 0)
    s_idx = lambda i: (jnp.maximum(i - n_p, 0), 0)
    return pl.pallas_call(
        functools.partial(_ffn_kernel, n_prompt_steps=n_p, final=final),
        grid=(n_p + n_s,),
        in_specs=[
            pl.BlockSpec((TM, D_MODEL), p_idx),
            pl.BlockSpec((TM, D_MODEL), s_idx),
            _wspec((1, D_MODEL), layer),
            pl.BlockSpec((D_MODEL, D_FF), lambda i: (0, 0), pipeline_mode=pl.Buffered(1)),
            pl.BlockSpec((D_MODEL, D_FF), lambda i: (0, 0), pipeline_mode=pl.Buffered(1)),
            pl.BlockSpec((D_FF, D_MODEL), lambda i: (0, 0), pipeline_mode=pl.Buffered(1)),
            pl.BlockSpec((1, D_MODEL), lambda i: (0, 0)),
        ],
        out_specs=[pl.BlockSpec((TM, D_MODEL), p_idx), pl.BlockSpec((TM, D_MODEL), s_idx)],
        out_shape=[jax.ShapeDtypeStruct(xp.shape, F32), jax.ShapeDtypeStruct(xs.shape, F32)],
        compiler_params=pltpu.CompilerParams(
            dimension_semantics=("arbitrary",), vmem_limit_bytes=VMEM_LIMIT),
        name="ffn",
    )(xp, xs, g, wg, wu, wd, final_g)


def _cast_win0_kernel(wt_ref, o_ref):
    o_ref[...] = wt_ref[...].T.astype(BF16)


def _cast_win0(win_t):
    return pl.pallas_call(
        _cast_win0_kernel,
        grid=(IN_MAIN // CAST0_COLS,),
        in_specs=[pl.BlockSpec((None, CAST0_COLS, D_MODEL), lambda s: (0, s, 0))],
        out_specs=pl.BlockSpec((D_MODEL, CAST0_COLS), lambda s: (0, s)),
        out_shape=jax.ShapeDtypeStruct((D_MODEL, IN_MAIN), BF16),
        compiler_params=pltpu.CompilerParams(dimension_semantics=("arbitrary",)),
        name="cast_win0",
    )(win_t)


def kernel(x_prompt, x_sample, state_pool, state_gla, attn_norm_g, w_in, pool_w, pool_scale, sgu_norm_g, sgu_ws, sgu_b, gla_wa2, gla_ba, gla_norm_g, w_out, ffn_norm_g, w_gate, w_up, w_down, final_norm_g):
    bp, seq, _ = x_prompt.shape
    bs, dseq, _ = x_sample.shape
    assert seq % PROMPT_TILE == 0 and dseq == 8 and bs % SAMPLE_SEQS == 0

    ng = attn_norm_g[:, None, :]
    win_t = w_in.transpose(0, 2, 1)
    win = _cast_win0(win_t)
    wina = jnp.pad(win_t[:, IN_MAIN:], ((0, 0), (0, GLA_RANK_PAD - GLA_RANK), (0, 0))).astype(BF16)
    eye_g = jnp.eye(len(POOL_WINDOWS), dtype=F32)
    poolw = jnp.einsum('lgcd,gh->lgchd', pool_w, eye_g).reshape(DEPTH, POOL_WIDTH, POOL_WIDTH).astype(BF16)
    pscale = pool_scale[:, None, :]
    sgug = sgu_norm_g.reshape(DEPTH, 1, SGU_WIDTH)
    sguw_cat = sgu_ws.transpose(0, 2, 1, 3).reshape(DEPTH, SGU_CHUNK, SGU_HEADS * SGU_CHUNK)
    sgub_tile = jnp.repeat(sgu_b.transpose(0, 2, 1), SGU_HEAD_DIM, axis=-1)
    sguw_dec = jnp.repeat(sgu_ws[:, :, :dseq, :dseq].transpose(0, 3, 2, 1), SGU_HEAD_DIM, axis=-1)
    sgub_dec = sgub_tile[:, :dseq]
    wa2 = jnp.pad(gla_wa2, ((0, 0), (0, GLA_RANK_PAD - GLA_RANK), (0, 0))).astype(BF16)
    ba = gla_ba[:, None, :]
    glag = gla_norm_g[:, None, :]
    wout = w_out[0].astype(BF16)
    fng = ffn_norm_g[:, None, :]
    fin = final_norm_g[None, :]

    xp = x_prompt.reshape(bp * seq, D_MODEL)
    xs = x_sample.reshape(bs * dseq, D_MODEL)
    pool0_p = jnp.zeros((1, bp, POOL_BUF, POOL_WIDTH), F32)
    gla0_p = jnp.zeros((1, bp, GLA_HEADS, GLA_DK, GLA_DV), F32)

    st_p = (jnp.zeros((DEPTH, bp, POOL_BUF, POOL_WIDTH), F32), jnp.zeros((DEPTH, bp, GLA_HEADS, GLA_DK, GLA_DV), F32))
    seeds_s = ((DEPTH, bs, POOL_BUF, POOL_WIDTH), (DEPTH, bs, GLA_HEADS, GLA_DK, GLA_DV), (DEPTH, bs * dseq, SGU_WIDTH))
    st_s = None
    for l in range(DEPTH):
        common = (ng, win, wina, poolw, pscale, sgug)
        tail = (wa2, ba, glag, wout)
        nxt = l + 1 < DEPTH
        xp, st_p, seeds, cast = _mix_call(l, xp, pool0_p, gla0_p, common + (sguw_cat, sgub_tile) + tail, st_p,
                                    nseq=bp, L=seq, NS=1, T=PROMPT_TILE, pos0=0, state_layer=0, emit_vn=False,
                                    seed_shapes=seeds_s if l == 0 else (),
                                    cast_srcs=((w_gate, l), (w_up, l), (w_down, l)) + (((w_out, l + 1),) if nxt else ()),
                                    win_t_next=win_t if nxt else None)
        st_s = seeds if l == 0 else st_s
        xs, st_s, _, _ = _mix_call(l, xs, state_pool, state_gla, common + (sguw_dec, sgub_dec) + tail, st_s,
                                nseq=bs, L=dseq, NS=SAMPLE_SEQS, T=dseq, pos0=PAST_LEN, state_layer=l, emit_vn=True)
        xp, xs = _ffn_call(l, xp, xs, (fng,) + cast[:3], fin, TM=FFN_ROWS, final=l == DEPTH - 1)
        if nxt:
            wout, win = cast[3], cast[4]

    return (xp.reshape(bp, seq, D_MODEL), xs.reshape(bs, dseq, D_MODEL),
            st_p[0], st_p[1], st_s[0], st_s[1], st_s[2].reshape(DEPTH, bs, dseq, SGU_WIDTH))
```

```python
import functools

import jax
import jax.numpy as jnp
from jax import lax
from jax.experimental import pallas as pl
from jax.experimental.pallas import tpu as pltpu

F32 = jnp.float32
BF16 = jnp.bfloat16

D_MODEL = 1024
DEPTH = 4
POOL_WIDTH = 256
POOL_WINDOWS = (2, 4, 8, 16)
POOL_GROUP_DIM = 64
POOL_BUF = 15
POOL_PAD = 16
SGU_WIDTH = 256
SGU_HEADS = 4
SGU_HEAD_DIM = 64
SGU_CHUNK = 128
GLA_HEADS = 4
GLA_DK = 64
GLA_DV = 128
GLA_KW = GLA_HEADS * GLA_DK
GLA_VW = GLA_HEADS * GLA_DV
GLA_RANK = 16
GLA_RANK_PAD = 128
GLA_GATE_NORM = 16.0
GLA_CHUNK = 64
MXU_COLS = 256
LANES = 128
PART_ROWS = 256
SEQ_UNROLL = 32
D_FF = 2816
EPS = 1e-6

OFF_P, OFF_U, OFF_V, OFF_Q, OFF_K, OFF_VG, OFF_G, OFF_A = 0, 256, 512, 768, 1024, 1280, 1792, 2304
IN_MAIN = 2304
CAST0_COLS = 768
FFN_CHUNKS = ((0, 1536), (1536, 2816))

PAST_LEN = 16384
PROMPT_TILE = 1024
SAMPLE_SEQS = 32
FFN_ROWS = 512
VMEM_LIMIT = 56 * 1024 * 1024


def _rms(x, g):
    return x * lax.rsqrt(jnp.mean(x * x, axis=-1, keepdims=True) + EPS) * g


def _dot(a, b):
    return jnp.dot(a, b, preferred_element_type=F32)


def _dot_nt(a, b):
    return lax.dot_general(a, b, (((1,), (1,)), ((), ())), preferred_element_type=F32)


def _dot_tn(a, b):
    return lax.dot_general(a, b, (((0,), (0,)), ((), ())), preferred_element_type=F32)


def _split_bf16(x):
    hi = x.astype(BF16)
    lo = (x - hi.astype(F32)).astype(BF16)
    return hi, lo


def _mix_kernel(*refs, NS, T, pos0, emit_vn, n_prev, n_cast, cast_win):
    (x_ref, pool0_ref, gla0_ref, ng_ref, win_ref, wina_ref, poolw_ref, pscale_ref,
     sgug_ref, sguw_ref, sgub_ref, wa2_ref, ba_ref, glag_ref, wout_ref) = refs[:15]
    cast_in = refs[15:15 + n_cast]
    wint_ref = refs[15 + n_cast] if cast_win else None
    refs = refs[15 + n_cast + cast_win + n_prev:]
    xo_ref, pool_o_ref, gla_o_ref = refs[:3]
    vn_o_ref = refs[3] if emit_vn else None
    n_tail = 7 + cast_win
    seed_refs = refs[3 + emit_vn:len(refs) - n_tail - n_cast]
    cast_out = refs[len(refs) - n_tail - n_cast:len(refs) - n_tail]
    win_next_ref = refs[len(refs) - n_tail] if cast_win else None
    z_s, pz_s, qe_s, ke_s, kd_s, ecol_s, o_s = refs[-7:]
    j = pl.program_id(1)
    R = NS * T
    P = PART_ROWS
    TP = P // NS
    C = min(GLA_CHUNK, T)
    units = P // C
    assert R % P == 0 and (NS == 1 or R == P)

    @pl.when(j == 0)
    def _():
        if NS == 1:
            pz_s[:, 0:1, :] = jnp.zeros((NS, 1, POOL_WIDTH), F32)
            pz_s[:, 1:POOL_PAD, :] = pool0_ref[...]
        gla_o_ref[...] = gla0_ref[...]

    for seed_ref in seed_refs:
        seed_ref[...] = jnp.zeros(seed_ref.shape, F32)
    if cast_win:
        win_next_ref[...] = wint_ref[...].T.astype(BF16)

    grp = lax.broadcasted_iota(jnp.int32, (1, 1, POOL_WIDTH), 2) // POOL_GROUP_DIM
    win = jnp.where(grp == 0, 2, jnp.where(grp == 1, 4, jnp.where(grp == 2, 8, 16)))
    r_i = lax.broadcasted_iota(jnp.int32, (SGU_WIDTH, SGU_WIDTH), 0) // SGU_HEAD_DIM
    c_i = lax.broadcasted_iota(jnp.int32, (SGU_WIDTH, SGU_WIDTH), 1) // SGU_HEAD_DIM
    head_ones = (r_i == c_i).astype(BF16)
    lane_head = lax.broadcasted_iota(jnp.int32, (1, SGU_WIDTH), 1) // SGU_HEAD_DIM
    if T >= SGU_CHUNK:
        row = lax.broadcasted_iota(jnp.int32, (SGU_CHUNK, SGU_HEADS * SGU_CHUNK), 0)
        col = lax.broadcasted_iota(jnp.int32, (SGU_CHUNK, SGU_HEADS * SGU_CHUNK), 1)
        wcat = jnp.where((col & (SGU_CHUNK - 1)) <= row, sguw_ref[...], 0.0).astype(BF16)
    rr = lax.broadcasted_iota(jnp.int32, (P, P), 0)
    cc = lax.broadcasted_iota(jnp.int32, (P, P), 1)
    tri_bd = ((rr // C == cc // C) & (rr >= cc)).astype(BF16)
    sel2 = (lax.broadcasted_iota(jnp.int32, (8, 2 * GLA_DV), 0) // 2
            == lax.broadcasted_iota(jnp.int32, (8, 2 * GLA_DV), 1) // GLA_DV).astype(BF16)
    khead = lax.broadcasted_iota(jnp.int32, (1, GLA_KW), 1) // GLA_DK
    vhead = lax.broadcasted_iota(jnp.int32, (1, GLA_VW), 1) // GLA_DV
    arow = lax.broadcasted_iota(jnp.int32, (C, GLA_HEADS * C), 0)
    acol = lax.broadcasted_iota(jnp.int32, (C, GLA_HEADS * C), 1) & (C - 1)
    causal = acol <= arow
    zblk = jnp.zeros((GLA_DK, GLA_DV), F32)

    def gla_unit_stages(r0, e0, read_state, write_state):
        rows = pl.ds(r0, C)
        qe = qe_s[rows, :].astype(BF16)
        ke = ke_s[rows, :]
        kd = kd_s[rows, :]
        vv = z_s[rows, OFF_VG:OFF_VG + GLA_VW]
        ke_bd = jnp.concatenate(
            [jnp.where(khead == hh, ke, 0.0) for hh in range(GLA_HEADS)], axis=0).astype(BF16)
        att_raw = _dot_nt(qe, ke_bd)
        v_bd = jnp.concatenate(
            [jnp.where(vhead == hh, vv, 0.0) for hh in range(GLA_HEADS)], axis=0).astype(BF16)
        kd_st = jnp.concatenate(
            [kd[:, hh * GLA_DK:(hh + 1) * GLA_DK] for hh in range(GLA_HEADS)], axis=0).astype(BF16)
        upd = _dot_tn(kd_st, v_bd)
        yield
        att = jnp.where(causal, att_raw, 0.0).astype(BF16)
        o_intra = _dot(att, v_bd)
        yield
        s_blocks = read_state()
        s_bd = jnp.concatenate(
            [jnp.concatenate([s_blocks[hh] if hc == hh else zblk for hc in range(GLA_HEADS)], axis=1)
             for hh in range(GLA_HEADS)], axis=0)
        o_s[rows, :] = o_intra + _dot(qe, s_bd.astype(BF16))
        ecol = ecol_s[:, pl.ds(e0, GLA_DV)]
        write_state([ecol[hh * GLA_DK:(hh + 1) * GLA_DK] * s_blocks[hh] + upd[:, hh * GLA_DV:(hh + 1) * GLA_DV]
                     for hh in range(GLA_HEADS)])

    def step(gen):
        try:
            next(gen)
        except StopIteration:
            pass

    def part_stages(part):
        r0p = part * P
        prow = slice(r0p, r0p + P)
        t0 = part * TP
        x = x_ref[prow, :]
        h = _rms(x, ng_ref[...]).astype(BF16)
        yield
        for n in range(IN_MAIN // MXU_COLS):
            cs = slice(n * MXU_COLS, (n + 1) * MXU_COLS)
            z_s[prow, cs] = _dot(h, win_ref[:, cs])
            yield
        alow = _dot_nt(h, wina_ref[...]).astype(BF16)

        if NS == 1:
            p3 = z_s[prow, OFF_P:OFF_P + POOL_WIDTH].reshape(NS, TP, POOL_WIDTH)
            pz_s[:, POOL_PAD + t0:POOL_PAD + t0 + TP, :] = p3
            e = pz_s[:, t0:t0 + POOL_PAD + TP, :]
            w2 = e + pltpu.roll(e, 1, 1)
            w4 = w2 + pltpu.roll(w2, 2, 1)
            w8 = w4 + pltpu.roll(w4, 4, 1)
            w16 = w8 + pltpu.roll(w8, 8, 1)
            wsum = jnp.where(grp == 0, w2, jnp.where(grp == 1, w4, jnp.where(grp == 2, w8, w16)))
            wsum = wsum[:, POOL_PAD:POOL_PAD + TP, :]
            pos = pos0 + j * T + t0 + lax.broadcasted_iota(jnp.int32, (1, TP, 1), 1)
            cnt = jnp.minimum(pos + 1, win).astype(F32)
            d = (wsum / cnt - p3).reshape(P, POOL_WIDTH)
        else:
            hist_out = []
            for c in range(POOL_WIDTH // LANES):
                cl = slice(c * LANES, (c + 1) * LANES)
                pz_s[c] = z_s[prow, OFF_P + c * LANES:OFF_P + (c + 1) * LANES]
                ev = [pool0_ref[r][:, cl] for r in range(POOL_BUF)]
                ev += [pz_s[c, pl.ds(t, NS, stride=T), :] for t in range(T)]
                n_e = len(ev)
                w2 = {r: ev[r] + ev[r - 1] for r in range(1, n_e)}
                w4 = {r: w2[r] + w2[r - 2] for r in range(3, n_e)}
                w8 = {r: w4[r] + w4[r - 4] for r in range(7, n_e)}
                w16 = {r: w8[r] + w8[r - 8] for r in range(15, n_e)}
                grp_c, win_c = grp[0][:, cl], win[0][:, cl]
                for t in range(T):
                    r = POOL_BUF + t
                    wsum = jnp.where(grp_c == 0, w2[r], jnp.where(grp_c == 1, w4[r], jnp.where(grp_c == 2, w8[r], w16[r])))
                    cnt = jnp.minimum(pos0 + j * T + t + 1, win_c).astype(F32)
                    pz_s[2 + c, pl.ds(t, NS, stride=T), :] = wsum / cnt - ev[r]
                hist_out.append(ev[n_e - POOL_BUF:])
            for r in range(POOL_BUF):
                pool_o_ref[r] = jnp.concatenate([half[r] for half in hist_out], axis=1)
            d = jnp.concatenate([pz_s[2], pz_s[3]], axis=1)
        a_out = _dot(d.astype(BF16), poolw_ref[...]) * pscale_ref[...]
        yield

        v = z_s[prow, OFF_V:OFF_V + SGU_WIDTH]
        u = z_s[prow, OFF_U:OFF_U + SGU_WIDTH]
        vsq_hi, vsq_lo = _split_bf16(v * v)
        ss = _dot(vsq_hi, head_ones) + _dot(vsq_lo, head_ones)
        vn = v * lax.rsqrt(ss * (1.0 / SGU_HEAD_DIM) + EPS) * sgug_ref[...]
        if emit_vn:
            vn_o_ref[prow, :] = vn
        yield
        if T >= SGU_CHUNK:
            parts = []
            for c in range(P // SGU_CHUNK):
                vc = vn[c * SGU_CHUNK:(c + 1) * SGU_CHUNK]
                stack = jnp.concatenate(
                    [jnp.where(lane_head == hh, vc, 0.0) for hh in range(SGU_HEADS)], axis=0)
                parts.append(_dot(wcat, stack.astype(BF16)) + sgub_ref[...])
            s_gate = jnp.concatenate(parts, axis=0)
        else:
            vn3 = vn.reshape(NS, T, SGU_WIDTH)
            srow = lax.broadcasted_iota(jnp.int32, (T, SGU_WIDTH), 0)
            s3 = jnp.zeros((NS, T, SGU_WIDTH), F32) + sgub_ref[...][None]
            for jj in range(T):
                coef = jnp.where(srow >= jj, sguw_ref[jj], 0.0)
                s3 = s3 + vn3[:, jj:jj + 1, :] * coef[None]
            s_gate = s3.reshape(P, SGU_WIDTH)
        b_out = u * s_gate
        yield

        xg = _dot(alow, wa2_ref[...]) + ba_ref[...]
        lg = (jnp.minimum(xg, 0.0) - jnp.log(1.0 + jnp.exp(-jnp.abs(xg)))) * (1.0 / GLA_GATE_NORM)
        lg_hi, lg_lo = _split_bf16(lg)
        bcum = _dot(tri_bd, lg_hi) + _dot(tri_bd, lg_lo)
        b3 = bcum.reshape(units, C, GLA_KW)
        btot = jnp.broadcast_to(b3[:, C - 1:C, :], (units, C, GLA_KW)).reshape(P, GLA_KW)
        ke_all = z_s[prow, OFF_K:OFF_K + GLA_KW] * jnp.exp(-bcum)
        qe_s[prow, :] = z_s[prow, OFF_Q:OFF_Q + GLA_KW] * (GLA_DK ** -0.5) * jnp.exp(bcum)
        ke_s[prow, :] = ke_all
        kd_s[prow, :] = ke_all * jnp.exp(btot)
        yield
        etot = jnp.exp(b3[:, C - 1, :])
        et_hi = etot.astype(BF16).astype(F32)
        et_lo = etot - et_hi
        zpad = jnp.zeros((4, GLA_KW), F32)
        for pp in range(units // 2):
            a_rows = jnp.concatenate([et_hi[2 * pp:2 * pp + 1], et_lo[2 * pp:2 * pp + 1],
                                      et_hi[2 * pp + 1:2 * pp + 2], et_lo[2 * pp + 1:2 * pp + 2], zpad], axis=0)
            e0 = (part * units + 2 * pp) * GLA_DV
            ecol_s[:, e0:e0 + 2 * GLA_DV] = _dot_tn(a_rows.astype(BF16), sel2)
        yield

        if NS == 1:
            state = [[gla_o_ref[0, hh] for hh in range(GLA_HEADS)]]
            ug = [gla_unit_stages(r0p + c * C, (part * units + c) * GLA_DV,
                                  lambda: state[0], lambda new: state.__setitem__(0, new)) for c in range(units)]
            step(ug[0])
            step(ug[0])
            for c in range(units):
                if c + 1 < units:
                    step(ug[c + 1])
                step(ug[c])
                if c + 1 < units:
                    step(ug[c + 1])
                yield
            for hh in range(GLA_HEADS):
                gla_o_ref[0, hh] = state[0][hh]
        else:
            assert units == NS and NS % SEQ_UNROLL == 0

            def seq_group(g, carry):
                def unit(sq):
                    def write(new):
                        for hh in range(GLA_HEADS):
                            gla_o_ref[sq, hh] = new[hh]
                    return gla_unit_stages(pl.multiple_of(sq * C, C), pl.multiple_of(sq * GLA_DV, GLA_DV),
                                           lambda: [gla_o_ref[sq, hh] for hh in range(GLA_HEADS)], write)
                ug = [unit(g * SEQ_UNROLL + i) for i in range(SEQ_UNROLL)]
                for _ in range(3):
                    for u_gen in ug:
                        step(u_gen)
                return carry

            lax.fori_loop(0, NS // SEQ_UNROLL, seq_group, 0)

        gate = z_s[prow, OFF_G:OFF_G + GLA_VW]
        o_all = o_s[prow, :]
        c_parts = []
        for hh in range(GLA_HEADS):
            sl = slice(hh * GLA_DV, (hh + 1) * GLA_DV)
            gh = gate[:, sl]
            c_parts.append(_rms(o_all[:, sl], glag_ref[...]) * (gh * jax.nn.sigmoid(gh)))

        mix = jnp.concatenate([a_out, b_out] + c_parts, axis=-1).astype(BF16)
        yield
        for n in range(D_MODEL // MXU_COLS):
            cs = slice(n * MXU_COLS, (n + 1) * MXU_COLS)
            xo_ref[prow, cs] = x[:, cs] + _dot(mix, wout_ref[:, cs])
            yield


    n_proj = 1 + IN_MAIN // MXU_COLS
    gens = [part_stages(p) for p in range(R // P)]
    done = [False] * len(gens)

    def advance(p, n=1):
        for _ in range(n):
            if not done[p]:
                try:
                    next(gens[p])
                except StopIteration:
                    done[p] = True

    n_mix = 6 + units if NS == 1 else 7
    n_head = 2
    advance(0, n_head)
    for src, dst in zip(cast_in, cast_out):
        dst[...] = src[...].astype(BF16)
        advance(0, 2)
        n_head += 2
    assert n_head <= n_proj
    advance(0, n_proj - n_head)
    for p in range(len(gens)):
        for _ in range(n_mix):
            advance(p)
            if p + 1 < len(gens):
                advance(p + 1)
            if p > 0:
                advance(p - 1)
    for p in range(len(gens)):
        while not done[p]:
            advance(p)

    if NS == 1:
        pool_o_ref[...] = pz_s[:, T + 1:T + POOL_PAD, :]
        pz_s[:, 0:POOL_PAD, :] = pz_s[:, T:T + POOL_PAD, :]


def _ffn_kernel(xp_ref, xs_ref, g_ref, wg_ref, wu_ref, wd_ref, fg_ref, op_ref, os_ref, *, n_prompt_steps, final):
    def part_stages(x_ref, o_ref, rows):
        x = x_ref[rows, :]
        hf = _rms(x, g_ref[...]).astype(BF16)
        yield
        acc = x
        for lo, hi in FFN_CHUNKS:
            gt = _dot(hf, wg_ref[:, lo:hi])
            yield
            up = _dot(hf, wu_ref[:, lo:hi])
            yield
            act = (gt * jax.nn.sigmoid(gt) * up).astype(BF16)
            acc = acc + _dot(act, wd_ref[lo:hi, :])
            yield
        if final:
            acc = _rms(acc, fg_ref[...])
        o_ref[rows, :] = acc

    def ffn(x_ref, o_ref):
        n_rows = x_ref.shape[0]
        gens = [part_stages(x_ref, o_ref, slice(r, r + PART_ROWS)) for r in range(0, n_rows, PART_ROWS)]
        live = list(gens)
        lag = 0
        while live:
            for g in list(live[:lag + 1]):
                try:
                    next(g)
                except StopIteration:
                    live.remove(g)
            lag += 1

    i = pl.program_id(0)
    pl.when(i < n_prompt_steps)(lambda: ffn(xp_ref, op_ref))
    pl.when(i >= n_prompt_steps)(lambda: ffn(xs_ref, os_ref))


def _wspec(shape, layer):
    nd = len(shape)
    return pl.BlockSpec((None,) + tuple(shape), lambda *g, _l=layer, _n=nd: (_l,) + (0,) * _n,
                        pipeline_mode=pl.Buffered(1))


def _mix_call(layer, x2, pool0, gla0, wts, prev, *, nseq, L, NS, T, pos0, state_layer, emit_vn, seed_shapes=(), cast_srcs=(), win_t_next=None):
    R = NS * T
    n_chunk = L // T
    grid = (nseq // NS, n_chunk)
    (ng, win, wina, poolw, pscale, sgug, sguw, sgub, wa2, ba, glag, wout) = wts
    sl = state_layer
    if NS == 1:
        pool_shape = (DEPTH, nseq, POOL_BUF, POOL_WIDTH)
        pool_spec = lambda lyr: pl.BlockSpec((None, NS, POOL_BUF, POOL_WIDTH), lambda i, j: (lyr, i, 0, 0))
        pool_scratch = pltpu.VMEM((NS, POOL_PAD + T, POOL_WIDTH), F32)
    else:
        pool_shape = (DEPTH, POOL_BUF, nseq, POOL_WIDTH)
        pool_spec = lambda lyr: pl.BlockSpec((None, POOL_BUF, NS, POOL_WIDTH), lambda i, j: (lyr, 0, i, 0))
        pool_scratch = pltpu.VMEM((2 * POOL_WIDTH // LANES, R, LANES), F32)
    in_specs = [
        pl.BlockSpec((R, D_MODEL), lambda i, j: (i * n_chunk + j, 0)),
        pool_spec(sl),
        pl.BlockSpec((None, NS, GLA_HEADS, GLA_DK, GLA_DV), lambda i, j: (sl, i, 0, 0, 0)),
        _wspec((1, D_MODEL), layer),
        pl.BlockSpec((D_MODEL, IN_MAIN), lambda i, j: (0, 0), pipeline_mode=pl.Buffered(1)),
        _wspec((GLA_RANK_PAD, D_MODEL), layer),
        _wspec((POOL_WIDTH, POOL_WIDTH), layer),
        _wspec((1, POOL_WIDTH), layer),
        _wspec((1, SGU_WIDTH), layer),
        _wspec(sguw.shape[1:], layer),
        _wspec(sgub.shape[1:], layer),
        _wspec((GLA_RANK_PAD, GLA_KW), layer),
        _wspec((1, GLA_KW), layer),
        _wspec((1, GLA_DV), layer),
        pl.BlockSpec((D_MODEL, D_MODEL), lambda i, j: (0, 0), pipeline_mode=pl.Buffered(1)),
    ]
    operands = [x2, pool0, gla0, ng, win, wina, poolw, pscale, sgug, sguw, sgub, wa2, ba, glag, wout]
    out_specs = [
        pl.BlockSpec((R, D_MODEL), lambda i, j: (i * n_chunk + j, 0)),
        pool_spec(layer),
        pl.BlockSpec((None, NS, GLA_HEADS, GLA_DK, GLA_DV), lambda i, j: (layer, i, 0, 0, 0)),
    ]
    out_shape = [
        jax.ShapeDtypeStruct((nseq * L, D_MODEL), F32),
        jax.ShapeDtypeStruct(pool_shape, F32),
        jax.ShapeDtypeStruct((DEPTH, nseq, GLA_HEADS, GLA_DK, GLA_DV), F32),
    ]
    if emit_vn:
        out_specs.append(pl.BlockSpec((None, R, SGU_WIDTH), lambda i, j: (layer, i * n_chunk + j, 0)))
        out_shape.append(jax.ShapeDtypeStruct((DEPTH, nseq * L, SGU_WIDTH), F32))
    assert len(prev) == len(out_shape) - 1
    n_steps = grid[0] * grid[1]
    for shp, ax in seed_shapes:
        blk = tuple(d // n_steps if a == ax else d for a, d in enumerate(shp))
        out_specs.append(pl.BlockSpec(
            blk, lambda i, j, _n=len(shp), _ax=ax: tuple(i * n_chunk + j if a == _ax else 0 for a in range(_n))))
        out_shape.append(jax.ShapeDtypeStruct(shp, F32))
    for w, w_layer in cast_srcs:
        rows, cols = w.shape[1] // n_steps, w.shape[2]
        operands.append(w)
        in_specs.append(pl.BlockSpec((None, rows, cols), lambda i, j, _l=w_layer: (_l, i * n_chunk + j, 0)))
        out_specs.append(pl.BlockSpec((rows, cols), lambda i, j: (i * n_chunk + j, 0)))
        out_shape.append(jax.ShapeDtypeStruct(w.shape[1:], BF16))
    if win_t_next is not None:
        n_blk = IN_MAIN // MXU_COLS
        assert n_steps >= n_blk
        blk_idx = lambda i, j: jnp.minimum(i * n_chunk + j, n_blk - 1)
        operands.append(win_t_next)
        in_specs.append(pl.BlockSpec((None, MXU_COLS, D_MODEL), lambda i, j: (layer + 1, blk_idx(i, j), 0)))
        out_specs.append(pl.BlockSpec((D_MODEL, MXU_COLS), lambda i, j: (0, blk_idx(i, j))))
        out_shape.append(jax.ShapeDtypeStruct((D_MODEL, IN_MAIN), BF16))
    aliases = {}
    for k, arr in enumerate(prev):
        aliases[len(operands)] = 1 + k
        operands.append(arr)
        in_specs.append(pl.BlockSpec(memory_space=pl.ANY))
    scratch = [
        pltpu.VMEM((R, IN_MAIN), F32),
        pool_scratch,
        pltpu.VMEM((R, GLA_KW), F32),
        pltpu.VMEM((R, GLA_KW), F32),
        pltpu.VMEM((R, GLA_KW), F32),
        pltpu.VMEM((GLA_KW, (R // min(GLA_CHUNK, T)) * GLA_DV), F32),
        pltpu.VMEM((R, GLA_VW), F32),
    ]
    outs = pl.pallas_call(
        functools.partial(_mix_kernel, NS=NS, T=T, pos0=pos0, emit_vn=emit_vn, n_prev=len(prev),
                          n_cast=len(cast_srcs), cast_win=win_t_next is not None),
        grid=grid, in_specs=in_specs, out_specs=out_specs, out_shape=out_shape,
        scratch_shapes=scratch, input_output_aliases=aliases,
        compiler_params=pltpu.CompilerParams(
            dimension_semantics=("arbitrary", "arbitrary"), vmem_limit_bytes=VMEM_LIMIT),
        name=f"mix_T{T}",
    )(*operands)
    n_state, n_seed = len(prev), len(seed_shapes)
    return (outs[0], tuple(outs[1:1 + n_state]), tuple(outs[1 + n_state:1 + n_state + n_seed]),
            tuple(outs[1 + n_state + n_seed:]))


def _ffn_call(layer, xp, xs, wts, final_g, *, TM, final):
    n_p, n_s = xp.shape[0] // TM, xs.shape[0] // TM
    g, wg, wu, wd = wts
    p_idx = lambda i: (jnp.minimum(i, n_p - 1), 0)
    s_idx = lambda i: (jnp.maximum(i - n_p, 0), 0)
    return pl.pallas_call(
        functools.partial(_ffn_kernel, n_prompt_steps=n_p, final=final),
        grid=(n_p + n_s,),
        in_specs=[
            pl.BlockSpec((TM, D_MODEL), p_idx),
            pl.BlockSpec((TM, D_MODEL), s_idx),
            _wspec((1, D_MODEL), layer),
            pl.BlockSpec((D_MODEL, D_FF), lambda i: (0, 0), pipeline_mode=pl.Buffered(1)),
            pl.BlockSpec((D_MODEL, D_FF), lambda i: (0, 0), pipeline_mode=pl.Buffered(1)),
            pl.BlockSpec((D_FF, D_MODEL), lambda i: (0, 0), pipeline_mode=pl.Buffered(1)),
            pl.BlockSpec((1, D_MODEL), lambda i: (0, 0)),
        ],
        out_specs=[pl.BlockSpec((TM, D_MODEL), p_idx), pl.BlockSpec((TM, D_MODEL), s_idx)],
        out_shape=[jax.ShapeDtypeStruct(xp.shape, F32), jax.ShapeDtypeStruct(xs.shape, F32)],
        compiler_params=pltpu.CompilerParams(
            dimension_semantics=("arbitrary",), vmem_limit_bytes=VMEM_LIMIT),
        name="ffn",
    )(xp, xs, g, wg, wu, wd, final_g)


def _cast_win0_kernel(wt_ref, o_ref):
    o_ref[...] = wt_ref[...].T.astype(BF16)


def _cast_win0(win_t):
    return pl.pallas_call(
        _cast_win0_kernel,
        grid=(IN_MAIN // CAST0_COLS,),
        in_specs=[pl.BlockSpec((None, CAST0_COLS, D_MODEL), lambda s: (0, s, 0))],
        out_specs=pl.BlockSpec((D_MODEL, CAST0_COLS), lambda s: (0, s)),
        out_shape=jax.ShapeDtypeStruct((D_MODEL, IN_MAIN), BF16),
        compiler_params=pltpu.CompilerParams(dimension_semantics=("arbitrary",)),
        name="cast_win0",
    )(win_t)


def kernel(x_prompt, x_sample, state_pool, state_gla, attn_norm_g, w_in, pool_w, pool_scale, sgu_norm_g, sgu_ws, sgu_b, gla_wa2, gla_ba, gla_norm_g, w_out, ffn_norm_g, w_gate, w_up, w_down, final_norm_g):
    bp, seq, _ = x_prompt.shape
    bs, dseq, _ = x_sample.shape
    assert seq % PROMPT_TILE == 0 and dseq == 8 and bs % SAMPLE_SEQS == 0

    ng = attn_norm_g[:, None, :]
    win_t = w_in.transpose(0, 2, 1)
    win = _cast_win0(win_t)
    wina = jnp.pad(win_t[:, IN_MAIN:], ((0, 0), (0, GLA_RANK_PAD - GLA_RANK), (0, 0))).astype(BF16)
    eye_g = jnp.eye(len(POOL_WINDOWS), dtype=F32)
    poolw = jnp.einsum('lgcd,gh->lgchd', pool_w, eye_g).reshape(DEPTH, POOL_WIDTH, POOL_WIDTH).astype(BF16)
    pscale = pool_scale[:, None, :]
    sgug = sgu_norm_g.reshape(DEPTH, 1, SGU_WIDTH)
    sguw_cat = sgu_ws.transpose(0, 2, 1, 3).reshape(DEPTH, SGU_CHUNK, SGU_HEADS * SGU_CHUNK)
    sgub_tile = jnp.repeat(sgu_b.transpose(0, 2, 1), SGU_HEAD_DIM, axis=-1)
    sguw_dec = jnp.repeat(sgu_ws[:, :, :dseq, :dseq].transpose(0, 3, 2, 1), SGU_HEAD_DIM, axis=-1)
    sgub_dec = sgub_tile[:, :dseq]
    wa2 = jnp.pad(gla_wa2, ((0, 0), (0, GLA_RANK_PAD - GLA_RANK), (0, 0))).astype(BF16)
    ba = gla_ba[:, None, :]
    glag = gla_norm_g[:, None, :]
    wout = w_out[0].astype(BF16)
    fng = ffn_norm_g[:, None, :]
    fin = final_norm_g[None, :]

    xp = x_prompt.reshape(bp * seq, D_MODEL)
    xs = x_sample.reshape(bs * dseq, D_MODEL)
    pool0_p = jnp.zeros((1, bp, POOL_BUF, POOL_WIDTH), F32)
    gla0_p = jnp.zeros((1, bp, GLA_HEADS, GLA_DK, GLA_DV), F32)

    st_p = (jnp.zeros((DEPTH, bp, POOL_BUF, POOL_WIDTH), F32), jnp.zeros((DEPTH, bp, GLA_HEADS, GLA_DK, GLA_DV), F32))
    seeds_s = (((DEPTH, POOL_BUF, bs, POOL_WIDTH), 2), ((DEPTH, bs, GLA_HEADS, GLA_DK, GLA_DV), 1),
               ((DEPTH, bs * dseq, SGU_WIDTH), 1))
    pool_s_in = state_pool.transpose(0, 2, 1, 3)
    st_s = None
    for l in range(DEPTH):
        common = (ng, win, wina, poolw, pscale, sgug)
        tail = (wa2, ba, glag, wout)
        nxt = l + 1 < DEPTH
        xp, st_p, seeds, cast = _mix_call(l, xp, pool0_p, gla0_p, common + (sguw_cat, sgub_tile) + tail, st_p,
                                    nseq=bp, L=seq, NS=1, T=PROMPT_TILE, pos0=0, state_layer=0, emit_vn=False,
                                    seed_shapes=seeds_s if l == 0 else (),
                                    cast_srcs=((w_gate, l), (w_up, l), (w_down, l)) + (((w_out, l + 1),) if nxt else ()),
                                    win_t_next=win_t if nxt else None)
        st_s = seeds if l == 0 else st_s
        xs, st_s, _, _ = _mix_call(l, xs, pool_s_in, state_gla, common + (sguw_dec, sgub_dec) + tail, st_s,
                                nseq=bs, L=dseq, NS=SAMPLE_SEQS, T=dseq, pos0=PAST_LEN, state_layer=l, emit_vn=True)
        xp, xs = _ffn_call(l, xp, xs, (fng,) + cast[:3], fin, TM=FFN_ROWS, final=l == DEPTH - 1)
        if nxt:
            wout, win = cast[3], cast[4]

    return (xp.reshape(bp, seq, D_MODEL), xs.reshape(bs, dseq, D_MODEL),
            st_p[0], st_p[1], st_s[0].transpose(0, 2, 1, 3), st_s[1], st_s[2].reshape(DEPTH, bs, dseq, SGU_WIDTH))
```

```python
import functools

import jax
import jax.numpy as jnp
from jax import lax
from jax.experimental import pallas as pl
from jax.experimental.pallas import tpu as pltpu

F32 = jnp.float32
BF16 = jnp.bfloat16

D_MODEL = 1024
DEPTH = 4
POOL_WIDTH = 256
POOL_WINDOWS = (2, 4, 8, 16)
POOL_GROUP_DIM = 64
POOL_BUF = 15
POOL_PAD = 16
SGU_WIDTH = 256
SGU_HEADS = 4
SGU_HEAD_DIM = 64
SGU_CHUNK = 128
GLA_HEADS = 4
GLA_DK = 64
GLA_DV = 128
GLA_KW = GLA_HEADS * GLA_DK
GLA_VW = GLA_HEADS * GLA_DV
GLA_RANK = 16
GLA_RANK_PAD = 128
GLA_GATE_NORM = 16.0
GLA_CHUNK = 64
MXU_COLS = 256
LANES = 128
PART_ROWS = 256
SEQ_UNROLL = 32
D_FF = 2816
EPS = 1e-6

OFF_P, OFF_U, OFF_V, OFF_Q, OFF_K, OFF_VG, OFF_G, OFF_A = 0, 256, 512, 768, 1024, 1280, 1792, 2304
IN_MAIN = 2304
CAST0_COLS = 768
FFN_CHUNKS = ((0, 1536), (1536, 2816))

PAST_LEN = 16384
PROMPT_TILE = 1024
SAMPLE_SEQS = 32
FFN_ROWS = 512
VMEM_LIMIT = 56 * 1024 * 1024


def _rms(x, g):
    return x * lax.rsqrt(jnp.mean(x * x, axis=-1, keepdims=True) + EPS) * g


def _dot(a, b):
    return jnp.dot(a, b, preferred_element_type=F32)


def _dot_nt(a, b):
    return lax.dot_general(a, b, (((1,), (1,)), ((), ())), preferred_element_type=F32)


def _dot_tn(a, b):
    return lax.dot_general(a, b, (((0,), (0,)), ((), ())), preferred_element_type=F32)


def _split_bf16(x):
    hi = x.astype(BF16)
    lo = (x - hi.astype(F32)).astype(BF16)
    return hi, lo


def _mix_kernel(*refs, NS, T, pos0, emit_vn, n_prev, n_cast, cast_win):
    (x_ref, pool0_ref, gla0_ref, ng_ref, win_ref, wina_ref, poolw_ref, pscale_ref,
     sgug_ref, sguw_ref, sgub_ref, wa2_ref, ba_ref, glag_ref, wout_ref) = refs[:15]
    cast_in = refs[15:15 + n_cast]
    wint_ref = refs[15 + n_cast] if cast_win else None
    refs = refs[15 + n_cast + cast_win + n_prev:]
    xo_ref, pool_o_ref, gla_o_ref = refs[:3]
    vn_o_ref = refs[3] if emit_vn else None
    n_tail = 7 + cast_win
    seed_refs = refs[3 + emit_vn:len(refs) - n_tail - n_cast]
    cast_out = refs[len(refs) - n_tail - n_cast:len(refs) - n_tail]
    win_next_ref = refs[len(refs) - n_tail] if cast_win else None
    z_s, pz_s, qe_s, ke_s, kd_s, ecol_s, o_s = refs[-7:]
    j = pl.program_id(1)
    R = NS * T
    P = PART_ROWS
    TP = P // NS
    C = min(GLA_CHUNK, T)
    units = P // C
    assert R % P == 0 and (NS == 1 or R == P)

    @pl.when(j == 0)
    def _():
        if NS == 1:
            pz_s[:, 0:1, :] = jnp.zeros((NS, 1, POOL_WIDTH), F32)
            pz_s[:, 1:POOL_PAD, :] = pool0_ref[...]
        gla_o_ref[...] = gla0_ref[...]

    for seed_ref in seed_refs:
        seed_ref[...] = jnp.zeros(seed_ref.shape, F32)
    if cast_win:
        win_next_ref[...] = wint_ref[...].T.astype(BF16)

    grp = lax.broadcasted_iota(jnp.int32, (1, 1, POOL_WIDTH), 2) // POOL_GROUP_DIM
    win = jnp.where(grp == 0, 2, jnp.where(grp == 1, 4, jnp.where(grp == 2, 8, 16)))
    r_i = lax.broadcasted_iota(jnp.int32, (SGU_WIDTH, SGU_WIDTH), 0) // SGU_HEAD_DIM
    c_i = lax.broadcasted_iota(jnp.int32, (SGU_WIDTH, SGU_WIDTH), 1) // SGU_HEAD_DIM
    head_ones = (r_i == c_i).astype(BF16)
    lane_head = lax.broadcasted_iota(jnp.int32, (1, SGU_WIDTH), 1) // SGU_HEAD_DIM
    if T >= SGU_CHUNK:
        row = lax.broadcasted_iota(jnp.int32, (SGU_CHUNK, SGU_HEADS * SGU_CHUNK), 0)
        col = lax.broadcasted_iota(jnp.int32, (SGU_CHUNK, SGU_HEADS * SGU_CHUNK), 1)
        wcat = jnp.where((col & (SGU_CHUNK - 1)) <= row, sguw_ref[...], 0.0).astype(BF16)
    rr = lax.broadcasted_iota(jnp.int32, (P, P), 0)
    cc = lax.broadcasted_iota(jnp.int32, (P, P), 1)
    tri_bd = ((rr // C == cc // C) & (rr >= cc)).astype(BF16)
    sel2 = (lax.broadcasted_iota(jnp.int32, (8, 2 * GLA_DV), 0) // 2
            == lax.broadcasted_iota(jnp.int32, (8, 2 * GLA_DV), 1) // GLA_DV).astype(BF16)
    khead = lax.broadcasted_iota(jnp.int32, (1, GLA_KW), 1) // GLA_DK
    vhead = lax.broadcasted_iota(jnp.int32, (1, GLA_VW), 1) // GLA_DV
    arow = lax.broadcasted_iota(jnp.int32, (C, GLA_HEADS * C), 0)
    acol = lax.broadcasted_iota(jnp.int32, (C, GLA_HEADS * C), 1) & (C - 1)
    causal = acol <= arow
    zblk = jnp.zeros((GLA_DK, GLA_DV), F32)

    def gla_unit_stages(r0, e0, read_state, write_state):
        rows = pl.ds(r0, C)
        qe = qe_s[rows, :].astype(BF16)
        ke = ke_s[rows, :]
        kd = kd_s[rows, :]
        vv = z_s[rows, OFF_VG:OFF_VG + GLA_VW]
        ke_bd = jnp.concatenate(
            [jnp.where(khead == hh, ke, 0.0) for hh in range(GLA_HEADS)], axis=0).astype(BF16)
        att_raw = _dot_nt(qe, ke_bd)
        v_bd = jnp.concatenate(
            [jnp.where(vhead == hh, vv, 0.0) for hh in range(GLA_HEADS)], axis=0).astype(BF16)
        kd_st = jnp.concatenate(
            [kd[:, hh * GLA_DK:(hh + 1) * GLA_DK] for hh in range(GLA_HEADS)], axis=0).astype(BF16)
        upd = _dot_tn(kd_st, v_bd)
        yield
        att = jnp.where(causal, att_raw, 0.0).astype(BF16)
        o_intra = _dot(att, v_bd)
        yield
        s_blocks = read_state()
        s_bd = jnp.concatenate(
            [jnp.concatenate([s_blocks[hh] if hc == hh else zblk for hc in range(GLA_HEADS)], axis=1)
             for hh in range(GLA_HEADS)], axis=0)
        o_s[rows, :] = o_intra + _dot(qe, s_bd.astype(BF16))
        ecol = ecol_s[:, pl.ds(e0, GLA_DV)]
        write_state([ecol[hh * GLA_DK:(hh + 1) * GLA_DK] * s_blocks[hh] + upd[:, hh * GLA_DV:(hh + 1) * GLA_DV]
                     for hh in range(GLA_HEADS)])

    def step(gen):
        try:
            next(gen)
        except StopIteration:
            pass

    def part_stages(part):
        r0p = part * P
        prow = slice(r0p, r0p + P)
        t0 = part * TP
        x = x_ref[prow, :]
        h = _rms(x, ng_ref[...]).astype(BF16)
        yield
        for n in range(IN_MAIN // MXU_COLS):
            cs = slice(n * MXU_COLS, (n + 1) * MXU_COLS)
            z_s[prow, cs] = _dot(h, win_ref[:, cs])
            yield
        alow = _dot_nt(h, wina_ref[...]).astype(BF16)

        if NS == 1:
            p3 = z_s[prow, OFF_P:OFF_P + POOL_WIDTH].reshape(NS, TP, POOL_WIDTH)
            pz_s[:, POOL_PAD + t0:POOL_PAD + t0 + TP, :] = p3
            e = pz_s[:, t0:t0 + POOL_PAD + TP, :]
            w2 = e + pltpu.roll(e, 1, 1)
            w4 = w2 + pltpu.roll(w2, 2, 1)
            w8 = w4 + pltpu.roll(w4, 4, 1)
            w16 = w8 + pltpu.roll(w8, 8, 1)
            yield
            wsum = jnp.where(grp == 0, w2, jnp.where(grp == 1, w4, jnp.where(grp == 2, w8, w16)))
            wsum = wsum[:, POOL_PAD:POOL_PAD + TP, :]
            pos = pos0 + j * T + t0 + lax.broadcasted_iota(jnp.int32, (1, TP, 1), 1)
            cnt = jnp.minimum(pos + 1, win).astype(F32)
            d = (wsum / cnt - p3).reshape(P, POOL_WIDTH)
        else:
            hist_out = []
            for c in range(POOL_WIDTH // LANES):
                cl = slice(c * LANES, (c + 1) * LANES)
                pz_s[c] = z_s[prow, OFF_P + c * LANES:OFF_P + (c + 1) * LANES]
                ev = [pool0_ref[r][:, cl] for r in range(POOL_BUF)]
                ev += [pz_s[c, pl.ds(t, NS, stride=T), :] for t in range(T)]
                n_e = len(ev)
                w2 = {r: ev[r] + ev[r - 1] for r in range(1, n_e)}
                w4 = {r: w2[r] + w2[r - 2] for r in range(3, n_e)}
                w8 = {r: w4[r] + w4[r - 4] for r in range(7, n_e)}
                w16 = {r: w8[r] + w8[r - 8] for r in range(15, n_e)}
                grp_c, win_c = grp[0][:, cl], win[0][:, cl]
                for t in range(T):
                    r = POOL_BUF + t
                    wsum = jnp.where(grp_c == 0, w2[r], jnp.where(grp_c == 1, w4[r], jnp.where(grp_c == 2, w8[r], w16[r])))
                    cnt = jnp.minimum(pos0 + j * T + t + 1, win_c).astype(F32)
                    pz_s[2 + c, pl.ds(t, NS, stride=T), :] = wsum / cnt - ev[r]
                hist_out.append(ev[n_e - POOL_BUF:])
            for r in range(POOL_BUF):
                pool_o_ref[r] = jnp.concatenate([half[r] for half in hist_out], axis=1)
            d = jnp.concatenate([pz_s[2], pz_s[3]], axis=1)
        a_out = _dot(d.astype(BF16), poolw_ref[...]) * pscale_ref[...]
        yield

        v = z_s[prow, OFF_V:OFF_V + SGU_WIDTH]
        u = z_s[prow, OFF_U:OFF_U + SGU_WIDTH]
        vsq_hi, vsq_lo = _split_bf16(v * v)
        ss = _dot(vsq_hi, head_ones) + _dot(vsq_lo, head_ones)
        vn = v * lax.rsqrt(ss * (1.0 / SGU_HEAD_DIM) + EPS) * sgug_ref[...]
        if emit_vn:
            vn_o_ref[prow, :] = vn
        yield
        if T >= SGU_CHUNK:
            parts = []
            for c in range(P // SGU_CHUNK):
                vc = vn[c * SGU_CHUNK:(c + 1) * SGU_CHUNK]
                stack = jnp.concatenate(
                    [jnp.where(lane_head == hh, vc, 0.0) for hh in range(SGU_HEADS)], axis=0)
                parts.append(_dot(wcat, stack.astype(BF16)) + sgub_ref[...])
            s_gate = jnp.concatenate(parts, axis=0)
        else:
            vn3 = vn.reshape(NS, T, SGU_WIDTH)
            srow = lax.broadcasted_iota(jnp.int32, (T, SGU_WIDTH), 0)
            s3 = jnp.zeros((NS, T, SGU_WIDTH), F32) + sgub_ref[...][None]
            for jj in range(T):
                coef = jnp.where(srow >= jj, sguw_ref[jj], 0.0)
                s3 = s3 + vn3[:, jj:jj + 1, :] * coef[None]
            s_gate = s3.reshape(P, SGU_WIDTH)
        b_out = u * s_gate
        yield

        xg = _dot(alow, wa2_ref[...]) + ba_ref[...]
        lg = (jnp.minimum(xg, 0.0) - jnp.log(1.0 + jnp.exp(-jnp.abs(xg)))) * (1.0 / GLA_GATE_NORM)
        lg_hi, lg_lo = _split_bf16(lg)
        bcum = _dot(tri_bd, lg_hi) + _dot(tri_bd, lg_lo)
        b3 = bcum.reshape(units, C, GLA_KW)
        btot = jnp.broadcast_to(b3[:, C - 1:C, :], (units, C, GLA_KW)).reshape(P, GLA_KW)
        yield
        ke_all = z_s[prow, OFF_K:OFF_K + GLA_KW] * jnp.exp(-bcum)
        qe_s[prow, :] = z_s[prow, OFF_Q:OFF_Q + GLA_KW] * (GLA_DK ** -0.5) * jnp.exp(bcum)
        ke_s[prow, :] = ke_all
        kd_s[prow, :] = ke_all * jnp.exp(btot)
        yield
        etot = jnp.exp(b3[:, C - 1, :])
        et_hi = etot.astype(BF16).astype(F32)
        et_lo = etot - et_hi
        zpad = jnp.zeros((4, GLA_KW), F32)
        for pp in range(units // 2):
            a_rows = jnp.concatenate([et_hi[2 * pp:2 * pp + 1], et_lo[2 * pp:2 * pp + 1],
                                      et_hi[2 * pp + 1:2 * pp + 2], et_lo[2 * pp + 1:2 * pp + 2], zpad], axis=0)
            e0 = (part * units + 2 * pp) * GLA_DV
            ecol_s[:, e0:e0 + 2 * GLA_DV] = _dot_tn(a_rows.astype(BF16), sel2)
        yield

        if NS == 1:
            state = [[gla_o_ref[0, hh] for hh in range(GLA_HEADS)]]
            ug = [gla_unit_stages(r0p + c * C, (part * units + c) * GLA_DV,
                                  lambda: state[0], lambda new: state.__setitem__(0, new)) for c in range(units)]
            step(ug[0])
            step(ug[0])
            for c in range(units):
                if c + 1 < units:
                    step(ug[c + 1])
                step(ug[c])
                if c + 1 < units:
                    step(ug[c + 1])
                yield
            for hh in range(GLA_HEADS):
                gla_o_ref[0, hh] = state[0][hh]
        else:
            assert units == NS and NS % SEQ_UNROLL == 0

            def seq_group(g, carry):
                def unit(sq):
                    def write(new):
                        for hh in range(GLA_HEADS):
                            gla_o_ref[sq, hh] = new[hh]
                    return gla_unit_stages(pl.multiple_of(sq * C, C), pl.multiple_of(sq * GLA_DV, GLA_DV),
                                           lambda: [gla_o_ref[sq, hh] for hh in range(GLA_HEADS)], write)
                ug = [unit(g * SEQ_UNROLL + i) for i in range(SEQ_UNROLL)]
                for _ in range(3):
                    for u_gen in ug:
                        step(u_gen)
                return carry

            lax.fori_loop(0, NS // SEQ_UNROLL, seq_group, 0)

        gate = z_s[prow, OFF_G:OFF_G + GLA_VW]
        o_all = o_s[prow, :]
        c_parts = []
        for hh in range(GLA_HEADS):
            sl = slice(hh * GLA_DV, (hh + 1) * GLA_DV)
            gh = gate[:, sl]
            c_parts.append(_rms(o_all[:, sl], glag_ref[...]) * (gh * jax.nn.sigmoid(gh)))
            if hh == GLA_HEADS // 2 - 1:
                yield

        mix = jnp.concatenate([a_out, b_out] + c_parts, axis=-1).astype(BF16)
        yield
        for n in range(D_MODEL // MXU_COLS):
            cs = slice(n * MXU_COLS, (n + 1) * MXU_COLS)
            xo_ref[prow, cs] = x[:, cs] + _dot(mix, wout_ref[:, cs])
            yield


    n_proj = 1 + IN_MAIN // MXU_COLS
    gens = [part_stages(p) for p in range(R // P)]
    done = [False] * len(gens)

    def advance(p, n=1):
        for _ in range(n):
            if not done[p]:
                try:
                    next(gens[p])
                except StopIteration:
                    done[p] = True

    n_mix = 9 + units if NS == 1 else 9
    n_head = 2
    advance(0, n_head)
    for src, dst in zip(cast_in, cast_out):
        dst[...] = src[...].astype(BF16)
        advance(0, 2)
        n_head += 2
    assert n_head <= n_proj
    advance(0, n_proj - n_head)
    for p in range(len(gens)):
        for _ in range(n_mix):
            advance(p)
            if p + 1 < len(gens):
                advance(p + 1)
            if p > 0:
                advance(p - 1)
    for p in range(len(gens)):
        while not done[p]:
            advance(p)

    if NS == 1:
        pool_o_ref[...] = pz_s[:, T + 1:T + POOL_PAD, :]
        pz_s[:, 0:POOL_PAD, :] = pz_s[:, T:T + POOL_PAD, :]


def _ffn_kernel(xp_ref, xs_ref, g_ref, wg_ref, wu_ref, wd_ref, fg_ref, op_ref, os_ref, *, n_prompt_steps, final):
    def part_stages(x_ref, o_ref, rows):
        x = x_ref[rows, :]
        hf = _rms(x, g_ref[...]).astype(BF16)
        yield
        acc = x
        for lo, hi in FFN_CHUNKS:
            gt = _dot(hf, wg_ref[:, lo:hi])
            yield
            up = _dot(hf, wu_ref[:, lo:hi])
            yield
            act = (gt * jax.nn.sigmoid(gt) * up).astype(BF16)
            acc = acc + _dot(act, wd_ref[lo:hi, :])
            yield
        if final:
            acc = _rms(acc, fg_ref[...])
        o_ref[rows, :] = acc

    def ffn(x_ref, o_ref):
        n_rows = x_ref.shape[0]
        gens = [part_stages(x_ref, o_ref, slice(r, r + PART_ROWS)) for r in range(0, n_rows, PART_ROWS)]
        live = list(gens)
        lag = 0
        while live:
            for g in list(live[:lag + 1]):
                try:
                    next(g)
                except StopIteration:
                    live.remove(g)
            lag += 1

    i = pl.program_id(0)
    pl.when(i < n_prompt_steps)(lambda: ffn(xp_ref, op_ref))
    pl.when(i >= n_prompt_steps)(lambda: ffn(xs_ref, os_ref))


def _wspec(shape, layer):
    nd = len(shape)
    return pl.BlockSpec((None,) + tuple(shape), lambda *g, _l=layer, _n=nd: (_l,) + (0,) * _n,
                        pipeline_mode=pl.Buffered(1))


def _mix_call(layer, x2, pool0, gla0, wts, prev, *, nseq, L, NS, T, pos0, state_layer, emit_vn, seed_shapes=(), cast_srcs=(), win_t_next=None):
    R = NS * T
    n_chunk = L // T
    grid = (nseq // NS, n_chunk)
    (ng, win, wina, poolw, pscale, sgug, sguw, sgub, wa2, ba, glag, wout) = wts
    sl = state_layer
    if NS == 1:
        pool_shape = (DEPTH, nseq, POOL_BUF, POOL_WIDTH)
        pool_spec = lambda lyr: pl.BlockSpec((None, NS, POOL_BUF, POOL_WIDTH), lambda i, j: (lyr, i, 0, 0))
        pool_scratch = pltpu.VMEM((NS, POOL_PAD + T, POOL_WIDTH), F32)
    else:
        pool_shape = (DEPTH, POOL_BUF, nseq, POOL_WIDTH)
        pool_spec = lambda lyr: pl.BlockSpec((None, POOL_BUF, NS, POOL_WIDTH), lambda i, j: (lyr, 0, i, 0))
        pool_scratch = pltpu.VMEM((2 * POOL_WIDTH // LANES, R, LANES), F32)
    in_specs = [
        pl.BlockSpec((R, D_MODEL), lambda i, j: (i * n_chunk + j, 0)),
        pool_spec(sl),
        pl.BlockSpec((None, NS, GLA_HEADS, GLA_DK, GLA_DV), lambda i, j: (sl, i, 0, 0, 0)),
        _wspec((1, D_MODEL), layer),
        pl.BlockSpec((D_MODEL, IN_MAIN), lambda i, j: (0, 0), pipeline_mode=pl.Buffered(1)),
        _wspec((GLA_RANK_PAD, D_MODEL), layer),
        _wspec((POOL_WIDTH, POOL_WIDTH), layer),
        _wspec((1, POOL_WIDTH), layer),
        _wspec((1, SGU_WIDTH), layer),
        _wspec(sguw.shape[1:], layer),
        _wspec(sgub.shape[1:], layer),
        _wspec((GLA_RANK_PAD, GLA_KW), layer),
        _wspec((1, GLA_KW), layer),
        _wspec((1, GLA_DV), layer),
        pl.BlockSpec((D_MODEL, D_MODEL), lambda i, j: (0, 0), pipeline_mode=pl.Buffered(1)),
    ]
    operands = [x2, pool0, gla0, ng, win, wina, poolw, pscale, sgug, sguw, sgub, wa2, ba, glag, wout]
    out_specs = [
        pl.BlockSpec((R, D_MODEL), lambda i, j: (i * n_chunk + j, 0)),
        pool_spec(layer),
        pl.BlockSpec((None, NS, GLA_HEADS, GLA_DK, GLA_DV), lambda i, j: (layer, i, 0, 0, 0)),
    ]
    out_shape = [
        jax.ShapeDtypeStruct((nseq * L, D_MODEL), F32),
        jax.ShapeDtypeStruct(pool_shape, F32),
        jax.ShapeDtypeStruct((DEPTH, nseq, GLA_HEADS, GLA_DK, GLA_DV), F32),
    ]
    if emit_vn:
        out_specs.append(pl.BlockSpec((None, R, SGU_WIDTH), lambda i, j: (layer, i * n_chunk + j, 0)))
        out_shape.append(jax.ShapeDtypeStruct((DEPTH, nseq * L, SGU_WIDTH), F32))
    assert len(prev) == len(out_shape) - 1
    n_steps = grid[0] * grid[1]
    for shp, ax in seed_shapes:
        blk = tuple(d // n_steps if a == ax else d for a, d in enumerate(shp))
        out_specs.append(pl.BlockSpec(
            blk, lambda i, j, _n=len(shp), _ax=ax: tuple(i * n_chunk + j if a == _ax else 0 for a in range(_n))))
        out_shape.append(jax.ShapeDtypeStruct(shp, F32))
    for w, w_layer in cast_srcs:
        rows, cols = w.shape[1] // n_steps, w.shape[2]
        operands.append(w)
        in_specs.append(pl.BlockSpec((None, rows, cols), lambda i, j, _l=w_layer: (_l, i * n_chunk + j, 0)))
        out_specs.append(pl.BlockSpec((rows, cols), lambda i, j: (i * n_chunk + j, 0)))
        out_shape.append(jax.ShapeDtypeStruct(w.shape[1:], BF16))
    if win_t_next is not None:
        n_blk = IN_MAIN // MXU_COLS
        assert n_steps >= n_blk
        blk_idx = lambda i, j: jnp.minimum(i * n_chunk + j, n_blk - 1)
        operands.append(win_t_next)
        in_specs.append(pl.BlockSpec((None, MXU_COLS, D_MODEL), lambda i, j: (layer + 1, blk_idx(i, j), 0)))
        out_specs.append(pl.BlockSpec((D_MODEL, MXU_COLS), lambda i, j: (0, blk_idx(i, j))))
        out_shape.append(jax.ShapeDtypeStruct((D_MODEL, IN_MAIN), BF16))
    aliases = {}
    for k, arr in enumerate(prev):
        aliases[len(operands)] = 1 + k
        operands.append(arr)
        in_specs.append(pl.BlockSpec(memory_space=pl.ANY))
    scratch = [
        pltpu.VMEM((R, IN_MAIN), F32),
        pool_scratch,
        pltpu.VMEM((R, GLA_KW), F32),
        pltpu.VMEM((R, GLA_KW), F32),
        pltpu.VMEM((R, GLA_KW), F32),
        pltpu.VMEM((GLA_KW, (R // min(GLA_CHUNK, T)) * GLA_DV), F32),
        pltpu.VMEM((R, GLA_VW), F32),
    ]
    outs = pl.pallas_call(
        functools.partial(_mix_kernel, NS=NS, T=T, pos0=pos0, emit_vn=emit_vn, n_prev=len(prev),
                          n_cast=len(cast_srcs), cast_win=win_t_next is not None),
        grid=grid, in_specs=in_specs, out_specs=out_specs, out_shape=out_shape,
        scratch_shapes=scratch, input_output_aliases=aliases,
        compiler_params=pltpu.CompilerParams(
            dimension_semantics=("arbitrary", "arbitrary"), vmem_limit_bytes=VMEM_LIMIT),
        name=f"mix_T{T}",
    )(*operands)
    n_state, n_seed = len(prev), len(seed_shapes)
    return (outs[0], tuple(outs[1:1 + n_state]), tuple(outs[1 + n_state:1 + n_state + n_seed]),
            tuple(outs[1 + n_state + n_seed:]))


def _ffn_call(layer, xp, xs, wts, final_g, *, TM, final):
    n_p, n_s = xp.shape[0] // TM, xs.shape[0] // TM
    g, wg, wu, wd = wts
    p_idx = lambda i: (jnp.minimum(i, n_p - 1), 0)
    s_idx = lambda i: (jnp.maximum(i - n_p, 0), 0)
    return pl.pallas_call(
        functools.partial(_ffn_kernel, n_prompt_steps=n_p, final=final),
        grid=(n_p + n_s,),
        in_specs=[
            pl.BlockSpec((TM, D_MODEL), p_idx),
            pl.BlockSpec((TM, D_MODEL), s_idx),
            _wspec((1, D_MODEL), layer),
            pl.BlockSpec((D_MODEL, D_FF), lambda i: (0, 0), pipeline_mode=pl.Buffered(1)),
            pl.BlockSpec((D_MODEL, D_FF), lambda i: (0, 0), pipeline_mode=pl.Buffered(1)),
            pl.BlockSpec((D_FF, D_MODEL), lambda i: (0, 0), pipeline_mode=pl.Buffered(1)),
            pl.BlockSpec((1, D_MODEL), lambda i: (0, 0)),
        ],
        out_specs=[pl.BlockSpec((TM, D_MODEL), p_idx), pl.BlockSpec((TM, D_MODEL), s_idx)],
        out_shape=[jax.ShapeDtypeStruct(xp.shape, F32), jax.ShapeDtypeStruct(xs.shape, F32)],
        compiler_params=pltpu.CompilerParams(
            dimension_semantics=("arbitrary",), vmem_limit_bytes=VMEM_LIMIT),
        name="ffn",
    )(xp, xs, g, wg, wu, wd, final_g)


def _cast_win0_kernel(wt_ref, o_ref):
    o_ref[...] = wt_ref[...].T.astype(BF16)


def _cast_win0(win_t):
    return pl.pallas_call(
        _cast_win0_kernel,
        grid=(IN_MAIN // CAST0_COLS,),
        in_specs=[pl.BlockSpec((None, CAST0_COLS, D_MODEL), lambda s: (0, s, 0))],
        out_specs=pl.BlockSpec((D_MODEL, CAST0_COLS), lambda s: (0, s)),
        out_shape=jax.ShapeDtypeStruct((D_MODEL, IN_MAIN), BF16),
        compiler_params=pltpu.CompilerParams(dimension_semantics=("arbitrary",)),
        name="cast_win0",
    )(win_t)


def kernel(x_prompt, x_sample, state_pool, state_gla, attn_norm_g, w_in, pool_w, pool_scale, sgu_norm_g, sgu_ws, sgu_b, gla_wa2, gla_ba, gla_norm_g, w_out, ffn_norm_g, w_gate, w_up, w_down, final_norm_g):
    bp, seq, _ = x_prompt.shape
    bs, dseq, _ = x_sample.shape
    assert seq % PROMPT_TILE == 0 and dseq == 8 and bs % SAMPLE_SEQS == 0

    ng = attn_norm_g[:, None, :]
    win_t = w_in.transpose(0, 2, 1)
    win = _cast_win0(win_t)
    wina = jnp.pad(win_t[:, IN_MAIN:], ((0, 0), (0, GLA_RANK_PAD - GLA_RANK), (0, 0))).astype(BF16)
    eye_g = jnp.eye(len(POOL_WINDOWS), dtype=F32)
    poolw = jnp.einsum('lgcd,gh->lgchd', pool_w, eye_g).reshape(DEPTH, POOL_WIDTH, POOL_WIDTH).astype(BF16)
    pscale = pool_scale[:, None, :]
    sgug = sgu_norm_g.reshape(DEPTH, 1, SGU_WIDTH)
    sguw_cat = sgu_ws.transpose(0, 2, 1, 3).reshape(DEPTH, SGU_CHUNK, SGU_HEADS * SGU_CHUNK)
    sgub_tile = jnp.repeat(sgu_b.transpose(0, 2, 1), SGU_HEAD_DIM, axis=-1)
    sguw_dec = jnp.repeat(sgu_ws[:, :, :dseq, :dseq].transpose(0, 3, 2, 1), SGU_HEAD_DIM, axis=-1)
    sgub_dec = sgub_tile[:, :dseq]
    wa2 = jnp.pad(gla_wa2, ((0, 0), (0, GLA_RANK_PAD - GLA_RANK), (0, 0))).astype(BF16)
    ba = gla_ba[:, None, :]
    glag = gla_norm_g[:, None, :]
    wout = w_out[0].astype(BF16)
    fng = ffn_norm_g[:, None, :]
    fin = final_norm_g[None, :]

    xp = x_prompt.reshape(bp * seq, D_MODEL)
    xs = x_sample.reshape(bs * dseq, D_MODEL)
    pool0_p = jnp.zeros((1, bp, POOL_BUF, POOL_WIDTH), F32)
    gla0_p = jnp.zeros((1, bp, GLA_HEADS, GLA_DK, GLA_DV), F32)

    st_p = (jnp.zeros((DEPTH, bp, POOL_BUF, POOL_WIDTH), F32), jnp.zeros((DEPTH, bp, GLA_HEADS, GLA_DK, GLA_DV), F32))
    seeds_s = (((DEPTH, POOL_BUF, bs, POOL_WIDTH), 2), ((DEPTH, bs, GLA_HEADS, GLA_DK, GLA_DV), 1),
               ((DEPTH, bs * dseq, SGU_WIDTH), 1))
    pool_s_in = state_pool.transpose(0, 2, 1, 3)
    st_s = None
    for l in range(DEPTH):
        common = (ng, win, wina, poolw, pscale, sgug)
        tail = (wa2, ba, glag, wout)
        nxt = l + 1 < DEPTH
        xp, st_p, seeds, cast = _mix_call(l, xp, pool0_p, gla0_p, common + (sguw_cat, sgub_tile) + tail, st_p,
                                    nseq=bp, L=seq, NS=1, T=PROMPT_TILE, pos0=0, state_layer=0, emit_vn=False,
                                    seed_shapes=seeds_s if l == 0 else (),
                                    cast_srcs=((w_gate, l), (w_up, l), (w_down, l)) + (((w_out, l + 1),) if nxt else ()),
                                    win_t_next=win_t if nxt else None)
        st_s = seeds if l == 0 else st_s
        xs, st_s, _, _ = _mix_call(l, xs, pool_s_in, state_gla, common + (sguw_dec, sgub_dec) + tail, st_s,
                                nseq=bs, L=dseq, NS=SAMPLE_SEQS, T=dseq, pos0=PAST_LEN, state_layer=l, emit_vn=True)
        xp, xs = _ffn_call(l, xp, xs, (fng,) + cast[:3], fin, TM=FFN_ROWS, final=l == DEPTH - 1)
        if nxt:
            wout, win = cast[3], cast[4]

    return (xp.reshape(bp, seq, D_MODEL), xs.reshape(bs, dseq, D_MODEL),
            st_p[0], st_p[1], st_s[0].transpose(0, 2, 1, 3), st_s[1], st_s[2].reshape(DEPTH, bs, dseq, SGU_WIDTH))
```
